```python
import math
import jax, jax.numpy as jnp
from jax import lax
import numpy as np

D_MODEL = 1024
BATCH = 2
SEQ = 8192
DEPTH = 2
DEC_BATCH = 128
DEC_SEQ = 1
PAST_LEN = 2048
PAGE_SIZE = 128

HEAD_DIM = 64
MIX_WIDTH = D_MODEL
N_HEADS = MIX_WIDTH // HEAD_DIM
H_RET = N_HEADS // 2
H_MOBA = N_HEADS - H_RET
W_RET = H_RET * HEAD_DIM
W_MOBA = H_MOBA * HEAD_DIM
IN_COLS = 4 * W_RET + 3 * W_MOBA
RET_CHUNK = 128
ROPE_BASE = 10000.0
MOBA_BLOCK = 256
MOBA_TOPK = 3
MOBA_Q_CHUNK = 64
D_FF = 2816
N_EXPERTS = 8
TOP_K = 2
D_FF_EXPERT = 3584
N_DENSE = (DEPTH + 1) // 2
N_MOE = DEPTH // 2
EPS = 1e-6

kernel_name = "hymba_retnet_moba_moe_step"


def rmsnorm(x, g):
    x32 = x.astype(jnp.float32)
    y = x32 * lax.rsqrt(jnp.mean(x32 * x32, axis=-1, keepdims=True) + EPS)
    return (y * g.astype(jnp.float32)).astype(x.dtype)


def group_norm_heads(o, g):
    mu = jnp.mean(o, axis=-1, keepdims=True)
    var = jnp.mean(jnp.square(o - mu), axis=-1, keepdims=True)
    return (o - mu) * lax.rsqrt(var + EPS) * g.astype(jnp.float32)[None, :, None, :]


def split_heads(t, n_heads):
    b, s, _ = t.shape
    return t.reshape(b, s, n_heads, HEAD_DIM).transpose(0, 2, 1, 3)


def merge_heads(t):
    b, h, s, d = t.shape
    return t.transpose(0, 2, 1, 3).reshape(b, s, h * d)


def rope(x, pos):
    half = HEAD_DIM // 2
    inv = ROPE_BASE ** (-jnp.arange(half, dtype=jnp.float32) / half)
    ang = pos.astype(jnp.float32)[:, None] * inv[None, :]
    cos, sin = jnp.cos(ang), jnp.sin(ang)
    x32 = x.astype(jnp.float32)
    x1, x2 = x32[..., :half], x32[..., half:]
    return jnp.concatenate([x1 * cos - x2 * sin, x1 * sin + x2 * cos], axis=-1).astype(x.dtype)


def ret_log_decay():
    return jnp.log1p(-jnp.exp2(-5.0 - jnp.arange(H_RET, dtype=jnp.float32)))


def retention_chunk(q, k, v, s0):
    c = q.shape[2]
    ld = ret_log_decay()
    i = jnp.arange(c, dtype=jnp.float32)
    diff = i[:, None] - i[None, :]
    dmask = jnp.where(diff >= 0, jnp.exp(ld[:, None, None] * jnp.maximum(diff, 0.0)), 0.0)
    q32, k32, v32 = q.astype(jnp.float32), k.astype(jnp.float32), v.astype(jnp.float32)
    scores = jnp.einsum('bhid,bhjd->bhij', q32, k32) * dmask[None]
    cross = jnp.exp(ld[:, None] * (i + 1.0)[None, :])
    o = jnp.einsum('bhij,bhjd->bhid', scores, v32) + jnp.einsum('bhid,bhde->bhie', q32, s0) * cross[None, :, :, None]
    kdec = jnp.exp(ld[:, None] * (c - 1.0 - i)[None, :])
    s_new = jnp.exp(ld * c)[None, :, None, None] * s0 + jnp.einsum('bhjd,bhje->bhde', k32 * kdec[None, :, :, None], v32)
    return o, s_new


def retention_prompt(q, k, v):
    b, h, t, d = q.shape
    nc = t // RET_CHUNK
    chunks = lambda a: a.reshape(b, h, nc, RET_CHUNK, d).transpose(2, 0, 1, 3, 4)

    def step(s, qkv):
        o, s2 = retention_chunk(qkv[0], qkv[1], qkv[2], s)
        return s2, o

    s_fin, o = lax.scan(step, jnp.zeros((b, h, d, d), jnp.float32), (chunks(q), chunks(k), chunks(v)))
    return o.transpose(1, 2, 0, 3, 4).reshape(b, h, t, d), s_fin


def moba_attend(q, q_pos, k_all, v_all):
    b, h, length, d = k_all.shape
    nq = q.shape[2]
    nb = length // MOBA_BLOCK
    kb = k_all.reshape(b, h, nb, MOBA_BLOCK, d)
    vb = v_all.reshape(b, h, nb, MOBA_BLOCK, d)
    q32 = q.astype(jnp.float32)
    k_mean = jnp.mean(kb.astype(jnp.float32), axis=3)
    gate = jnp.einsum('bhqd,bhnd->bhqn', q32, k_mean)
    cur = q_pos // MOBA_BLOCK
    past = jnp.arange(nb)[None, :] < cur[:, None]
    gate = jnp.where(past[None, None], gate, -jnp.inf)
    k_sel = min(MOBA_TOPK, nb)
    _, top_idx = lax.top_k(gate, k_sel)
    own = jnp.broadcast_to(cur[None, None, :, None], (b, h, nq, 1)).astype(top_idx.dtype)
    idx = jnp.concatenate([top_idx, own], axis=-1)
    n_sel = k_sel + 1
    is_own = jnp.arange(n_sel) == k_sel
    valid_blk = is_own | (idx < cur[None, None, :, None])
    bi = jnp.arange(b)[:, None, None, None]
    hi = jnp.arange(h)[None, :, None, None]
    kg = kb[bi, hi, idx]
    vg = vb[bi, hi, idx]
    key_pos = idx[..., None] * MOBA_BLOCK + jnp.arange(MOBA_BLOCK)
    mask = valid_blk[..., None] & (key_pos <= q_pos[None, None, :, None, None])
    s = jnp.einsum('bhqd,bhqnsd->bhqns', q32, kg.astype(jnp.float32)) * (d ** -0.5)
    s = jnp.where(mask, s, -jnp.inf).reshape(b, h, nq, n_sel * MOBA_BLOCK)
    p = jax.nn.softmax(s, axis=-1).reshape(b, h, nq, n_sel, MOBA_BLOCK)
    o = jnp.einsum('bhqns,bhqnsd->bhqd', p, vg.astype(jnp.float32))
    return o.astype(q.dtype)


def moba_prompt(q, k, v):
    b, h, t, d = q.shape
    pad = (-t) % MOBA_BLOCK
    if pad > 0:
        z = jnp.zeros((b, h, pad, d), k.dtype)
        k = jnp.concatenate([k, z], axis=2)
        v = jnp.concatenate([v, z], axis=2)
    nqc = t // MOBA_Q_CHUNK
    qc = q.reshape(b, h, nqc, MOBA_Q_CHUNK, d).transpose(2, 0, 1, 3, 4)
    posc = jnp.arange(t, dtype=jnp.int32).reshape(nqc, MOBA_Q_CHUNK)
    o = lax.map(lambda a: moba_attend(a[0], a[1], k, v), (qc, posc))
    return o.transpose(1, 2, 0, 3, 4).reshape(b, h, t, d)


def moba_sample(q, k_new, v_new, cache_k_l, cache_v_l, page_table):
    db, h, t_new, d = q.shape
    past_len = page_table.shape[1] * PAGE_SIZE
    gather = lambda c: c[page_table].reshape(db, past_len, H_MOBA, HEAD_DIM).transpose(0, 2, 1, 3).astype(k_new.dtype)
    pad = (-(past_len + t_new)) % MOBA_BLOCK
    z = jnp.zeros((db, h, pad, d), k_new.dtype)
    k_all = jnp.concatenate([gather(cache_k_l), k_new, z], axis=2)
    v_all = jnp.concatenate([gather(cache_v_l), v_new, z], axis=2)
    q_pos = past_len + jnp.arange(t_new, dtype=jnp.int32)
    return moba_attend(q, q_pos, k_all, v_all)


def swiglu(h, wg, wu, wd):
    return (jax.nn.silu(h @ wg) * (h @ wu)) @ wd


def moe_swiglu(h, w_router, wg, wu, wd):
    b, t, d = h.shape
    hf = h.reshape(b * t, d)
    logits = (hf @ w_router).astype(jnp.float32)
    top_v, top_i = lax.top_k(logits, TOP_K)
    w = jax.nn.softmax(top_v, axis=-1)
    combine = jnp.sum(jax.nn.one_hot(top_i, N_EXPERTS, dtype=jnp.float32) * w[..., None], axis=1)

    def body(acc, e):
        wg_e, wu_e, wd_e, c_e = e
        return acc + c_e[:, None].astype(hf.dtype) * swiglu(hf, wg_e, wu_e, wd_e), None

    acc, _ = lax.scan(body, jnp.zeros_like(hf), (wg, wu, wd, combine.T))
    return acc.reshape(b, t, d)


def layer_step(l, x, pos, ret_s0, past_k, past_v, page_table,
               g_mix, w_in, ret_gn, q_norm_g, k_norm_g, w_out, g_ffn,
               w_ffn_gate, w_ffn_up, w_ffn_down, w_router, w_exp_gate, w_exp_up, w_exp_down):
    h = rmsnorm(x, g_mix[l])
    z = jnp.einsum('btd,dc->btc', h, w_in[l])
    cuts = [W_RET, 2 * W_RET, 3 * W_RET, 4 * W_RET, 4 * W_RET + W_MOBA, 4 * W_RET + 2 * W_MOBA]
    qr, kr, vr, gr, qm, km, vm = jnp.split(z, cuts, axis=-1)
    qr = rope(split_heads(qr, H_RET), pos)
    kr = rope(split_heads(kr, H_RET), pos) * (HEAD_DIM ** -0.5)
    vr = split_heads(vr, H_RET)
    if ret_s0 is None:
        o_r, s_new = retention_prompt(qr, kr, vr)
    else:
        o_r, s_new = retention_chunk(qr, kr, vr, ret_s0.astype(jnp.float32))
    o_r = merge_heads(group_norm_heads(o_r, ret_gn[l])).astype(x.dtype) * jax.nn.silu(gr)
    qm = rmsnorm(split_heads(qm, H_MOBA), q_norm_g[l])
    km = rmsnorm(split_heads(km, H_MOBA), k_norm_g[l])
    vm = split_heads(vm, H_MOBA)
    if past_k is None:
        o_m = moba_prompt(qm, km, vm)
    else:
        o_m = moba_sample(qm, km, vm, past_k, past_v, page_table)
    o_m = merge_heads(o_m)
    x = x + jnp.einsum('btc,cd->btd', jnp.concatenate([o_r, o_m], axis=-1), w_out[l])
    h2 = rmsnorm(x, g_ffn[l])
    i = l // 2
    if l % 2 == 0:
        x = x + swiglu(h2, w_ffn_gate[i], w_ffn_up[i], w_ffn_down[i])
    else:
        x = x + moe_swiglu(h2, w_router[i], w_exp_gate[i], w_exp_up[i], w_exp_down[i])
    return x, s_new.astype(x.dtype), km.transpose(0, 2, 1, 3), vm.transpose(0, 2, 1, 3)


def setup_inputs(seed: int = 0) -> dict:
    key = jax.random.key(seed)
    ks = jax.random.split(key, 24)
    n_pages = PAST_LEN // PAGE_SIZE
    n_used = DEC_BATCH * n_pages
    n_phys = n_used + n_used // 4
    f32 = jnp.float32
    nrm = lambda k, shape, scale: jax.random.normal(k, shape, f32) * scale
    page_table = jax.random.permutation(ks[0], n_phys)[:n_used].reshape(DEC_BATCH, n_pages).astype(jnp.int32)
    return {
        'x_prompt': nrm(ks[1], (BATCH, SEQ, D_MODEL), 1.0),
        'x_sample': nrm(ks[2], (DEC_BATCH, DEC_SEQ, D_MODEL), 1.0),
        'cache_k': nrm(ks[3], (DEPTH, n_phys, PAGE_SIZE, H_MOBA, HEAD_DIM), 1.0),
        'cache_v': nrm(ks[4], (DEPTH, n_phys, PAGE_SIZE, H_MOBA, HEAD_DIM), 1.0),
        'state_ret': nrm(ks[5], (DEPTH, DEC_BATCH, H_RET, HEAD_DIM, HEAD_DIM), 0.5),
        'page_table': page_table,
        'g_mix': 1.0 + nrm(ks[6], (DEPTH, D_MODEL), 0.02),
        'w_in': nrm(ks[7], (DEPTH, D_MODEL, IN_COLS), D_MODEL ** -0.5),
        'ret_gn': 1.0 + nrm(ks[8], (DEPTH, H_RET, HEAD_DIM), 0.02),
        'q_norm_g': 1.0 + nrm(ks[9], (DEPTH, HEAD_DIM), 0.02),
        'k_norm_g': 1.0 + nrm(ks[10], (DEPTH, HEAD_DIM), 0.02),
        'w_out': nrm(ks[11], (DEPTH, MIX_WIDTH, D_MODEL), MIX_WIDTH ** -0.5),
        'g_ffn': 1.0 + nrm(ks[12], (DEPTH, D_MODEL), 0.02),
        'w_ffn_gate': nrm(ks[13], (N_DENSE, D_MODEL, D_FF), D_MODEL ** -0.5),
        'w_ffn_up': nrm(ks[14], (N_DENSE, D_MODEL, D_FF), D_MODEL ** -0.5),
        'w_ffn_down': nrm(ks[15], (N_DENSE, D_FF, D_MODEL), D_FF ** -0.5),
        'w_router': nrm(ks[16], (N_MOE, D_MODEL, N_EXPERTS), D_MODEL ** -0.5),
        'w_exp_gate': nrm(ks[17], (N_MOE, N_EXPERTS, D_MODEL, D_FF_EXPERT), D_MODEL ** -0.5),
        'w_exp_up': nrm(ks[18], (N_MOE, N_EXPERTS, D_MODEL, D_FF_EXPERT), D_MODEL ** -0.5),
        'w_exp_down': nrm(ks[19], (N_MOE, N_EXPERTS, D_FF_EXPERT, D_MODEL), D_FF_EXPERT ** -0.5),
    }


def reference(x_prompt, x_sample, cache_k, cache_v, state_ret, page_table,
              g_mix, w_in, ret_gn, q_norm_g, k_norm_g, w_out, g_ffn,
              w_ffn_gate, w_ffn_up, w_ffn_down, w_router, w_exp_gate, w_exp_up, w_exp_down):
    weights = (g_mix, w_in, ret_gn, q_norm_g, k_norm_g, w_out, g_ffn,
               w_ffn_gate, w_ffn_up, w_ffn_down, w_router, w_exp_gate, w_exp_up, w_exp_down)
    pos_p = jnp.arange(x_prompt.shape[1], dtype=jnp.int32)
    pos_s = page_table.shape[1] * PAGE_SIZE + jnp.arange(x_sample.shape[1], dtype=jnp.int32)
    xp, xs = x_prompt, x_sample
    sp_l, kp_l, vp_l, ss_l, ks_l, vs_l = [], [], [], [], [], []
    for l in range(DEPTH):
        xp, sp, kp, vp = layer_step(l, xp, pos_p, None, None, None, None, *weights)
        xs, ss, kss, vss = layer_step(l, xs, pos_s, state_ret[l], cache_k[l], cache_v[l], page_table, *weights)
        sp_l.append(sp); kp_l.append(kp); vp_l.append(vp)
        ss_l.append(ss); ks_l.append(kss); vs_l.append(vss)
    new_state_ret_prompt = jnp.stack(sp_l)
    new_k_prompt = jnp.stack(kp_l)
    new_v_prompt = jnp.stack(vp_l)
    new_state_ret_sample = jnp.stack(ss_l)
    new_k_sample = jnp.stack(ks_l)
    new_v_sample = jnp.stack(vs_l)
    return (xp, xs, new_state_ret_prompt, new_k_prompt, new_v_prompt,
            new_state_ret_sample, new_k_sample, new_v_sample)
```

```python
import functools
import math

import jax
import jax.numpy as jnp
from jax import lax
from jax.experimental import pallas as pl
from jax.experimental.pallas import tpu as pltpu

F32 = jnp.float32
BF16 = jnp.bfloat16

HEAD_DIM = 64
H_RET = 8
H_MOBA = 8
W_RET = H_RET * HEAD_DIM
W_MOBA = H_MOBA * HEAD_DIM
PAGE_SIZE = 128
ROPE_BASE = 10000.0
MOBA_BLOCK = 256
MOBA_TOPK = 3
TOP_K = 2
EPS = 1e-6
RET_CHUNK = 256
LANES = 128
NEG = -1e30
VMEM_LIMIT = 56 * 1024 * 1024

HIGHEST = lax.Precision.HIGHEST


def _cparams(sem):
    return pltpu.CompilerParams(dimension_semantics=sem, vmem_limit_bytes=VMEM_LIMIT)


def _silu(x):
    return x / (1.0 + jnp.exp(-x))


def _dot(a, b):
    return jnp.dot(a, b, preferred_element_type=F32)


def _dot_nt(a, b, precision=None):
    return lax.dot_general(a, b, (((1,), (1,)), ((), ())), precision=precision,
                           preferred_element_type=F32)


def _dot_tn(a, b):
    return lax.dot_general(a, b, (((0,), (0,)), ((), ())), preferred_element_type=F32)


def _split3(x):
    hi = x.astype(BF16)
    r = x - hi.astype(F32)
    mid = r.astype(BF16)
    lo = (r - mid.astype(F32)).astype(BF16)
    return hi, mid, lo


def _proj_in_kernel(x_ref, g_ref, w_ref, cos_ref, sin_ref, qg_ref, kg_ref, hm_ref,
                    qr_ref, kr_ref, vr_ref, gr_ref, qm_ref, km_ref, vm_ref,
                    kmb_ref, vmb_ref, *rest, with_kmean):
    x = x_ref[...]
    tm = x.shape[0]
    ms = jnp.mean(x * x, axis=-1, keepdims=True)
    h = (x * lax.rsqrt(ms + EPS) * g_ref[...]).astype(BF16)

    def proj(c0, width):
        return _dot(h, w_ref[:, c0:c0 + width])

    cos = cos_ref[...]
    sin = sin_ref[...]
    lane = lax.broadcasted_iota(jnp.int32, (tm, LANES), 1)
    first_half = (lane % HEAD_DIM) < (HEAD_DIM // 2)

    def rope(z):
        outs = []
        for c in range(z.shape[1] // LANES):
            zc = z[:, c * LANES:(c + 1) * LANES]
            rot = jnp.where(first_half,
                            pltpu.roll(zc, LANES - HEAD_DIM // 2, 1),
                            pltpu.roll(zc, HEAD_DIM // 2, 1))
            outs.append(zc * cos + rot * sin)
        return jnp.concatenate(outs, axis=1)

    hm = hm_ref[...]

    def head_norm(z, g):
        sq = z * z
        hi = sq.astype(BF16)
        lo = (sq - hi.astype(F32)).astype(BF16)
        msq = _dot(hi, hm) + _dot(lo, hm)
        return z * lax.rsqrt(msq + EPS) * g

    qr_ref[...] = rope(proj(0, W_RET)).astype(BF16)
    kr_ref[...] = (rope(proj(W_RET, W_RET)) * (HEAD_DIM ** -0.5)).astype(BF16)
    vr_ref[...] = proj(2 * W_RET, W_RET).astype(BF16)
    gr_ref[...] = proj(3 * W_RET, W_RET).astype(BF16)
    base = 4 * W_RET
    qm_ref[...] = head_norm(proj(base, W_MOBA), qg_ref[...])
    km = head_norm(proj(base + W_MOBA, W_MOBA), kg_ref[...])
    km_ref[...] = km
    kmb_ref[...] = km.astype(BF16)
    vm = proj(base + 2 * W_MOBA, W_MOBA)
    vm_ref[...] = vm
    vmb_ref[...] = vm.astype(BF16)
    if with_kmean:
        kmean_ref = rest[0]
        for r in range(tm // MOBA_BLOCK):
            kmean_ref[0, r:r + 1, :] = jnp.mean(
                km[r * MOBA_BLOCK:(r + 1) * MOBA_BLOCK], axis=0, keepdims=True)


def proj_in(x, g, w_bf, cos, sin, qg, kg, hm, *, tm, with_kmean):
    n, d = x.shape
    n_pos_tiles = cos.shape[0] // tm
    row = lambda i: (i, 0)
    fixed = lambda i: (0, 0)
    wide = lambda dt: jax.ShapeDtypeStruct((n, W_RET), dt)
    out_shape = [wide(BF16), wide(BF16), wide(BF16), wide(BF16),
                 wide(F32), wide(F32), wide(F32), wide(BF16), wide(BF16)]
    out_specs = [pl.BlockSpec((tm, W_RET), row)] * 9
    if with_kmean:
        nb = tm // MOBA_BLOCK
        out_shape.append(jax.ShapeDtypeStruct((n // tm, nb, W_MOBA), F32))
        out_specs.append(pl.BlockSpec((1, nb, W_MOBA), lambda i: (i, 0, 0)))
    return pl.pallas_call(
        functools.partial(_proj_in_kernel, with_kmean=with_kmean),
        grid=(n // tm,),
        in_specs=[
            pl.BlockSpec((tm, d), row),
            pl.BlockSpec((1, d), fixed),
            pl.BlockSpec(w_bf.shape, fixed),
            pl.BlockSpec((tm, LANES), lambda i: (i % n_pos_tiles, 0)),
            pl.BlockSpec((tm, LANES), lambda i: (i % n_pos_tiles, 0)),
            pl.BlockSpec((1, W_MOBA), fixed),
            pl.BlockSpec((1, W_MOBA), fixed),
            pl.BlockSpec((W_MOBA, W_MOBA), fixed),
        ],
        out_specs=out_specs,
        out_shape=out_shape,
        compiler_params=_cparams(("parallel",)),
        name="proj_in",
    )(x, g, w_bf, cos, sin, qg, kg, hm)


def _ret_prompt_kernel(q_ref, k_ref, v_ref, g_ref, gn_ref, dm_ref, cross_ref, kdec_ref,
                       gc_ref, o_ref, sfin_ref, s_scr):
    c = pl.program_id(1)

    @pl.when(c == 0)
    def _():
        s_scr[...] = jnp.zeros_like(s_scr)

    outs = []
    for h in range(H_RET):
        sl = slice(h * HEAD_DIM, (h + 1) * HEAD_DIM)
        q = q_ref[0, :, sl]
        k = k_ref[0, :, sl]
        v = v_ref[0, :, sl]
        s0 = s_scr[h]
        scores = _dot_nt(q, k) * dm_ref[h]
        o = _dot(scores.astype(BF16), v) + _dot(q, s0.astype(BF16)) * cross_ref[h]
        kd = (k.astype(F32) * kdec_ref[h]).astype(BF16)
        s_scr[h] = gc_ref[h] * s0 + _dot_tn(kd, v)
        mu = jnp.mean(o, axis=-1, keepdims=True)
        oc = o - mu
        var = jnp.mean(oc * oc, axis=-1, keepdims=True)
        on = oc * lax.rsqrt(var + EPS) * gn_ref[:, sl]
        outs.append(on * _silu(g_ref[0, :, sl].astype(F32)))
    o_ref[0] = jnp.concatenate(outs, axis=1).astype(BF16)

    @pl.when(c == pl.num_programs(1) - 1)
    def _():
        sfin_ref[0] = s_scr[...]


def _ret_log_decay():
    return jnp.log1p(-jnp.exp2(-5.0 - jnp.arange(H_RET, dtype=F32)))


def retention_prompt(qr, kr, vr, gr, gn):
    b, t, w = qr.shape
    c = min(RET_CHUNK, t)
    ld = _ret_log_decay()
    i = jnp.arange(c, dtype=F32)
    diff = i[:, None] - i[None, :]
    dmask = jnp.where(diff >= 0, jnp.exp(ld[:, None, None] * jnp.maximum(diff, 0.0)), 0.0)
    bc = lambda a: jnp.broadcast_to(a[:, :, None], (H_RET, c, HEAD_DIM))
    cross = bc(jnp.exp(ld[:, None] * (i + 1.0)[None, :]))
    kdec = bc(jnp.exp(ld[:, None] * (c - 1.0 - i)[None, :]))
    gc = jnp.broadcast_to(jnp.exp(ld * c)[:, None, None], (H_RET, HEAD_DIM, HEAD_DIM))
    tile = pl.BlockSpec((1, c, w), lambda bi, ci: (bi, ci, 0))
    const3 = lambda shape: pl.BlockSpec(shape, lambda bi, ci: (0, 0, 0))
    return pl.pallas_call(
        _ret_prompt_kernel,
        grid=(b, t // c),
        in_specs=[tile, tile, tile, tile,
                  pl.BlockSpec((1, w), lambda bi, ci: (0, 0)),
                  const3((H_RET, c, c)), const3((H_RET, c, HEAD_DIM)),
                  const3((H_RET, c, HEAD_DIM)), const3((H_RET, HEAD_DIM, HEAD_DIM))],
        out_specs=[tile,
                   pl.BlockSpec((1, H_RET, HEAD_DIM, HEAD_DIM), lambda bi, ci: (bi, 0, 0, 0))],
        out_shape=[jax.ShapeDtypeStruct((b, t, w), BF16),
                   jax.ShapeDtypeStruct((b, H_RET, HEAD_DIM, HEAD_DIM), F32)],
        scratch_shapes=[pltpu.VMEM((H_RET, HEAD_DIM, HEAD_DIM), F32)],
        compiler_params=_cparams(("parallel", "arbitrary")),
        name="retention_prompt",
    )(qr, kr, vr, gr, gn, dmask, cross, kdec, gc)


def _moba_prompt_kernel(q_ref, k_ref, v_ref, km_ref, o_ref):
    qi = pl.program_id(2)
    nb = km_ref.shape[1]
    bs = MOBA_BLOCK
    scale = HEAD_DIM ** -0.5
    blk = lax.broadcasted_iota(jnp.int32, (bs, nb), 1)
    row = lax.broadcasted_iota(jnp.int32, (bs, bs), 0)
    col = lax.broadcasted_iota(jnp.int32, (bs, bs), 1)
    outs = []
    for hh in range(LANES // HEAD_DIM):
        sl = slice(hh * HEAD_DIM, (hh + 1) * HEAD_DIM)
        qf = q_ref[0, :, sl]
        qb = qf.astype(BF16)
        gate = _dot_nt(qf, km_ref[0, :, sl], precision=HIGHEST)
        g = jnp.where(blk < qi, gate, -jnp.inf)
        sel = jnp.zeros((bs, nb), F32)
        for _ in range(min(MOBA_TOPK, nb)):
            mx = jnp.max(g, axis=1, keepdims=True)
            idx = jnp.min(jnp.where(g == mx, blk, nb), axis=1, keepdims=True)
            pick = blk == idx
            sel = jnp.where(pick, 1.0, sel)
            g = jnp.where(pick, -jnp.inf, g)
        sel = jnp.where(blk < qi, sel, 0.0)

        def update(carry, s, vj):
            m, l, acc = carry
            m_new = jnp.maximum(m, jnp.max(s, axis=1, keepdims=True))
            alpha = jnp.exp(m - m_new)
            p = jnp.exp(s - m_new)
            l = alpha * l + jnp.sum(p, axis=1, keepdims=True)
            acc = alpha * acc + _dot(p.astype(BF16), vj)
            return m_new, l, acc

        def body(j, carry):
            start = pl.multiple_of(j * bs, bs)
            kj = k_ref[0, pl.ds(start, bs), sl]
            vj = v_ref[0, pl.ds(start, bs), sl]
            s = _dot_nt(qb, kj) * scale
            selj = jnp.sum(jnp.where(blk == j, sel, 0.0), axis=1, keepdims=True)
            s = jnp.where(selj > 0.0, s, NEG)
            return update(carry, s, vj)

        init = (jnp.full((bs, 1), NEG, F32), jnp.zeros((bs, 1), F32),
                jnp.zeros((bs, HEAD_DIM), F32))
        carry = lax.fori_loop(0, qi, body, init)
        start = pl.multiple_of(qi * bs, bs)
        kj = k_ref[0, pl.ds(start, bs), sl]
        vj = v_ref[0, pl.ds(start, bs), sl]
        s = jnp.where(col <= row, _dot_nt(qb, kj) * scale, NEG)
        _, l, acc = update(carry, s, vj)
        outs.append(acc / l)
    o_ref[0] = jnp.concatenate(outs, axis=1).astype(BF16)


def moba_prompt(qm, kmb, vmb, kmean):
    b, t, w = qm.shape
    nb = t // MOBA_BLOCK
    return pl.pallas_call(
        _moba_prompt_kernel,
        grid=(b, w // LANES, nb),
        in_specs=[
            pl.BlockSpec((1, MOBA_BLOCK, LANES), lambda bi, hp, qi: (bi, qi, hp)),
            pl.BlockSpec((1, t, LANES), lambda bi, hp, qi: (bi, 0, hp)),
            pl.BlockSpec((1, t, LANES), lambda bi, hp, qi: (bi, 0, hp)),
            pl.BlockSpec((1, nb, LANES), lambda bi, hp, qi: (bi, 0, hp)),
        ],
        out_specs=pl.BlockSpec((1, MOBA_BLOCK, LANES), lambda bi, hp, qi: (bi, qi, hp)),
        out_shape=jax.ShapeDtypeStruct((b, t, w), BF16),
        compiler_params=_cparams(("parallel", "parallel", "arbitrary")),
        name="moba_prompt",
    )(qm, kmb, vmb, kmean)


def _ret_sample_kernel(q_ref, k_ref, v_ref, g_ref, gam_ref, gn_ref, s0_ref, e1_ref, e2_ref,
                       o_ref, s_ref):
    e1 = e1_ref[...]
    e2 = e2_ref[...]

    def expand(x, e):
        hi, mid, lo = _split3(x)
        return _dot(hi, e) + _dot(mid, e) + _dot(lo, e)

    kexp = expand(k_ref[...], e1)
    vexp = expand(v_ref[...], e2)
    qexp = expand(q_ref[...], e1)
    s_new = gam_ref[:, :1] * s0_ref[...] + kexp * vexp
    s_ref[...] = s_new
    hi, mid, lo = _split3(qexp * s_new)
    o = _dot_nt(hi, e2) + _dot_nt(mid, e2) + _dot_nt(lo, e2)
    mu = jnp.mean(o, axis=-1, keepdims=True)
    oc = o - mu
    var = jnp.mean(oc * oc, axis=-1, keepdims=True)
    o_ref[...] = oc * lax.rsqrt(var + EPS) * gn_ref[...] * _silu(g_ref[...])


def retention_sample(q, k, v, gr, gn, s0):
    n = q.shape[0]
    r = n * H_RET
    dd = HEAD_DIM * HEAD_DIM
    rows = lambda a: a.reshape(r, HEAD_DIM)
    gam = jnp.tile(jnp.broadcast_to(jnp.exp(_ret_log_decay())[:, None], (H_RET, HEAD_DIM)), (n, 1))
    gnr = jnp.tile(gn.reshape(H_RET, HEAD_DIM), (n, 1))
    lane = jnp.arange(dd)
    e1 = (lane[None, :] // HEAD_DIM == jnp.arange(HEAD_DIM)[:, None]).astype(BF16)
    e2 = (lane[None, :] % HEAD_DIM == jnp.arange(HEAD_DIM)[:, None]).astype(BF16)
    tr = min(128, r)
    small = pl.BlockSpec((tr, HEAD_DIM), lambda i: (i, 0))
    big = pl.BlockSpec((tr, dd), lambda i: (i, 0))
    emat = pl.BlockSpec((HEAD_DIM, dd), lambda i: (0, 0))
    o, s_new = pl.pallas_call(
        _ret_sample_kernel,
        grid=(r // tr,),
        in_specs=[small, small, small, small, small, small, big, emat, emat],
        out_specs=[small, big],
        out_shape=[jax.ShapeDtypeStruct((r, HEAD_DIM), F32),
                   jax.ShapeDtypeStruct((r, dd), F32)],
        compiler_params=_cparams(("parallel",)),
        name="retention_sample",
    )(rows(q), rows(k), rows(v), rows(gr), gam, gnr, s0.reshape(r, dd), e1, e2)
    return o.reshape(n, W_RET), s_new.reshape(n, H_RET, HEAD_DIM, HEAD_DIM)


def _moba_sample_kernel(pt_ref, q_ref, kn_ref, vn_ref, *refs, n_pages):
    del pt_ref
    k_refs = refs[:n_pages]
    v_refs = refs[n_pages:2 * n_pages]
    o_ref = refs[2 * n_pages]
    scale = HEAD_DIM ** -0.5
    ppb = MOBA_BLOCK // PAGE_SIZE
    nbp = n_pages // ppb
    head = lax.broadcasted_iota(jnp.int32, (H_MOBA, W_MOBA), 0)
    lane = lax.broadcasted_iota(jnp.int32, (H_MOBA, W_MOBA), 1)
    own_head = (lane // HEAD_DIM) == head
    qrows = jnp.where(own_head, q_ref[0], 0.0)
    s_pages = [_dot_nt(qrows, k_refs[p][0], precision=HIGHEST) for p in range(n_pages)]
    gates = []
    for j in range(nbp):
        tot = s_pages[j * ppb]
        for p in range(1, ppb):
            tot = tot + s_pages[j * ppb + p]
        gates.append(jnp.sum(tot, axis=1, keepdims=True) * (1.0 / MOBA_BLOCK))
    k_sel = min(MOBA_TOPK, nbp + 1)
    sels = []
    for j in range(nbp):
        rank = jnp.zeros((H_MOBA, 1), F32)
        for i in range(nbp):
            if i == j:
                continue
            ahead = (gates[i] > gates[j]) | ((gates[i] == gates[j]) & (i < j))
            rank = rank + jnp.where(ahead, 1.0, 0.0)
        sels.append(rank < k_sel)
    s_own = jnp.sum(qrows * kn_ref[0], axis=1, keepdims=True) * scale
    m = s_own
    s_msk = []
    for p in range(n_pages):
        sp = jnp.where(sels[p // ppb], s_pages[p] * scale, NEG)
        s_msk.append(sp)
        m = jnp.maximum(m, jnp.max(sp, axis=1, keepdims=True))
    p_own = jnp.exp(s_own - m)
    l = p_own
    acc = p_own * jnp.broadcast_to(vn_ref[0], (H_MOBA, W_MOBA))
    for p in range(n_pages):
        pp = jnp.exp(s_msk[p] - m)
        l = l + jnp.sum(pp, axis=1, keepdims=True)
        acc = acc + jnp.dot(pp, v_refs[p][0], precision=HIGHEST, preferred_element_type=F32)
    o8 = acc / l
    o_ref[0] = jnp.sum(jnp.where(own_head, o8, 0.0), axis=0, keepdims=True)


def moba_sample(q, k_new, v_new, cache_k3, cache_v3, page_table, page_base):
    n, w = q.shape
    n_pages = page_table.shape[1]
    three = lambda a: a.reshape(n, 1, w)
    vec = pl.BlockSpec((1, 1, w), lambda s, pt: (s, 0, 0))

    def page_spec(p):
        return pl.BlockSpec((1, PAGE_SIZE, w), lambda s, pt: (page_base + pt[s, p], 0, 0))

    pages = [page_spec(p) for p in range(n_pages)]
    out = pl.pallas_call(
        functools.partial(_moba_sample_kernel, n_pages=n_pages),
        grid_spec=pltpu.PrefetchScalarGridSpec(
            num_scalar_prefetch=1,
            grid=(n,),
            in_specs=[vec, vec, vec] + pages + pages,
            out_specs=vec,
        ),
        out_shape=jax.ShapeDtypeStruct((n, 1, w), F32),
        compiler_params=_cparams(("arbitrary",)),
        name="moba_sample",
    )(page_table, three(q), three(k_new), three(v_new),
      *([cache_k3] * n_pages), *([cache_v3] * n_pages))
    return out.reshape(n, w)


def _out_proj_kernel(or_ref, om_ref, x_ref, w_ref, g_ref, x1_ref, h2_ref):
    x1 = (x_ref[...] + _dot(or_ref[...], w_ref[:W_RET, :])
          + _dot(om_ref[...], w_ref[W_RET:, :]))
    x1_ref[...] = x1
    ms = jnp.mean(x1 * x1, axis=-1, keepdims=True)
    h2_ref[...] = x1 * lax.rsqrt(ms + EPS) * g_ref[...]


def out_proj(o_r, o_m, x, w_bf, g, *, tm):
    n, d = x.shape
    row = lambda i: (i, 0)
    fixed = lambda i: (0, 0)
    return pl.pallas_call(
        _out_proj_kernel,
        grid=(n // tm,),
        in_specs=[pl.BlockSpec((tm, W_RET), row), pl.BlockSpec((tm, W_MOBA), row),
                  pl.BlockSpec((tm, d), row), pl.BlockSpec(w_bf.shape, fixed),
                  pl.BlockSpec((1, d), fixed)],
        out_specs=[pl.BlockSpec((tm, d), row), pl.BlockSpec((tm, d), row)],
        out_shape=[jax.ShapeDtypeStruct((n, d), F32), jax.ShapeDtypeStruct((n, d), F32)],
        compiler_params=_cparams(("parallel",)),
        name="out_proj",
    )(o_r, o_m, x, w_bf, g)


def _ffn_kernel(x1_ref, h_ref, wg_ref, wu_ref, wd_ref, o_ref, acc_ref):
    f = pl.program_id(1)

    @pl.when(f == 0)
    def _():
        acc_ref[...] = x1_ref[...]

    h = h_ref[...].astype(BF16)
    a = _silu(_dot(h, wg_ref[...])) * _dot(h, wu_ref[...])
    acc_ref[...] += _dot(a.astype(BF16), wd_ref[...])

    @pl.when(f == pl.num_programs(1) - 1)
    def _():
        o_ref[...] = acc_ref[...]


def _ff_tile(d_ff):
    for tf in (512, 256, 128):
        if d_ff % tf == 0:
            return tf
    return d_ff


def ffn_dense(x1, h2, wg, wu, wd, *, tm):
    n, d = x1.shape
    d_ff = wg.shape[1]
    tf = _ff_tile(d_ff)
    row = lambda i, f: (i, 0)
    return pl.pallas_call(
        _ffn_kernel,
        grid=(n // tm, d_ff // tf),
        in_specs=[pl.BlockSpec((tm, d), row), pl.BlockSpec((tm, d), row),
                  pl.BlockSpec((d, tf), lambda i, f: (0, f)),
                  pl.BlockSpec((d, tf), lambda i, f: (0, f)),
                  pl.BlockSpec((tf, d), lambda i, f: (f, 0))],
        out_specs=pl.BlockSpec((tm, d), row),
        out_shape=jax.ShapeDtypeStruct((n, d), F32),
        scratch_shapes=[pltpu.VMEM((tm, d), F32)],
        compiler_params=_cparams(("parallel", "arbitrary")),
        name="ffn_dense",
    )(x1, h2, wg, wu, wd)


def _router_kernel(h_ref, w_ref, c_ref):
    logits = jnp.dot(h_ref[...], w_ref[...], precision=HIGHEST, preferred_element_type=F32)
    n_e = logits.shape[1]
    eid = lax.broadcasted_iota(jnp.int32, logits.shape, 1)
    v1 = jnp.max(logits, axis=1, keepdims=True)
    i1 = jnp.min(jnp.where(logits == v1, eid, n_e), axis=1, keepdims=True)
    rest = jnp.where(eid == i1, -jnp.inf, logits)
    v2 = jnp.max(rest, axis=1, keepdims=True)
    i2 = jnp.min(jnp.where(rest == v2, eid, n_e), axis=1, keepdims=True)
    e2 = jnp.exp(v2 - v1)
    w1 = 1.0 / (1.0 + e2)
    w2 = e2 / (1.0 + e2)
    c_ref[...] = jnp.where(eid == i1, w1, 0.0) + jnp.where(eid == i2, w2, 0.0)


def router(h2, w_router, *, tm):
    n, d = h2.shape
    n_e = w_router.shape[1]
    return pl.pallas_call(
        _router_kernel,
        grid=(n // tm,),
        in_specs=[pl.BlockSpec((tm, d), lambda i: (i, 0)),
                  pl.BlockSpec((d, n_e), lambda i: (0, 0))],
        out_specs=pl.BlockSpec((tm, n_e), lambda i: (i, 0)),
        out_shape=jax.ShapeDtypeStruct((n, n_e), F32),
        compiler_params=_cparams(("parallel",)),
        name="router",
    )(h2, w_router)


def _moe_kernel(x1_ref, h_ref, c_ref, wg_ref, wu_ref, wd_ref, o_ref, acc_ref):
    e = pl.program_id(1)
    f = pl.program_id(2)

    @pl.when((e == 0) & (f == 0))
    def _():
        acc_ref[...] = x1_ref[...]

    n_e = c_ref.shape[1]
    eid = lax.broadcasted_iota(jnp.int32, c_ref.shape, 1)
    ce = jnp.sum(jnp.where(eid == e, c_ref[...], 0.0), axis=1, keepdims=True)
    h = h_ref[...].astype(BF16)
    a = _silu(_dot(h, wg_ref[0])) * _dot(h, wu_ref[0])
    acc_ref[...] += ce * _dot(a.astype(BF16), wd_ref[0])

    @pl.when((e == pl.num_programs(1) - 1) & (f == pl.num_programs(2) - 1))
    def _():
        o_ref[...] = acc_ref[...]


def moe_dense(x1, h2, comb, wg, wu, wd, *, tm):
    n, d = x1.shape
    n_e, _, d_ff = wg.shape
    tf = _ff_tile(d_ff)
    row = lambda i, e, f: (i, 0)
    return pl.pallas_call(
        _moe_kernel,
        grid=(n // tm, n_e, d_ff // tf),
        in_specs=[pl.BlockSpec((tm, d), row), pl.BlockSpec((tm, d), row),
                  pl.BlockSpec((tm, n_e), row),
                  pl.BlockSpec((1, d, tf), lambda i, e, f: (e, 0, f)),
                  pl.BlockSpec((1, d, tf), lambda i, e, f: (e, 0, f)),
                  pl.BlockSpec((1, tf, d), lambda i, e, f: (e, f, 0))],
        out_specs=pl.BlockSpec((tm, d), row),
        out_shape=jax.ShapeDtypeStruct((n, d), F32),
        scratch_shapes=[pltpu.VMEM((tm, d), F32)],
        compiler_params=_cparams(("parallel", "arbitrary", "arbitrary")),
        name="moe_dense",
    )(x1, h2, comb, wg, wu, wd)


def _rope_tables(pos):
    half = HEAD_DIM // 2
    inv = ROPE_BASE ** (-jnp.arange(half, dtype=F32) / half)
    ang = pos.astype(F32)[:, None] * inv[None, :]
    cos, sin = jnp.cos(ang), jnp.sin(ang)
    reps = LANES // HEAD_DIM
    cos_t = jnp.tile(jnp.concatenate([cos, cos], axis=1), (1, reps))
    sin_t = jnp.tile(jnp.concatenate([-sin, sin], axis=1), (1, reps))
    return cos_t, sin_t


def _token_tile(n):
    for tm in (512, 256, 128, 64, 32, 16, 8):
        if n % tm == 0:
            return tm
    return n


def kernel(x_prompt, x_sample, cache_k, cache_v, state_ret, page_table, g_mix, w_in, ret_gn,
           q_norm_g, k_norm_g, w_out, g_ffn, w_ffn_gate, w_ffn_up, w_ffn_down, w_router,
           w_exp_gate, w_exp_up, w_exp_down):
    b, t, d = x_prompt.shape
    db, ds, _ = x_sample.shape
    depth = w_in.shape[0]
    n_phys = cache_k.shape[1]
    n_p, n_s = b * t, db * ds
    tm_p, tm_s = _token_tile(n_p), _token_tile(n_s)
    past_len = page_table.shape[1] * PAGE_SIZE

    cos_p, sin_p = _rope_tables(jnp.arange(t, dtype=jnp.int32))
    cos_s, sin_s = _rope_tables(jnp.broadcast_to(
        past_len + jnp.arange(ds, dtype=jnp.int32)[None, :], (db, ds)).reshape(n_s))
    head_mean = (jnp.kron(jnp.eye(H_MOBA, dtype=F32), jnp.ones((HEAD_DIM, HEAD_DIM), F32))
                 / HEAD_DIM).astype(BF16)
    cache_k3 = cache_k.reshape(depth * n_phys, PAGE_SIZE, W_MOBA)
    cache_v3 = cache_v.reshape(depth * n_phys, PAGE_SIZE, W_MOBA)

    xp = x_prompt.reshape(n_p, d)
    xs = x_sample.reshape(n_s, d)
    sp_l, kp_l, vp_l, ss_l, ks_l, vs_l = [], [], [], [], [], []
    for l in range(depth):
        w_in_b = w_in[l].astype(BF16)
        w_out_b = w_out[l].astype(BF16)
        g1 = g_mix[l].reshape(1, d)
        g2 = g_ffn[l].reshape(1, d)
        qg = jnp.tile(q_norm_g[l], H_MOBA).reshape(1, W_MOBA)
        kg = jnp.tile(k_norm_g[l], H_MOBA).reshape(1, W_MOBA)
        gn = ret_gn[l].reshape(1, W_RET)

        (qr, kr, vr, gr, qm, km, vm, kmb, vmb, kmean) = proj_in(
            xp, g1, w_in_b, cos_p, sin_p, qg, kg, head_mean, tm=tm_p, with_kmean=True)
        seq = lambda a: a.reshape(b, t, a.shape[-1])
        o_r, s_fin = retention_prompt(seq(qr), seq(kr), seq(vr), seq(gr), gn)
        o_m = moba_prompt(seq(qm), seq(kmb), seq(vmb), kmean.reshape(b, t // MOBA_BLOCK, W_MOBA))
        x1p, h2p = out_proj(o_r.reshape(n_p, W_RET), o_m.reshape(n_p, W_MOBA), xp, w_out_b, g2,
                            tm=tm_p)
        sp_l.append(s_fin)
        kp_l.append(km.reshape(b, t, H_MOBA, HEAD_DIM))
        vp_l.append(vm.reshape(b, t, H_MOBA, HEAD_DIM))

        (qr, kr, vr, gr, qm, km, vm, _, _) = proj_in(
            xs, g1, w_in_b, cos_s, sin_s, qg, kg, head_mean, tm=tm_s, with_kmean=False)
        f32 = lambda a: a.astype(F32)
        o_r, s_new = retention_sample(f32(qr), f32(kr), f32(vr), f32(gr), gn, state_ret[l])
        o_m = moba_sample(qm, km, vm, cache_k3, cache_v3, page_table, l * n_phys)
        x1s, h2s = out_proj(o_r.astype(BF16), o_m.astype(BF16), xs, w_out_b, g2, tm=tm_s)
        ss_l.append(s_new)
        ks_l.append(km.reshape(db, ds, H_MOBA, HEAD_DIM))
        vs_l.append(vm.reshape(db, ds, H_MOBA, HEAD_DIM))

        i = l // 2
        if l % 2 == 0:
            wg, wu, wd = (w_ffn_gate[i].astype(BF16), w_ffn_up[i].astype(BF16),
                          w_ffn_down[i].astype(BF16))
            xp = ffn_dense(x1p, h2p, wg, wu, wd, tm=tm_p)
            xs = ffn_dense(x1s, h2s, wg, wu, wd, tm=tm_s)
        else:
            wg, wu, wd = (w_exp_gate[i].astype(BF16), w_exp_up[i].astype(BF16),
                          w_exp_down[i].astype(BF16))
            xp = moe_dense(x1p, h2p, router(h2p, w_router[i], tm=tm_p), wg, wu, wd, tm=tm_p)
            xs = moe_dense(x1s, h2s, router(h2s, w_router[i], tm=tm_s), wg, wu, wd, tm=tm_s)

    return (xp.reshape(b, t, d), xs.reshape(db, ds, d),
            jnp.stack(sp_l), jnp.stack(kp_l), jnp.stack(vp_l),
            jnp.stack(ss_l), jnp.stack(ks_l), jnp.stack(vs_l))
```

```python
import functools
import math

import jax
import jax.numpy as jnp
from jax import lax
from jax.experimental import pallas as pl
from jax.experimental.pallas import tpu as pltpu

F32 = jnp.float32
BF16 = jnp.bfloat16

HEAD_DIM = 64
H_RET = 8
H_MOBA = 8
W_RET = H_RET * HEAD_DIM
W_MOBA = H_MOBA * HEAD_DIM
PAGE_SIZE = 128
ROPE_BASE = 10000.0
MOBA_BLOCK = 256
MOBA_TOPK = 3
TOP_K = 2
EPS = 1e-6
RET_CHUNK = 256
MOBA_KEY_CHUNK = 128
MOBA_UNROLL = 4
LANES = 128
NEG = -1e30
VMEM_LIMIT = 56 * 1024 * 1024

HIGHEST = lax.Precision.HIGHEST


def _cparams(sem):
    return pltpu.CompilerParams(dimension_semantics=sem, vmem_limit_bytes=VMEM_LIMIT)


def _silu(x):
    return x / (1.0 + jnp.exp(-x))


def _dot(a, b):
    return jnp.dot(a, b, preferred_element_type=F32)


def _dot_nt(a, b, precision=None):
    return lax.dot_general(a, b, (((1,), (1,)), ((), ())), precision=precision,
                           preferred_element_type=F32)


def _dot_tn(a, b):
    return lax.dot_general(a, b, (((0,), (0,)), ((), ())), preferred_element_type=F32)


def _split3(x):
    hi = x.astype(BF16)
    r = x - hi.astype(F32)
    mid = r.astype(BF16)
    lo = (r - mid.astype(F32)).astype(BF16)
    return hi, mid, lo


def _proj_in_kernel(x_ref, g_ref, w_ref, cos_ref, sin_ref, qg_ref, kg_ref, hm_ref,
                    qr_ref, kr_ref, vr_ref, gr_ref, qm_ref, km_ref, vm_ref,
                    kmb_ref, vmb_ref, *rest, with_kmean):
    x = x_ref[...]
    tm = x.shape[0]
    ms = jnp.mean(x * x, axis=-1, keepdims=True)
    h = (x * lax.rsqrt(ms + EPS) * g_ref[...]).astype(BF16)

    def proj(c0, width):
        return _dot(h, w_ref[:, c0:c0 + width])

    cos = cos_ref[...]
    sin = sin_ref[...]
    lane = lax.broadcasted_iota(jnp.int32, (tm, LANES), 1)
    first_half = (lane % HEAD_DIM) < (HEAD_DIM // 2)

    def rope(z):
        outs = []
        for c in range(z.shape[1] // LANES):
            zc = z[:, c * LANES:(c + 1) * LANES]
            rot = jnp.where(first_half,
                            pltpu.roll(zc, LANES - HEAD_DIM // 2, 1),
                            pltpu.roll(zc, HEAD_DIM // 2, 1))
            outs.append(zc * cos + rot * sin)
        return jnp.concatenate(outs, axis=1)

    hm = hm_ref[...]

    def head_norm(z, g):
        sq = z * z
        hi = sq.astype(BF16)
        lo = (sq - hi.astype(F32)).astype(BF16)
        msq = _dot(hi, hm) + _dot(lo, hm)
        return z * lax.rsqrt(msq + EPS) * g

    qr_ref[...] = rope(proj(0, W_RET)).astype(BF16)
    kr_ref[...] = (rope(proj(W_RET, W_RET)) * (HEAD_DIM ** -0.5)).astype(BF16)
    vr_ref[...] = proj(2 * W_RET, W_RET).astype(BF16)
    gr_ref[...] = proj(3 * W_RET, W_RET).astype(BF16)
    base = 4 * W_RET
    qm_ref[...] = head_norm(proj(base, W_MOBA), qg_ref[...])
    km = head_norm(proj(base + W_MOBA, W_MOBA), kg_ref[...])
    km_ref[...] = km
    kmb_ref[...] = km.astype(BF16)
    vm = proj(base + 2 * W_MOBA, W_MOBA)
    vm_ref[...] = vm
    vmb_ref[...] = vm.astype(BF16)
    if with_kmean:
        kmean_ref = rest[0]
        for r in range(tm // MOBA_BLOCK):
            kmean_ref[0, r:r + 1, :] = jnp.mean(
                km[r * MOBA_BLOCK:(r + 1) * MOBA_BLOCK], axis=0, keepdims=True)


def proj_in(x, g, w_bf, cos, sin, qg, kg, hm, *, tm, with_kmean):
    n, d = x.shape
    n_pos_tiles = cos.shape[0] // tm
    row = lambda i: (i, 0)
    fixed = lambda i: (0, 0)
    wide = lambda dt: jax.ShapeDtypeStruct((n, W_RET), dt)
    out_shape = [wide(BF16), wide(BF16), wide(BF16), wide(BF16),
                 wide(F32), wide(F32), wide(F32), wide(BF16), wide(BF16)]
    out_specs = [pl.BlockSpec((tm, W_RET), row)] * 9
    if with_kmean:
        nb = tm // MOBA_BLOCK
        out_shape.append(jax.ShapeDtypeStruct((n // tm, nb, W_MOBA), F32))
        out_specs.append(pl.BlockSpec((1, nb, W_MOBA), lambda i: (i, 0, 0)))
    return pl.pallas_call(
        functools.partial(_proj_in_kernel, with_kmean=with_kmean),
        grid=(n // tm,),
        in_specs=[
            pl.BlockSpec((tm, d), row),
            pl.BlockSpec((1, d), fixed),
            pl.BlockSpec(w_bf.shape, fixed),
            pl.BlockSpec((tm, LANES), lambda i: (i % n_pos_tiles, 0)),
            pl.BlockSpec((tm, LANES), lambda i: (i % n_pos_tiles, 0)),
            pl.BlockSpec((1, W_MOBA), fixed),
            pl.BlockSpec((1, W_MOBA), fixed),
            pl.BlockSpec((W_MOBA, W_MOBA), fixed),
        ],
        out_specs=out_specs,
        out_shape=out_shape,
        compiler_params=_cparams(("parallel",)),
        name="proj_in",
    )(x, g, w_bf, cos, sin, qg, kg, hm)


def _ret_prompt_kernel(q_ref, k_ref, v_ref, g_ref, gn_ref, dm_ref, cross_ref, kdec_ref,
                       gc_ref, o_ref, sfin_ref, s_scr):
    c = pl.program_id(1)

    @pl.when(c == 0)
    def _():
        s_scr[...] = jnp.zeros_like(s_scr)

    outs = []
    for h in range(H_RET):
        sl = slice(h * HEAD_DIM, (h + 1) * HEAD_DIM)
        q = q_ref[0, :, sl]
        k = k_ref[0, :, sl]
        v = v_ref[0, :, sl]
        s0 = s_scr[h]
        scores = _dot_nt(q, k) * dm_ref[h]
        o = _dot(scores.astype(BF16), v) + _dot(q, s0.astype(BF16)) * cross_ref[h]
        kd = (k.astype(F32) * kdec_ref[h]).astype(BF16)
        s_scr[h] = gc_ref[h] * s0 + _dot_tn(kd, v)
        mu = jnp.mean(o, axis=-1, keepdims=True)
        oc = o - mu
        var = jnp.mean(oc * oc, axis=-1, keepdims=True)
        on = oc * lax.rsqrt(var + EPS) * gn_ref[:, sl]
        outs.append(on * _silu(g_ref[0, :, sl].astype(F32)))
    o_ref[0] = jnp.concatenate(outs, axis=1).astype(BF16)

    @pl.when(c == pl.num_programs(1) - 1)
    def _():
        sfin_ref[0] = s_scr[...]


def _ret_log_decay():
    return jnp.log1p(-jnp.exp2(-5.0 - jnp.arange(H_RET, dtype=F32)))


def retention_prompt(qr, kr, vr, gr, gn):
    b, t, w = qr.shape
    c = min(RET_CHUNK, t)
    ld = _ret_log_decay()
    i = jnp.arange(c, dtype=F32)
    diff = i[:, None] - i[None, :]
    dmask = jnp.where(diff >= 0, jnp.exp(ld[:, None, None] * jnp.maximum(diff, 0.0)), 0.0)
    bc = lambda a: jnp.broadcast_to(a[:, :, None], (H_RET, c, HEAD_DIM))
    cross = bc(jnp.exp(ld[:, None] * (i + 1.0)[None, :]))
    kdec = bc(jnp.exp(ld[:, None] * (c - 1.0 - i)[None, :]))
    gc = jnp.broadcast_to(jnp.exp(ld * c)[:, None, None], (H_RET, HEAD_DIM, HEAD_DIM))
    tile = pl.BlockSpec((1, c, w), lambda bi, ci: (bi, ci, 0))
    const3 = lambda shape: pl.BlockSpec(shape, lambda bi, ci: (0, 0, 0))
    return pl.pallas_call(
        _ret_prompt_kernel,
        grid=(b, t // c),
        in_specs=[tile, tile, tile, tile,
                  pl.BlockSpec((1, w), lambda bi, ci: (0, 0)),
                  const3((H_RET, c, c)), const3((H_RET, c, HEAD_DIM)),
                  const3((H_RET, c, HEAD_DIM)), const3((H_RET, HEAD_DIM, HEAD_DIM))],
        out_specs=[tile,
                   pl.BlockSpec((1, H_RET, HEAD_DIM, HEAD_DIM), lambda bi, ci: (bi, 0, 0, 0))],
        out_shape=[jax.ShapeDtypeStruct((b, t, w), BF16),
                   jax.ShapeDtypeStruct((b, H_RET, HEAD_DIM, HEAD_DIM), F32)],
        scratch_shapes=[pltpu.VMEM((H_RET, HEAD_DIM, HEAD_DIM), F32)],
        compiler_params=_cparams(("parallel", "arbitrary")),
        name="retention_prompt",
    )(qr, kr, vr, gr, gn, dmask, cross, kdec, gc)


def _moba_prompt_kernel(q_ref, k_ref, v_ref, km_ref, o_ref, vt_scr, sel_scr, m_scr, l_scr,
                        acc_scr):
    qi = pl.program_id(2)
    nb = km_ref.shape[1]
    bs = MOBA_BLOCK
    kc = MOBA_KEY_CHUNK
    n_chunks = bs // kc
    n_heads = LANES // HEAD_DIM

    @pl.when(qi == 0)
    def _():
        def transpose_block(j, c):
            start = pl.multiple_of(j * bs, bs)
            vt = v_ref[0, pl.ds(start, bs), :].T
            for hh in range(n_heads):
                vt_scr[j, hh] = vt[hh * HEAD_DIM:(hh + 1) * HEAD_DIM].astype(BF16)
            return c
        lax.fori_loop(0, nb, transpose_block, 0)

    qt = (q_ref[0] * (HEAD_DIM ** -0.5 * math.log2(math.e))).T
    drow = lax.broadcasted_iota(jnp.int32, qt.shape, 0)
    blk = lax.broadcasted_iota(jnp.int32, (nb, bs), 0)
    kmean = km_ref[0]
    qts = []
    for hh in range(n_heads):
        qh = jnp.where(drow // HEAD_DIM == hh, qt, 0.0)
        qts.append(qh.astype(BF16))
        gate = jnp.dot(kmean, qh, precision=HIGHEST, preferred_element_type=F32)
        g = jnp.where(blk < qi, gate, -jnp.inf)
        sel = jnp.zeros((nb, bs), F32)
        for _ in range(min(MOBA_TOPK, nb)):
            mx = jnp.max(g, axis=0, keepdims=True)
            idx = jnp.min(jnp.where(g == mx, blk, nb), axis=0, keepdims=True)
            pick = blk == idx
            sel = jnp.where(pick, 1.0, sel)
            g = jnp.where(pick, -jnp.inf, g)
        sel_scr[hh] = jnp.where(blk < qi, sel, 0.0)

    m_scr[...] = jnp.full(m_scr.shape, NEG, F32)
    l_scr[...] = jnp.zeros(l_scr.shape, F32)
    acc_scr[...] = jnp.zeros(acc_scr.shape, F32)

    def merge(hh, stats, sel_row=None):
        m = m_scr[hh]
        mb, lb, pv = stats
        m_new = jnp.maximum(m, mb)
        wb = jnp.exp2(mb - m_new)
        if sel_row is not None:
            m_new = jnp.where(sel_row, m_new, m)
            wb = jnp.where(sel_row, wb, 0.0)
        wa = jnp.exp2(m - m_new)
        m_scr[hh] = m_new
        l_scr[hh] = wa * l_scr[hh] + wb * lb
        acc_scr[hh] = wa * acc_scr[hh] + wb * pv

    krow = lax.broadcasted_iota(jnp.int32, (kc, bs), 0)
    qcol = lax.broadcasted_iota(jnp.int32, (kc, bs), 1)

    def blocks(j0, n_blocks, own):
        units = [(u, c, hh) for u in range(n_blocks) for c in range(n_chunks)
                 for hh in range(n_heads)]
        scores = []
        for u, c, hh in units:
            start = pl.multiple_of((j0 + u) * bs + c * kc, kc)
            s = _dot(k_ref[0, pl.ds(start, kc), :], qts[hh])
            if own:
                s = jnp.where(krow + c * kc <= qcol, s, NEG)
            scores.append(s)
        parts = []
        for s in scores:
            mb = jnp.max(s, axis=0, keepdims=True)
            p = jnp.exp2(s - mb)
            parts.append((mb, jnp.sum(p, axis=0, keepdims=True), p.astype(BF16)))
        pvs = [_dot(vt_scr[j0 + u, hh, :, c * kc:(c + 1) * kc], parts[i][2])
               for i, (u, c, hh) in enumerate(units)]
        for i, (u, c, hh) in enumerate(units):
            sel_row = None if own else sel_scr[hh, pl.ds(j0 + u, 1), :] > 0.0
            merge(hh, (parts[i][0], parts[i][1], pvs[i]), sel_row)

    def group_body(g, c_):
        blocks(g * MOBA_UNROLL, MOBA_UNROLL, False)
        return c_

    def single_body(j, c_):
        blocks(j, 1, False)
        return c_

    n_groups = qi // MOBA_UNROLL
    lax.fori_loop(0, n_groups, group_body, 0)
    lax.fori_loop(n_groups * MOBA_UNROLL, qi, single_body, 0)
    blocks(qi, 1, True)
    outs = [acc_scr[hh] / l_scr[hh] for hh in range(n_heads)]
    o_ref[0] = jnp.concatenate(outs, axis=0).T.astype(BF16)


def moba_prompt(qm, kmb, vmb, kmean):
    b, t, w = qm.shape
    nb = t // MOBA_BLOCK
    n_heads = LANES // HEAD_DIM
    return pl.pallas_call(
        _moba_prompt_kernel,
        grid=(b, w // LANES, nb),
        in_specs=[
            pl.BlockSpec((1, MOBA_BLOCK, LANES), lambda bi, hp, qi: (bi, qi, hp)),
            pl.BlockSpec((1, t, LANES), lambda bi, hp, qi: (bi, 0, hp)),
            pl.BlockSpec((1, t, LANES), lambda bi, hp, qi: (bi, 0, hp)),
            pl.BlockSpec((1, nb, LANES), lambda bi, hp, qi: (bi, 0, hp)),
        ],
        out_specs=pl.BlockSpec((1, MOBA_BLOCK, LANES), lambda bi, hp, qi: (bi, qi, hp)),
        out_shape=jax.ShapeDtypeStruct((b, t, w), BF16),
        scratch_shapes=[pltpu.VMEM((nb, n_heads, HEAD_DIM, MOBA_BLOCK), BF16),
                        pltpu.VMEM((n_heads, nb, MOBA_BLOCK), F32),
                        pltpu.VMEM((n_heads, 1, MOBA_BLOCK), F32),
                        pltpu.VMEM((n_heads, 1, MOBA_BLOCK), F32),
                        pltpu.VMEM((n_heads, HEAD_DIM, MOBA_BLOCK), F32)],
        compiler_params=_cparams(("parallel", "parallel", "arbitrary")),
        name="moba_prompt",
    )(qm, kmb, vmb, kmean)


def _ret_sample_kernel(q_ref, k_ref, v_ref, g_ref, gam_ref, gn_ref, s0_ref, e1_ref, e2_ref,
                       o_ref, s_ref):
    e1 = e1_ref[...]
    e2 = e2_ref[...]

    def expand(x, e):
        hi, mid, lo = _split3(x)
        return _dot(hi, e) + _dot(mid, e) + _dot(lo, e)

    kexp = expand(k_ref[...], e1)
    vexp = expand(v_ref[...], e2)
    qexp = expand(q_ref[...], e1)
    s_new = gam_ref[:, :1] * s0_ref[...] + kexp * vexp
    s_ref[...] = s_new
    hi, mid, lo = _split3(qexp * s_new)
    o = _dot_nt(hi, e2) + _dot_nt(mid, e2) + _dot_nt(lo, e2)
    mu = jnp.mean(o, axis=-1, keepdims=True)
    oc = o - mu
    var = jnp.mean(oc * oc, axis=-1, keepdims=True)
    o_ref[...] = oc * lax.rsqrt(var + EPS) * gn_ref[...] * _silu(g_ref[...])


def retention_sample(q, k, v, gr, gn, s0):
    n = q.shape[0]
    r = n * H_RET
    dd = HEAD_DIM * HEAD_DIM
    rows = lambda a: a.reshape(r, HEAD_DIM)
    gam = jnp.tile(jnp.broadcast_to(jnp.exp(_ret_log_decay())[:, None], (H_RET, HEAD_DIM)), (n, 1))
    gnr = jnp.tile(gn.reshape(H_RET, HEAD_DIM), (n, 1))
    lane = jnp.arange(dd)
    e1 = (lane[None, :] // HEAD_DIM == jnp.arange(HEAD_DIM)[:, None]).astype(BF16)
    e2 = (lane[None, :] % HEAD_DIM == jnp.arange(HEAD_DIM)[:, None]).astype(BF16)
    tr = min(128, r)
    small = pl.BlockSpec((tr, HEAD_DIM), lambda i: (i, 0))
    big = pl.BlockSpec((tr, dd), lambda i: (i, 0))
    emat = pl.BlockSpec((HEAD_DIM, dd), lambda i: (0, 0))
    o, s_new = pl.pallas_call(
        _ret_sample_kernel,
        grid=(r // tr,),
        in_specs=[small, small, small, small, small, small, big, emat, emat],
        out_specs=[small, big],
        out_shape=[jax.ShapeDtypeStruct((r, HEAD_DIM), F32),
                   jax.ShapeDtypeStruct((r, dd), F32)],
        compiler_params=_cparams(("parallel",)),
        name="retention_sample",
    )(rows(q), rows(k), rows(v), rows(gr), gam, gnr, s0.reshape(r, dd), e1, e2)
    return o.reshape(n, W_RET), s_new.reshape(n, H_RET, HEAD_DIM, HEAD_DIM)


def _moba_sample_kernel(pt_ref, q_ref, kn_ref, vn_ref, *refs, n_pages):
    del pt_ref
    k_refs = refs[:n_pages]
    v_refs = refs[n_pages:2 * n_pages]
    o_ref = refs[2 * n_pages]
    scale = HEAD_DIM ** -0.5
    ppb = MOBA_BLOCK // PAGE_SIZE
    nbp = n_pages // ppb
    head = lax.broadcasted_iota(jnp.int32, (H_MOBA, W_MOBA), 0)
    lane = lax.broadcasted_iota(jnp.int32, (H_MOBA, W_MOBA), 1)
    own_head = (lane // HEAD_DIM) == head
    qrows = jnp.where(own_head, q_ref[0], 0.0)
    s_pages = [_dot_nt(qrows, k_refs[p][0], precision=HIGHEST) for p in range(n_pages)]
    gates = []
    for j in range(nbp):
        tot = s_pages[j * ppb]
        for p in range(1, ppb):
            tot = tot + s_pages[j * ppb + p]
        gates.append(jnp.sum(tot, axis=1, keepdims=True) * (1.0 / MOBA_BLOCK))
    k_sel = min(MOBA_TOPK, nbp + 1)
    sels = []
    for j in range(nbp):
        rank = jnp.zeros((H_MOBA, 1), F32)
        for i in range(nbp):
            if i == j:
                continue
            ahead = (gates[i] > gates[j]) | ((gates[i] == gates[j]) & (i < j))
            rank = rank + jnp.where(ahead, 1.0, 0.0)
        sels.append(rank < k_sel)
    s_own = jnp.sum(qrows * kn_ref[0], axis=1, keepdims=True) * scale
    m = s_own
    s_msk = []
    for p in range(n_pages):
        sp = jnp.where(sels[p // ppb], s_pages[p] * scale, NEG)
        s_msk.append(sp)
        m = jnp.maximum(m, jnp.max(sp, axis=1, keepdims=True))
    p_own = jnp.exp(s_own - m)
    l = p_own
    acc = p_own * jnp.broadcast_to(vn_ref[0], (H_MOBA, W_MOBA))
    for p in range(n_pages):
        pp = jnp.exp(s_msk[p] - m)
        l = l + jnp.sum(pp, axis=1, keepdims=True)
        acc = acc + jnp.dot(pp, v_refs[p][0], precision=HIGHEST, preferred_element_type=F32)
    o8 = acc / l
    o_ref[0] = jnp.sum(jnp.where(own_head, o8, 0.0), axis=0, keepdims=True)


def moba_sample(q, k_new, v_new, cache_k3, cache_v3, page_table, page_base):
    n, w = q.shape
    n_pages = page_table.shape[1]
    three = lambda a: a.reshape(n, 1, w)
    vec = pl.BlockSpec((1, 1, w), lambda s, pt: (s, 0, 0))

    def page_spec(p):
        return pl.BlockSpec((1, PAGE_SIZE, w), lambda s, pt: (page_base + pt[s, p], 0, 0))

    pages = [page_spec(p) for p in range(n_pages)]
    out = pl.pallas_call(
        functools.partial(_moba_sample_kernel, n_pages=n_pages),
        grid_spec=pltpu.PrefetchScalarGridSpec(
            num_scalar_prefetch=1,
            grid=(n,),
            in_specs=[vec, vec, vec] + pages + pages,
            out_specs=vec,
        ),
        out_shape=jax.ShapeDtypeStruct((n, 1, w), F32),
        compiler_params=_cparams(("arbitrary",)),
        name="moba_sample",
    )(page_table, three(q), three(k_new), three(v_new),
      *([cache_k3] * n_pages), *([cache_v3] * n_pages))
    return out.reshape(n, w)


def _out_proj_kernel(or_ref, om_ref, x_ref, w_ref, g_ref, x1_ref, h2_ref):
    x1 = (x_ref[...] + _dot(or_ref[...], w_ref[:W_RET, :])
          + _dot(om_ref[...], w_ref[W_RET:, :]))
    x1_ref[...] = x1
    ms = jnp.mean(x1 * x1, axis=-1, keepdims=True)
    h2_ref[...] = x1 * lax.rsqrt(ms + EPS) * g_ref[...]


def out_proj(o_r, o_m, x, w_bf, g, *, tm):
    n, d = x.shape
    row = lambda i: (i, 0)
    fixed = lambda i: (0, 0)
    return pl.pallas_call(
        _out_proj_kernel,
        grid=(n // tm,),
        in_specs=[pl.BlockSpec((tm, W_RET), row), pl.BlockSpec((tm, W_MOBA), row),
                  pl.BlockSpec((tm, d), row), pl.BlockSpec(w_bf.shape, fixed),
                  pl.BlockSpec((1, d), fixed)],
        out_specs=[pl.BlockSpec((tm, d), row), pl.BlockSpec((tm, d), row)],
        out_shape=[jax.ShapeDtypeStruct((n, d), F32), jax.ShapeDtypeStruct((n, d), F32)],
        compiler_params=_cparams(("parallel",)),
        name="out_proj",
    )(o_r, o_m, x, w_bf, g)


def _ffn_kernel(x1_ref, h_ref, wg_ref, wu_ref, wd_ref, o_ref, acc_ref):
    f = pl.program_id(1)

    @pl.when(f == 0)
    def _():
        acc_ref[...] = x1_ref[...]

    h = h_ref[...].astype(BF16)
    a = _silu(_dot(h, wg_ref[...])) * _dot(h, wu_ref[...])
    acc_ref[...] += _dot(a.astype(BF16), wd_ref[...])

    @pl.when(f == pl.num_programs(1) - 1)
    def _():
        o_ref[...] = acc_ref[...]


def _ff_tile(d_ff):
    for tf in (512, 256, 128):
        if d_ff % tf == 0:
            return tf
    return d_ff


def ffn_dense(x1, h2, wg, wu, wd, *, tm):
    n, d = x1.shape
    d_ff = wg.shape[1]
    tf = _ff_tile(d_ff)
    row = lambda i, f: (i, 0)
    return pl.pallas_call(
        _ffn_kernel,
        grid=(n // tm, d_ff // tf),
        in_specs=[pl.BlockSpec((tm, d), row), pl.BlockSpec((tm, d), row),
                  pl.BlockSpec((d, tf), lambda i, f: (0, f)),
                  pl.BlockSpec((d, tf), lambda i, f: (0, f)),
                  pl.BlockSpec((tf, d), lambda i, f: (f, 0))],
        out_specs=pl.BlockSpec((tm, d), row),
        out_shape=jax.ShapeDtypeStruct((n, d), F32),
        scratch_shapes=[pltpu.VMEM((tm, d), F32)],
        compiler_params=_cparams(("parallel", "arbitrary")),
        name="ffn_dense",
    )(x1, h2, wg, wu, wd)


def _router_kernel(h_ref, w_ref, c_ref):
    logits = jnp.dot(h_ref[...], w_ref[...], precision=HIGHEST, preferred_element_type=F32)
    n_e = logits.shape[1]
    eid = lax.broadcasted_iota(jnp.int32, logits.shape, 1)
    v1 = jnp.max(logits, axis=1, keepdims=True)
    i1 = jnp.min(jnp.where(logits == v1, eid, n_e), axis=1, keepdims=True)
    rest = jnp.where(eid == i1, -jnp.inf, logits)
    v2 = jnp.max(rest, axis=1, keepdims=True)
    i2 = jnp.min(jnp.where(rest == v2, eid, n_e), axis=1, keepdims=True)
    e2 = jnp.exp(v2 - v1)
    w1 = 1.0 / (1.0 + e2)
    w2 = e2 / (1.0 + e2)
    c_ref[...] = jnp.where(eid == i1, w1, 0.0) + jnp.where(eid == i2, w2, 0.0)


def router(h2, w_router, *, tm):
    n, d = h2.shape
    n_e = w_router.shape[1]
    return pl.pallas_call(
        _router_kernel,
        grid=(n // tm,),
        in_specs=[pl.BlockSpec((tm, d), lambda i: (i, 0)),
                  pl.BlockSpec((d, n_e), lambda i: (0, 0))],
        out_specs=pl.BlockSpec((tm, n_e), lambda i: (i, 0)),
        out_shape=jax.ShapeDtypeStruct((n, n_e), F32),
        compiler_params=_cparams(("parallel",)),
        name="router",
    )(h2, w_router)


def _moe_kernel(x1_ref, h_ref, c_ref, wg_ref, wu_ref, wd_ref, o_ref, acc_ref):
    e = pl.program_id(1)
    f = pl.program_id(2)

    @pl.when((e == 0) & (f == 0))
    def _():
        acc_ref[...] = x1_ref[...]

    n_e = c_ref.shape[1]
    eid = lax.broadcasted_iota(jnp.int32, c_ref.shape, 1)
    ce = jnp.sum(jnp.where(eid == e, c_ref[...], 0.0), axis=1, keepdims=True)
    h = h_ref[...].astype(BF16)
    a = _silu(_dot(h, wg_ref[0])) * _dot(h, wu_ref[0])
    acc_ref[...] += ce * _dot(a.astype(BF16), wd_ref[0])

    @pl.when((e == pl.num_programs(1) - 1) & (f == pl.num_programs(2) - 1))
    def _():
        o_ref[...] = acc_ref[...]


def moe_dense(x1, h2, comb, wg, wu, wd, *, tm):
    n, d = x1.shape
    n_e, _, d_ff = wg.shape
    tf = _ff_tile(d_ff)
    row = lambda i, e, f: (i, 0)
    return pl.pallas_call(
        _moe_kernel,
        grid=(n // tm, n_e, d_ff // tf),
        in_specs=[pl.BlockSpec((tm, d), row), pl.BlockSpec((tm, d), row),
                  pl.BlockSpec((tm, n_e), row),
                  pl.BlockSpec((1, d, tf), lambda i, e, f: (e, 0, f)),
                  pl.BlockSpec((1, d, tf), lambda i, e, f: (e, 0, f)),
                  pl.BlockSpec((1, tf, d), lambda i, e, f: (e, f, 0))],
        out_specs=pl.BlockSpec((tm, d), row),
        out_shape=jax.ShapeDtypeStruct((n, d), F32),
        scratch_shapes=[pltpu.VMEM((tm, d), F32)],
        compiler_params=_cparams(("parallel", "arbitrary", "arbitrary")),
        name="moe_dense",
    )(x1, h2, comb, wg, wu, wd)


def _rope_tables(pos):
    half = HEAD_DIM // 2
    inv = ROPE_BASE ** (-jnp.arange(half, dtype=F32) / half)
    ang = pos.astype(F32)[:, None] * inv[None, :]
    cos, sin = jnp.cos(ang), jnp.sin(ang)
    reps = LANES // HEAD_DIM
    cos_t = jnp.tile(jnp.concatenate([cos, cos], axis=1), (1, reps))
    sin_t = jnp.tile(jnp.concatenate([-sin, sin], axis=1), (1, reps))
    return cos_t, sin_t


def _token_tile(n):
    for tm in (512, 256, 128, 64, 32, 16, 8):
        if n % tm == 0:
            return tm
    return n


def kernel(x_prompt, x_sample, cache_k, cache_v, state_ret, page_table, g_mix, w_in, ret_gn,
           q_norm_g, k_norm_g, w_out, g_ffn, w_ffn_gate, w_ffn_up, w_ffn_down, w_router,
           w_exp_gate, w_exp_up, w_exp_down):
    b, t, d = x_prompt.shape
    db, ds, _ = x_sample.shape
    depth = w_in.shape[0]
    n_phys = cache_k.shape[1]
    n_p, n_s = b * t, db * ds
    tm_p, tm_s = _token_tile(n_p), _token_tile(n_s)
    past_len = page_table.shape[1] * PAGE_SIZE

    cos_p, sin_p = _rope_tables(jnp.arange(t, dtype=jnp.int32))
    cos_s, sin_s = _rope_tables(jnp.broadcast_to(
        past_len + jnp.arange(ds, dtype=jnp.int32)[None, :], (db, ds)).reshape(n_s))
    head_mean = (jnp.kron(jnp.eye(H_MOBA, dtype=F32), jnp.ones((HEAD_DIM, HEAD_DIM), F32))
                 / HEAD_DIM).astype(BF16)
    cache_k3 = cache_k.reshape(depth * n_phys, PAGE_SIZE, W_MOBA)
    cache_v3 = cache_v.reshape(depth * n_phys, PAGE_SIZE, W_MOBA)

    xp = x_prompt.reshape(n_p, d)
    xs = x_sample.reshape(n_s, d)
    sp_l, kp_l, vp_l, ss_l, ks_l, vs_l = [], [], [], [], [], []
    for l in range(depth):
        w_in_b = w_in[l].astype(BF16)
        w_out_b = w_out[l].astype(BF16)
        g1 = g_mix[l].reshape(1, d)
        g2 = g_ffn[l].reshape(1, d)
        qg = jnp.tile(q_norm_g[l], H_MOBA).reshape(1, W_MOBA)
        kg = jnp.tile(k_norm_g[l], H_MOBA).reshape(1, W_MOBA)
        gn = ret_gn[l].reshape(1, W_RET)

        (qr, kr, vr, gr, qm, km, vm, kmb, vmb, kmean) = proj_in(
            xp, g1, w_in_b, cos_p, sin_p, qg, kg, head_mean, tm=tm_p, with_kmean=True)
        seq = lambda a: a.reshape(b, t, a.shape[-1])
        o_r, s_fin = retention_prompt(seq(qr), seq(kr), seq(vr), seq(gr), gn)
        o_m = moba_prompt(seq(qm), seq(kmb), seq(vm), kmean.reshape(b, t // MOBA_BLOCK, W_MOBA))
        x1p, h2p = out_proj(o_r.reshape(n_p, W_RET), o_m.reshape(n_p, W_MOBA), xp, w_out_b, g2,
                            tm=tm_p)
        sp_l.append(s_fin)
        kp_l.append(km.reshape(b, t, H_MOBA, HEAD_DIM))
        vp_l.append(vm.reshape(b, t, H_MOBA, HEAD_DIM))

        (qr, kr, vr, gr, qm, km, vm, _, _) = proj_in(
            xs, g1, w_in_b, cos_s, sin_s, qg, kg, head_mean, tm=tm_s, with_kmean=False)
        f32 = lambda a: a.astype(F32)
        o_r, s_new = retention_sample(f32(qr), f32(kr), f32(vr), f32(gr), gn, state_ret[l])
        o_m = moba_sample(qm, km, vm, cache_k3, cache_v3, page_table, l * n_phys)
        x1s, h2s = out_proj(o_r.astype(BF16), o_m.astype(BF16), xs, w_out_b, g2, tm=tm_s)
        ss_l.append(s_new)
        ks_l.append(km.reshape(db, ds, H_MOBA, HEAD_DIM))
        vs_l.append(vm.reshape(db, ds, H_MOBA, HEAD_DIM))

        i = l // 2
        if l % 2 == 0:
            wg, wu, wd = (w_ffn_gate[i].astype(BF16), w_ffn_up[i].astype(BF16),
                          w_ffn_down[i].astype(BF16))
            xp = ffn_dense(x1p, h2p, wg, wu, wd, tm=tm_p)
            xs = ffn_dense(x1s, h2s, wg, wu, wd, tm=tm_s)
        else:
            wg, wu, wd = (w_exp_gate[i].astype(BF16), w_exp_up[i].astype(BF16),
                          w_exp_down[i].astype(BF16))
            xp = moe_dense(x1p, h2p, router(h2p, w_router[i], tm=tm_p), wg, wu, wd, tm=tm_p)
            xs = moe_dense(x1s, h2s, router(h2s, w_router[i], tm=tm_s), wg, wu, wd, tm=tm_s)

    return (xp.reshape(b, t, d), xs.reshape(db, ds, d),
            jnp.stack(sp_l), jnp.stack(kp_l), jnp.stack(vp_l),
            jnp.stack(ss_l), jnp.stack(ks_l), jnp.stack(vs_l))
```

```python
import functools
import math

import jax
import jax.numpy as jnp
from jax import lax
from jax.experimental import pallas as pl
from jax.experimental.pallas import tpu as pltpu

F32 = jnp.float32
BF16 = jnp.bfloat16

HEAD_DIM = 64
H_RET = 8
H_MOBA = 8
W_RET = H_RET * HEAD_DIM
W_MOBA = H_MOBA * HEAD_DIM
PAGE_SIZE = 128
ROPE_BASE = 10000.0
MOBA_BLOCK = 256
MOBA_TOPK = 3
TOP_K = 2
EPS = 1e-6
RET_CHUNK = 256
MOBA_KEY_CHUNK = 128
MOBA_UNROLL = 4
MOE_TILE = 512
MOE_ROW_TILE = 256
LANES = 128
NEG = -1e30
VMEM_LIMIT = 56 * 1024 * 1024

HIGHEST = lax.Precision.HIGHEST


def _cparams(sem):
    return pltpu.CompilerParams(dimension_semantics=sem, vmem_limit_bytes=VMEM_LIMIT)


def _silu(x):
    return x / (1.0 + jnp.exp(-x))


def _dot(a, b):
    return jnp.dot(a, b, preferred_element_type=F32)


def _dot_nt(a, b, precision=None):
    return lax.dot_general(a, b, (((1,), (1,)), ((), ())), precision=precision,
                           preferred_element_type=F32)


def _dot_tn(a, b):
    return lax.dot_general(a, b, (((0,), (0,)), ((), ())), preferred_element_type=F32)


def _split3(x):
    hi = x.astype(BF16)
    r = x - hi.astype(F32)
    mid = r.astype(BF16)
    lo = (r - mid.astype(F32)).astype(BF16)
    return hi, mid, lo


def _proj_in_kernel(x_ref, g_ref, w_ref, cos_ref, sin_ref, qg_ref, kg_ref, hm_ref,
                    qr_ref, kr_ref, vr_ref, gr_ref, qm_ref, km_ref, vm_ref,
                    *prompt_refs):
    x = x_ref[...]
    tm = x.shape[0]
    ms = jnp.mean(x * x, axis=-1, keepdims=True)
    h = (x * lax.rsqrt(ms + EPS) * g_ref[...]).astype(BF16)

    def proj(c0, width):
        return _dot(h, w_ref[:, c0:c0 + width])

    cos = cos_ref[...]
    sin = sin_ref[...]
    lane = lax.broadcasted_iota(jnp.int32, (tm, LANES), 1)
    first_half = (lane % HEAD_DIM) < (HEAD_DIM // 2)

    def rope(z):
        outs = []
        for c in range(z.shape[1] // LANES):
            zc = z[:, c * LANES:(c + 1) * LANES]
            rot = jnp.where(first_half,
                            pltpu.roll(zc, LANES - HEAD_DIM // 2, 1),
                            pltpu.roll(zc, HEAD_DIM // 2, 1))
            outs.append(zc * cos + rot * sin)
        return jnp.concatenate(outs, axis=1)

    hm = hm_ref[...]

    def head_norm(z, g):
        sq = z * z
        hi = sq.astype(BF16)
        lo = (sq - hi.astype(F32)).astype(BF16)
        msq = _dot(hi, hm) + _dot(lo, hm)
        return z * lax.rsqrt(msq + EPS) * g

    qr_ref[...] = rope(proj(0, W_RET)).astype(BF16)
    kr_ref[...] = (rope(proj(W_RET, W_RET)) * (HEAD_DIM ** -0.5)).astype(BF16)
    vr_ref[...] = proj(2 * W_RET, W_RET).astype(BF16)
    gr_ref[...] = proj(3 * W_RET, W_RET).astype(BF16)
    base = 4 * W_RET
    qm_ref[...] = head_norm(proj(base, W_MOBA), qg_ref[...])
    km = head_norm(proj(base + W_MOBA, W_MOBA), kg_ref[...])
    km_ref[...] = km
    vm_ref[...] = proj(base + 2 * W_MOBA, W_MOBA)
    if prompt_refs:
        kmb_ref, kmean_ref = prompt_refs
        kmb_ref[...] = km.astype(BF16)
        for r in range(tm // MOBA_BLOCK):
            kmean_ref[0, r:r + 1, :] = jnp.mean(
                km[r * MOBA_BLOCK:(r + 1) * MOBA_BLOCK], axis=0, keepdims=True)


def proj_in(x, g, w_bf, cos, sin, qg, kg, hm, *, tm, with_kmean):
    n, d = x.shape
    n_pos_tiles = cos.shape[0] // tm
    row = lambda i: (i, 0)
    fixed = lambda i: (0, 0)
    wide = lambda dt: jax.ShapeDtypeStruct((n, W_RET), dt)
    out_shape = [wide(BF16), wide(BF16), wide(BF16), wide(BF16),
                 wide(F32), wide(F32), wide(F32)]
    out_specs = [pl.BlockSpec((tm, W_RET), row)] * 7
    if with_kmean:
        nb = tm // MOBA_BLOCK
        out_shape += [wide(BF16), jax.ShapeDtypeStruct((n // tm, nb, W_MOBA), F32)]
        out_specs += [pl.BlockSpec((tm, W_RET), row),
                      pl.BlockSpec((1, nb, W_MOBA), lambda i: (i, 0, 0))]
    return pl.pallas_call(
        _proj_in_kernel,
        grid=(n // tm,),
        in_specs=[
            pl.BlockSpec((tm, d), row),
            pl.BlockSpec((1, d), fixed),
            pl.BlockSpec(w_bf.shape, fixed),
            pl.BlockSpec((tm, LANES), lambda i: (i % n_pos_tiles, 0)),
            pl.BlockSpec((tm, LANES), lambda i: (i % n_pos_tiles, 0)),
            pl.BlockSpec((1, W_MOBA), fixed),
            pl.BlockSpec((1, W_MOBA), fixed),
            pl.BlockSpec((W_MOBA, W_MOBA), fixed),
        ],
        out_specs=out_specs,
        out_shape=out_shape,
        compiler_params=_cparams(("parallel",)),
        name="proj_in",
    )(x, g, w_bf, cos, sin, qg, kg, hm)


def _ret_prompt_kernel(q_ref, k_ref, v_ref, g_ref, gn_ref, dm_ref, cross_ref, kdec_ref,
                       gc_ref, o_ref, sfin_ref, s_scr):
    c = pl.program_id(1)

    @pl.when(c == 0)
    def _():
        s_scr[...] = jnp.zeros_like(s_scr)

    outs = []
    for h in range(H_RET):
        sl = slice(h * HEAD_DIM, (h + 1) * HEAD_DIM)
        q = q_ref[0, :, sl]
        k = k_ref[0, :, sl]
        v = v_ref[0, :, sl]
        s0 = s_scr[h]
        scores = _dot_nt(q, k) * dm_ref[h]
        o = _dot(scores.astype(BF16), v) + _dot(q, s0.astype(BF16)) * cross_ref[h]
        kd = (k.astype(F32) * kdec_ref[h]).astype(BF16)
        s_scr[h] = gc_ref[h] * s0 + _dot_tn(kd, v)
        mu = jnp.mean(o, axis=-1, keepdims=True)
        oc = o - mu
        var = jnp.mean(oc * oc, axis=-1, keepdims=True)
        on = oc * lax.rsqrt(var + EPS) * gn_ref[:, sl]
        outs.append(on * _silu(g_ref[0, :, sl].astype(F32)))
    o_ref[0] = jnp.concatenate(outs, axis=1).astype(BF16)

    @pl.when(c == pl.num_programs(1) - 1)
    def _():
        sfin_ref[0] = s_scr[...]


def _ret_log_decay():
    return jnp.log1p(-jnp.exp2(-5.0 - jnp.arange(H_RET, dtype=F32)))


def retention_prompt(qr, kr, vr, gr, gn):
    b, t, w = qr.shape
    c = min(RET_CHUNK, t)
    ld = _ret_log_decay()
    i = jnp.arange(c, dtype=F32)
    diff = i[:, None] - i[None, :]
    dmask = jnp.where(diff >= 0, jnp.exp(ld[:, None, None] * jnp.maximum(diff, 0.0)), 0.0)
    bc = lambda a: jnp.broadcast_to(a[:, :, None], (H_RET, c, HEAD_DIM))
    cross = bc(jnp.exp(ld[:, None] * (i + 1.0)[None, :]))
    kdec = bc(jnp.exp(ld[:, None] * (c - 1.0 - i)[None, :]))
    gc = jnp.broadcast_to(jnp.exp(ld * c)[:, None, None], (H_RET, HEAD_DIM, HEAD_DIM))
    tile = pl.BlockSpec((1, c, w), lambda bi, ci: (bi, ci, 0))
    const3 = lambda shape: pl.BlockSpec(shape, lambda bi, ci: (0, 0, 0))
    return pl.pallas_call(
        _ret_prompt_kernel,
        grid=(b, t // c),
        in_specs=[tile, tile, tile, tile,
                  pl.BlockSpec((1, w), lambda bi, ci: (0, 0)),
                  const3((H_RET, c, c)), const3((H_RET, c, HEAD_DIM)),
                  const3((H_RET, c, HEAD_DIM)), const3((H_RET, HEAD_DIM, HEAD_DIM))],
        out_specs=[tile,
                   pl.BlockSpec((1, H_RET, HEAD_DIM, HEAD_DIM), lambda bi, ci: (bi, 0, 0, 0))],
        out_shape=[jax.ShapeDtypeStruct((b, t, w), BF16),
                   jax.ShapeDtypeStruct((b, H_RET, HEAD_DIM, HEAD_DIM), F32)],
        scratch_shapes=[pltpu.VMEM((H_RET, HEAD_DIM, HEAD_DIM), F32)],
        compiler_params=_cparams(("parallel", "arbitrary")),
        name="retention_prompt",
    )(qr, kr, vr, gr, gn, dmask, cross, kdec, gc)


def _moba_prompt_kernel(q_ref, k_ref, v_ref, km_ref, o_ref, vt_scr, sel_scr, m_scr, l_scr,
                        acc_scr):
    qi = pl.program_id(2)
    nb = km_ref.shape[1]
    bs = MOBA_BLOCK
    kc = MOBA_KEY_CHUNK
    n_chunks = bs // kc
    n_heads = LANES // HEAD_DIM

    @pl.when(qi == 0)
    def _():
        def transpose_block(j, c):
            start = pl.multiple_of(j * bs, bs)
            vt = v_ref[0, pl.ds(start, bs), :].T
            for hh in range(n_heads):
                vt_scr[j, hh] = vt[hh * HEAD_DIM:(hh + 1) * HEAD_DIM].astype(BF16)
            return c
        lax.fori_loop(0, nb, transpose_block, 0)

    qt = (q_ref[0] * (HEAD_DIM ** -0.5 * math.log2(math.e))).T
    drow = lax.broadcasted_iota(jnp.int32, qt.shape, 0)
    blk = lax.broadcasted_iota(jnp.int32, (nb, bs), 0)
    kmean = km_ref[0]
    qts = []
    for hh in range(n_heads):
        qh = jnp.where(drow // HEAD_DIM == hh, qt, 0.0)
        qts.append(qh.astype(BF16))
        gate = jnp.dot(kmean, qh, precision=HIGHEST, preferred_element_type=F32)
        g = jnp.where(blk < qi, gate, -jnp.inf)
        sel = jnp.zeros((nb, bs), F32)
        for _ in range(min(MOBA_TOPK, nb)):
            mx = jnp.max(g, axis=0, keepdims=True)
            idx = jnp.min(jnp.where(g == mx, blk, nb), axis=0, keepdims=True)
            pick = blk == idx
            sel = jnp.where(pick, 1.0, sel)
            g = jnp.where(pick, -jnp.inf, g)
        sel_scr[hh] = jnp.where(blk < qi, sel, 0.0)

    m_scr[...] = jnp.full(m_scr.shape, NEG, F32)
    l_scr[...] = jnp.zeros(l_scr.shape, F32)
    acc_scr[...] = jnp.zeros(acc_scr.shape, F32)

    def merge(hh, stats, sel_row=None):
        m = m_scr[hh]
        mb, lb, pv = stats
        m_new = jnp.maximum(m, mb)
        wb = jnp.exp2(mb - m_new)
        if sel_row is not None:
            m_new = jnp.where(sel_row, m_new, m)
            wb = jnp.where(sel_row, wb, 0.0)
        wa = jnp.exp2(m - m_new)
        m_scr[hh] = m_new
        l_scr[hh] = wa * l_scr[hh] + wb * lb
        acc_scr[hh] = wa * acc_scr[hh] + wb * pv

    krow = lax.broadcasted_iota(jnp.int32, (kc, bs), 0)
    qcol = lax.broadcasted_iota(jnp.int32, (kc, bs), 1)

    def blocks(j0, n_blocks, own):
        units = [(u, c, hh) for u in range(n_blocks) for c in range(n_chunks)
                 for hh in range(n_heads)]
        scores = []
        for u, c, hh in units:
            start = pl.multiple_of((j0 + u) * bs + c * kc, kc)
            s = _dot(k_ref[0, pl.ds(start, kc), :], qts[hh])
            if own:
                s = jnp.where(krow + c * kc <= qcol, s, NEG)
            scores.append(s)
        parts = []
        for s in scores:
            mb = jnp.max(s, axis=0, keepdims=True)
            p = jnp.exp2(s - mb)
            parts.append((mb, jnp.sum(p, axis=0, keepdims=True), p.astype(BF16)))
        pvs = [_dot(vt_scr[j0 + u, hh, :, c * kc:(c + 1) * kc], parts[i][2])
               for i, (u, c, hh) in enumerate(units)]
        for i, (u, c, hh) in enumerate(units):
            sel_row = None if own else sel_scr[hh, pl.ds(j0 + u, 1), :] > 0.0
            merge(hh, (parts[i][0], parts[i][1], pvs[i]), sel_row)

    def group_body(g, c_):
        blocks(g * MOBA_UNROLL, MOBA_UNROLL, False)
        return c_

    def single_body(j, c_):
        blocks(j, 1, False)
        return c_

    n_groups = qi // MOBA_UNROLL
    lax.fori_loop(0, n_groups, group_body, 0)
    lax.fori_loop(n_groups * MOBA_UNROLL, qi, single_body, 0)
    blocks(qi, 1, True)
    outs = [acc_scr[hh] / l_scr[hh] for hh in range(n_heads)]
    o_ref[0] = jnp.concatenate(outs, axis=0).T.astype(BF16)


def moba_prompt(qm, kmb, vmb, kmean):
    b, t, w = qm.shape
    nb = t // MOBA_BLOCK
    n_heads = LANES // HEAD_DIM
    return pl.pallas_call(
        _moba_prompt_kernel,
        grid=(b, w // LANES, nb),
        in_specs=[
            pl.BlockSpec((1, MOBA_BLOCK, LANES), lambda bi, hp, qi: (bi, qi, hp)),
            pl.BlockSpec((1, t, LANES), lambda bi, hp, qi: (bi, 0, hp)),
            pl.BlockSpec((1, t, LANES), lambda bi, hp, qi: (bi, 0, hp)),
            pl.BlockSpec((1, nb, LANES), lambda bi, hp, qi: (bi, 0, hp)),
        ],
        out_specs=pl.BlockSpec((1, MOBA_BLOCK, LANES), lambda bi, hp, qi: (bi, qi, hp)),
        out_shape=jax.ShapeDtypeStruct((b, t, w), BF16),
        scratch_shapes=[pltpu.VMEM((nb, n_heads, HEAD_DIM, MOBA_BLOCK), BF16),
                        pltpu.VMEM((n_heads, nb, MOBA_BLOCK), F32),
                        pltpu.VMEM((n_heads, 1, MOBA_BLOCK), F32),
                        pltpu.VMEM((n_heads, 1, MOBA_BLOCK), F32),
                        pltpu.VMEM((n_heads, HEAD_DIM, MOBA_BLOCK), F32)],
        compiler_params=_cparams(("parallel", "parallel", "arbitrary")),
        name="moba_prompt",
    )(qm, kmb, vmb, kmean)


def _ret_sample_kernel(q_ref, k_ref, v_ref, g_ref, gam_ref, gn_ref, s0_ref, e1_ref, e2_ref,
                       o_ref, s_ref):
    e1 = e1_ref[...]
    e2 = e2_ref[...]

    def expand(x, e):
        hi, mid, lo = _split3(x)
        return _dot(hi, e) + _dot(mid, e) + _dot(lo, e)

    kexp = expand(k_ref[...], e1)
    vexp = expand(v_ref[...], e2)
    qexp = expand(q_ref[...], e1)
    s_new = gam_ref[:, :1] * s0_ref[...] + kexp * vexp
    s_ref[...] = s_new
    hi, mid, lo = _split3(qexp * s_new)
    o = _dot_nt(hi, e2) + _dot_nt(mid, e2) + _dot_nt(lo, e2)
    mu = jnp.mean(o, axis=-1, keepdims=True)
    oc = o - mu
    var = jnp.mean(oc * oc, axis=-1, keepdims=True)
    o_ref[...] = oc * lax.rsqrt(var + EPS) * gn_ref[...] * _silu(g_ref[...])


def retention_sample(q, k, v, gr, gn, s0):
    n = q.shape[0]
    r = n * H_RET
    dd = HEAD_DIM * HEAD_DIM
    rows = lambda a: a.reshape(r, HEAD_DIM)
    gam = jnp.tile(jnp.broadcast_to(jnp.exp(_ret_log_decay())[:, None], (H_RET, HEAD_DIM)), (n, 1))
    gnr = jnp.tile(gn.reshape(H_RET, HEAD_DIM), (n, 1))
    lane = jnp.arange(dd)
    e1 = (lane[None, :] // HEAD_DIM == jnp.arange(HEAD_DIM)[:, None]).astype(BF16)
    e2 = (lane[None, :] % HEAD_DIM == jnp.arange(HEAD_DIM)[:, None]).astype(BF16)
    tr = min(128, r)
    small = pl.BlockSpec((tr, HEAD_DIM), lambda i: (i, 0))
    big = pl.BlockSpec((tr, dd), lambda i: (i, 0))
    emat = pl.BlockSpec((HEAD_DIM, dd), lambda i: (0, 0))
    o, s_new = pl.pallas_call(
        _ret_sample_kernel,
        grid=(r // tr,),
        in_specs=[small, small, small, small, small, small, big, emat, emat],
        out_specs=[small, big],
        out_shape=[jax.ShapeDtypeStruct((r, HEAD_DIM), F32),
                   jax.ShapeDtypeStruct((r, dd), F32)],
        compiler_params=_cparams(("parallel",)),
        name="retention_sample",
    )(rows(q), rows(k), rows(v), rows(gr), gam, gnr, s0.reshape(r, dd), e1, e2)
    return o.reshape(n, W_RET), s_new.reshape(n, H_RET, HEAD_DIM, HEAD_DIM)


def _moba_sample_kernel(pt_ref, q_ref, kn_ref, vn_ref, *refs, n_pages):
    del pt_ref
    k_refs = refs[:n_pages]
    v_refs = refs[n_pages:2 * n_pages]
    o_ref = refs[2 * n_pages]
    scale = HEAD_DIM ** -0.5
    ppb = MOBA_BLOCK // PAGE_SIZE
    nbp = n_pages // ppb
    q = q_ref[...] * (scale * math.log2(math.e))
    s_pages = [jnp.sum(k_refs[p][...] * q, axis=-1, keepdims=True)
               for p in range(n_pages)]
    gates, maxes = [], []
    for j in range(nbp):
        tot = s_pages[j * ppb]
        top = s_pages[j * ppb]
        for p in range(1, ppb):
            tot = tot + s_pages[j * ppb + p]
            top = jnp.maximum(top, s_pages[j * ppb + p])
        gates.append(jnp.sum(tot, axis=0, keepdims=True))
        maxes.append(jnp.max(top, axis=0, keepdims=True))
    k_sel = min(MOBA_TOPK, nbp + 1)
    sels = []
    for j in range(nbp):
        rank = jnp.zeros(gates[j].shape, F32)
        for i in range(nbp):
            if i == j:
                continue
            ahead = (gates[i] > gates[j]) | ((gates[i] == gates[j]) & (i < j))
            rank = rank + jnp.where(ahead, 1.0, 0.0)
        sels.append(rank < k_sel)
    s_own = jnp.sum(q * kn_ref[...], axis=-1, keepdims=True)
    m = s_own
    for j in range(nbp):
        m = jnp.maximum(m, jnp.where(sels[j], maxes[j], NEG))
    p_own = jnp.exp2(s_own - m)
    l = p_own
    acc = p_own * vn_ref[...]
    for p in range(n_pages):
        shift = jnp.where(sels[p // ppb], m, -NEG)
        pp = jnp.exp2(s_pages[p] - shift)
        l = l + jnp.sum(pp, axis=0, keepdims=True)
        acc = acc + jnp.sum(pp * v_refs[p][...], axis=0, keepdims=True)
    o_ref[...] = acc / l


def moba_sample(q, k_new, v_new, cache_k, cache_v, page_table, layer):
    n = q.shape[0]
    n_pages = page_table.shape[1]
    vec = pl.BlockSpec((1, H_MOBA, HEAD_DIM), lambda s, pt: (s, 0, 0))

    def page_spec(p):
        return pl.BlockSpec((None, None, PAGE_SIZE, H_MOBA, HEAD_DIM),
                            lambda s, pt: (layer, pt[s, p], 0, 0, 0))

    pages = [page_spec(p) for p in range(n_pages)]
    return pl.pallas_call(
        functools.partial(_moba_sample_kernel, n_pages=n_pages),
        grid_spec=pltpu.PrefetchScalarGridSpec(
            num_scalar_prefetch=1,
            grid=(n,),
            in_specs=[vec, vec, vec] + pages + pages,
            out_specs=vec,
        ),
        out_shape=jax.ShapeDtypeStruct((n, H_MOBA, HEAD_DIM), F32),
        compiler_params=_cparams(("arbitrary",)),
        name="moba_sample",
    )(page_table, q, k_new, v_new, *([cache_k] * n_pages), *([cache_v] * n_pages))


def _out_proj_kernel(or_ref, om_ref, x_ref, w_ref, g_ref, x1_ref, h2_ref):
    x1 = (x_ref[...] + _dot(or_ref[...], w_ref[:W_RET, :])
          + _dot(om_ref[...], w_ref[W_RET:, :]))
    x1_ref[...] = x1
    ms = jnp.mean(x1 * x1, axis=-1, keepdims=True)
    h2_ref[...] = x1 * lax.rsqrt(ms + EPS) * g_ref[...]


def out_proj(o_r, o_m, x, w_bf, g, *, tm):
    n, d = x.shape
    row = lambda i: (i, 0)
    fixed = lambda i: (0, 0)
    return pl.pallas_call(
        _out_proj_kernel,
        grid=(n // tm,),
        in_specs=[pl.BlockSpec((tm, W_RET), row), pl.BlockSpec((tm, W_MOBA), row),
                  pl.BlockSpec((tm, d), row), pl.BlockSpec(w_bf.shape, fixed),
                  pl.BlockSpec((1, d), fixed)],
        out_specs=[pl.BlockSpec((tm, d), row), pl.BlockSpec((tm, d), row)],
        out_shape=[jax.ShapeDtypeStruct((n, d), F32), jax.ShapeDtypeStruct((n, d), F32)],
        compiler_params=_cparams(("parallel",)),
        name="out_proj",
    )(o_r, o_m, x, w_bf, g)


def _ffn_kernel(x1_ref, h_ref, wg_ref, wu_ref, wd_ref, o_ref, acc_ref):
    f = pl.program_id(1)

    @pl.when(f == 0)
    def _():
        acc_ref[...] = x1_ref[...]

    h = h_ref[...].astype(BF16)
    a = _silu(_dot(h, wg_ref[...])) * _dot(h, wu_ref[...])
    acc_ref[...] += _dot(a.astype(BF16), wd_ref[...])

    @pl.when(f == pl.num_programs(1) - 1)
    def _():
        o_ref[...] = acc_ref[...]


def _ff_tile(d_ff):
    for tf in (512, 256, 128):
        if d_ff % tf == 0:
            return tf
    return d_ff


def ffn_dense(x1, h2, wg, wu, wd, *, tm):
    n, d = x1.shape
    d_ff = wg.shape[1]
    tf = _ff_tile(d_ff)
    row = lambda i, f: (i, 0)
    return pl.pallas_call(
        _ffn_kernel,
        grid=(n // tm, d_ff // tf),
        in_specs=[pl.BlockSpec((tm, d), row), pl.BlockSpec((tm, d), row),
                  pl.BlockSpec((d, tf), lambda i, f: (0, f)),
                  pl.BlockSpec((d, tf), lambda i, f: (0, f)),
                  pl.BlockSpec((tf, d), lambda i, f: (f, 0))],
        out_specs=pl.BlockSpec((tm, d), row),
        out_shape=jax.ShapeDtypeStruct((n, d), F32),
        scratch_shapes=[pltpu.VMEM((tm, d), F32)],
        compiler_params=_cparams(("parallel", "arbitrary")),
        name="ffn_dense",
    )(x1, h2, wg, wu, wd)


ROUTE_E, ROUTE_RANK, ROUTE_W = 0, 2, 4


def _router_kernel(h_ref, w_ref, route_ref, cnt_ref, carry_scr):
    @pl.when(pl.program_id(0) == 0)
    def _():
        carry_scr[...] = jnp.zeros_like(carry_scr)

    logits = jnp.dot(h_ref[...], w_ref[...], precision=HIGHEST, preferred_element_type=F32)
    tm, n_e = logits.shape
    eid = lax.broadcasted_iota(jnp.int32, logits.shape, 1)
    v1 = jnp.max(logits, axis=1, keepdims=True)
    i1 = jnp.min(jnp.where(logits == v1, eid, n_e), axis=1, keepdims=True)
    rest = jnp.where(eid == i1, -jnp.inf, logits)
    v2 = jnp.max(rest, axis=1, keepdims=True)
    i2 = jnp.min(jnp.where(rest == v2, eid, n_e), axis=1, keepdims=True)
    e2 = jnp.exp(v2 - v1)
    w1 = 1.0 / (1.0 + e2)
    w2 = e2 / (1.0 + e2)
    onehot = jnp.where((eid == i1) | (eid == i2), 1.0, 0.0)
    r = lax.broadcasted_iota(jnp.int32, (tm, tm), 0)
    c = lax.broadcasted_iota(jnp.int32, (tm, tm), 1)
    earlier = jnp.where(c < r, 1.0, 0.0).astype(BF16)
    ranks = _dot(earlier, onehot.astype(BF16)) + carry_scr[...]
    r1 = jnp.sum(jnp.where(eid == i1, ranks, 0.0), axis=1, keepdims=True)
    r2 = jnp.sum(jnp.where(eid == i2, ranks, 0.0), axis=1, keepdims=True)
    total = carry_scr[...] + jnp.sum(onehot, axis=0, keepdims=True)
    carry_scr[...] = total
    cnt_ref[...] = total
    route = jnp.zeros(logits.shape, F32)
    for k, col in enumerate((i1.astype(F32), i2.astype(F32), r1, r2, w1, w2)):
        route = jnp.where(eid == k, col, route)
    route_ref[...] = route


def router(h2, w_router, *, tm):
    n, d = h2.shape
    n_e = w_router.shape[1]
    return pl.pallas_call(
        _router_kernel,
        grid=(n // tm,),
        in_specs=[pl.BlockSpec((tm, d), lambda i: (i, 0)),
                  pl.BlockSpec((d, n_e), lambda i: (0, 0))],
        out_specs=[pl.BlockSpec((tm, n_e), lambda i: (i, 0)),
                   pl.BlockSpec((1, n_e), lambda i: (0, 0))],
        out_shape=[jax.ShapeDtypeStruct((n, n_e), F32), jax.ShapeDtypeStruct((1, n_e), F32)],
        scratch_shapes=[pltpu.VMEM((1, n_e), F32)],
        compiler_params=_cparams(("arbitrary",)),
        name="router",
    )(h2, w_router)


def _dispatch_kernel(slots_ref, h_ref, xs_in_ref, xs_ref, sem):
    del xs_in_ref
    tm = h_ref.shape[0]
    base = pl.program_id(0) * tm

    def start(r, c):
        for k in range(TOP_K):
            slot = slots_ref[k, base + r]
            pltpu.make_async_copy(h_ref.at[pl.ds(r, 1)], xs_ref.at[pl.ds(slot, 1)], sem).start()
        return c

    lax.fori_loop(0, tm, start, 0)

    def wait(r, c):
        for k in range(TOP_K):
            pltpu.make_async_copy(h_ref.at[pl.ds(0, 1)], xs_ref.at[pl.ds(0, 1)], sem).wait()
        return c

    lax.fori_loop(0, tm, wait, 0)


def moe_dispatch(slots, h2, n_slots, *, tm):
    n, d = h2.shape
    xs0 = jnp.zeros((n_slots, d), h2.dtype)
    return pl.pallas_call(
        _dispatch_kernel,
        grid_spec=pltpu.PrefetchScalarGridSpec(
            num_scalar_prefetch=1,
            grid=(n // tm,),
            in_specs=[pl.BlockSpec((tm, d), lambda i, sl: (i, 0)),
                      pl.BlockSpec(memory_space=pl.ANY)],
            out_specs=pl.BlockSpec(memory_space=pl.ANY),
            scratch_shapes=[pltpu.SemaphoreType.DMA(())],
        ),
        out_shape=jax.ShapeDtypeStruct((n_slots, d), h2.dtype),
        input_output_aliases={2: 0},
        compiler_params=_cparams(("arbitrary",)),
        name="moe_dispatch",
    )(slots, h2, xs0)


def _experts_kernel(te_ref, nv_ref, xs_ref, wg_ref, wu_ref, wd_ref, ys_ref, acc_ref):
    del te_ref
    s = pl.program_id(0)
    f = pl.program_id(1)

    @pl.when(s < nv_ref[0])
    def _():
        @pl.when(f == 0)
        def _():
            acc_ref[...] = jnp.zeros_like(acc_ref)

        x = xs_ref[...].astype(BF16)
        a = _silu(_dot(x, wg_ref[0])) * _dot(x, wu_ref[0])
        acc_ref[...] += _dot(a.astype(BF16), wd_ref[0])

        @pl.when(f == pl.num_programs(1) - 1)
        def _():
            ys_ref[...] = acc_ref[...]

    @pl.when((s >= nv_ref[0]) & (f == 0))
    def _():
        ys_ref[...] = jnp.zeros_like(ys_ref)


def moe_experts(tile_expert, n_valid, xs, wg, wu, wd):
    n_slots, d = xs.shape
    d_ff = wg.shape[2]
    tf = _ff_tile(d_ff)
    nf = d_ff // tf

    def live(s, nv):
        return jnp.minimum(s, nv[0] - 1)

    def fcol(s, f, nv):
        return jnp.where(s < nv[0], f, nf - 1)

    return pl.pallas_call(
        _experts_kernel,
        grid_spec=pltpu.PrefetchScalarGridSpec(
            num_scalar_prefetch=2,
            grid=(n_slots // MOE_TILE, nf),
            in_specs=[pl.BlockSpec((MOE_TILE, d), lambda s, f, te, nv: (live(s, nv), 0)),
                      pl.BlockSpec((1, d, tf), lambda s, f, te, nv: (te[s], 0, fcol(s, f, nv))),
                      pl.BlockSpec((1, d, tf), lambda s, f, te, nv: (te[s], 0, fcol(s, f, nv))),
                      pl.BlockSpec((1, tf, d), lambda s, f, te, nv: (te[s], fcol(s, f, nv), 0))],
            out_specs=pl.BlockSpec((MOE_TILE, d), lambda s, f, te, nv: (s, 0)),
            scratch_shapes=[pltpu.VMEM((MOE_TILE, d), F32)],
        ),
        out_shape=jax.ShapeDtypeStruct((n_slots, d), F32),
        compiler_params=_cparams(("arbitrary", "arbitrary")),
        name="moe_experts",
    )(tile_expert, n_valid, xs, wg, wu, wd)


def _combine_kernel(slots_ref, x1_ref, route_ref, ys_ref, o_ref, g_scr, sem):
    tm = x1_ref.shape[0]
    base = pl.program_id(0) * tm

    def start(r, c):
        for k in range(TOP_K):
            slot = slots_ref[k, base + r]
            pltpu.make_async_copy(ys_ref.at[pl.ds(slot, 1)], g_scr.at[k, pl.ds(r, 1)], sem).start()
        return c

    lax.fori_loop(0, tm, start, 0)

    def wait(r, c):
        for k in range(TOP_K):
            pltpu.make_async_copy(ys_ref.at[pl.ds(0, 1)], g_scr.at[0, pl.ds(0, 1)], sem).wait()
        return c

    lax.fori_loop(0, tm, wait, 0)
    route = route_ref[...]
    out = x1_ref[...]
    for k in range(TOP_K):
        out = out + route[:, ROUTE_W + k:ROUTE_W + k + 1] * g_scr[k]
    o_ref[...] = out


def moe_combine(slots, x1, route, ys, *, tm):
    n, d = x1.shape
    return pl.pallas_call(
        _combine_kernel,
        grid_spec=pltpu.PrefetchScalarGridSpec(
            num_scalar_prefetch=1,
            grid=(n // tm,),
            in_specs=[pl.BlockSpec((tm, d), lambda i, sl: (i, 0)),
                      pl.BlockSpec((tm, route.shape[1]), lambda i, sl: (i, 0)),
                      pl.BlockSpec(memory_space=pl.ANY)],
            out_specs=pl.BlockSpec((tm, d), lambda i, sl: (i, 0)),
            scratch_shapes=[pltpu.VMEM((TOP_K, tm, d), F32), pltpu.SemaphoreType.DMA(())],
        ),
        out_shape=jax.ShapeDtypeStruct((n, d), F32),
        compiler_params=_cparams(("arbitrary",)),
        name="moe_combine",
    )(slots, x1, route, ys)


def moe_routed(x1, h2, w_router, wg, wu, wd, *, tm):
    n, _ = x1.shape
    n_e = wg.shape[0]
    route, cnt = router(h2, w_router, tm=tm)
    counts = cnt[0].astype(jnp.int32)
    padded = (counts + MOE_TILE - 1) // MOE_TILE * MOE_TILE
    ends = jnp.cumsum(padded)
    starts = ends - padded
    experts = route[:, ROUTE_E:ROUTE_E + TOP_K].astype(jnp.int32)
    ranks = route[:, ROUTE_RANK:ROUTE_RANK + TOP_K].astype(jnp.int32)
    slots = (starts[experts] + ranks).T
    n_tiles = (TOP_K * n + MOE_TILE - 1) // MOE_TILE + n_e
    tile_start = jnp.arange(n_tiles, dtype=jnp.int32) * MOE_TILE
    tile_expert = jnp.minimum(jnp.sum(tile_start[:, None] >= ends[None, :], axis=1), n_e - 1)
    n_valid = (ends[-1] // MOE_TILE).reshape(1)
    rows = min(MOE_ROW_TILE, n)
    xs = moe_dispatch(slots, h2, n_tiles * MOE_TILE, tm=rows)
    ys = moe_experts(tile_expert.astype(jnp.int32), n_valid.astype(jnp.int32), xs, wg, wu, wd)
    return moe_combine(slots, x1, route, ys, tm=rows)


def combine_matrix(route, n_e):
    eid = jnp.arange(n_e, dtype=F32)[None, :]
    comb = jnp.zeros((route.shape[0], n_e), F32)
    for k in range(TOP_K):
        comb = comb + jnp.where(route[:, ROUTE_E + k:ROUTE_E + k + 1] == eid,
                                route[:, ROUTE_W + k:ROUTE_W + k + 1], 0.0)
    return comb


def _moe_kernel(x1_ref, h_ref, c_ref, wg_ref, wu_ref, wd_ref, o_ref, acc_ref):
    e = pl.program_id(1)
    f = pl.program_id(2)

    @pl.when((e == 0) & (f == 0))
    def _():
        acc_ref[...] = x1_ref[...]

    n_e = c_ref.shape[1]
    eid = lax.broadcasted_iota(jnp.int32, c_ref.shape, 1)
    ce = jnp.sum(jnp.where(eid == e, c_ref[...], 0.0), axis=1, keepdims=True)
    h = h_ref[...].astype(BF16)
    a = _silu(_dot(h, wg_ref[0])) * _dot(h, wu_ref[0])
    acc_ref[...] += ce * _dot(a.astype(BF16), wd_ref[0])

    @pl.when((e == pl.num_programs(1) - 1) & (f == pl.num_programs(2) - 1))
    def _():
        o_ref[...] = acc_ref[...]


def moe_dense(x1, h2, comb, wg, wu, wd, *, tm):
    n, d = x1.shape
    n_e, _, d_ff = wg.shape
    tf = _ff_tile(d_ff)
    row = lambda i, e, f: (i, 0)
    return pl.pallas_call(
        _moe_kernel,
        grid=(n // tm, n_e, d_ff // tf),
        in_specs=[pl.BlockSpec((tm, d), row), pl.BlockSpec((tm, d), row),
                  pl.BlockSpec((tm, n_e), row),
                  pl.BlockSpec((1, d, tf), lambda i, e, f: (e, 0, f)),
                  pl.BlockSpec((1, d, tf), lambda i, e, f: (e, 0, f)),
                  pl.BlockSpec((1, tf, d), lambda i, e, f: (e, f, 0))],
        out_specs=pl.BlockSpec((tm, d), row),
        out_shape=jax.ShapeDtypeStruct((n, d), F32),
        scratch_shapes=[pltpu.VMEM((tm, d), F32)],
        compiler_params=_cparams(("parallel", "arbitrary", "arbitrary")),
        name="moe_dense",
    )(x1, h2, comb, wg, wu, wd)


def _rope_tables(pos):
    half = HEAD_DIM // 2
    inv = ROPE_BASE ** (-jnp.arange(half, dtype=F32) / half)
    ang = pos.astype(F32)[:, None] * inv[None, :]
    cos, sin = jnp.cos(ang), jnp.sin(ang)
    reps = LANES // HEAD_DIM
    cos_t = jnp.tile(jnp.concatenate([cos, cos], axis=1), (1, reps))
    sin_t = jnp.tile(jnp.concatenate([-sin, sin], axis=1), (1, reps))
    return cos_t, sin_t


def _token_tile(n):
    for tm in (512, 256, 128, 64, 32, 16, 8):
        if n % tm == 0:
            return tm
    return n


def kernel(x_prompt, x_sample, cache_k, cache_v, state_ret, page_table, g_mix, w_in, ret_gn,
           q_norm_g, k_norm_g, w_out, g_ffn, w_ffn_gate, w_ffn_up, w_ffn_down, w_router,
           w_exp_gate, w_exp_up, w_exp_down):
    b, t, d = x_prompt.shape
    db, ds, _ = x_sample.shape
    depth = w_in.shape[0]
    n_phys = cache_k.shape[1]
    n_p, n_s = b * t, db * ds
    tm_p, tm_s = _token_tile(n_p), _token_tile(n_s)
    past_len = page_table.shape[1] * PAGE_SIZE

    cos_p, sin_p = _rope_tables(jnp.arange(t, dtype=jnp.int32))
    cos_s, sin_s = _rope_tables(jnp.broadcast_to(
        past_len + jnp.arange(ds, dtype=jnp.int32)[None, :], (db, ds)).reshape(n_s))
    head_mean = (jnp.kron(jnp.eye(H_MOBA, dtype=F32), jnp.ones((HEAD_DIM, HEAD_DIM), F32))
                 / HEAD_DIM).astype(BF16)

    xp = x_prompt.reshape(n_p, d)
    xs = x_sample.reshape(n_s, d)
    sp_l, kp_l, vp_l, ss_l, ks_l, vs_l = [], [], [], [], [], []
    for l in range(depth):
        w_in_b = w_in[l].astype(BF16)
        w_out_b = w_out[l].astype(BF16)
        g1 = g_mix[l].reshape(1, d)
        g2 = g_ffn[l].reshape(1, d)
        qg = jnp.tile(q_norm_g[l], H_MOBA).reshape(1, W_MOBA)
        kg = jnp.tile(k_norm_g[l], H_MOBA).reshape(1, W_MOBA)
        gn = ret_gn[l].reshape(1, W_RET)

        (qr, kr, vr, gr, qm, km, vm, kmb, kmean) = proj_in(
            xp, g1, w_in_b, cos_p, sin_p, qg, kg, head_mean, tm=tm_p, with_kmean=True)
        seq = lambda a: a.reshape(b, t, a.shape[-1])
        o_r, s_fin = retention_prompt(seq(qr), seq(kr), seq(vr), seq(gr), gn)
        o_m = moba_prompt(seq(qm), seq(kmb), seq(vm), kmean.reshape(b, t // MOBA_BLOCK, W_MOBA))
        x1p, h2p = out_proj(o_r.reshape(n_p, W_RET), o_m.reshape(n_p, W_MOBA), xp, w_out_b, g2,
                            tm=tm_p)
        sp_l.append(s_fin)
        kp_l.append(km.reshape(b, t, H_MOBA, HEAD_DIM))
        vp_l.append(vm.reshape(b, t, H_MOBA, HEAD_DIM))

        (qr, kr, vr, gr, qm, km, vm) = proj_in(
            xs, g1, w_in_b, cos_s, sin_s, qg, kg, head_mean, tm=tm_s, with_kmean=False)
        f32 = lambda a: a.astype(F32)
        o_r, s_new = retention_sample(f32(qr), f32(kr), f32(vr), f32(gr), gn, state_ret[l])
        heads = lambda a: a.reshape(n_s, H_MOBA, HEAD_DIM)
        o_m = moba_sample(heads(qm), heads(km), heads(vm), cache_k, cache_v, page_table,
                          l).reshape(n_s, W_MOBA)
        x1s, h2s = out_proj(o_r.astype(BF16), o_m.astype(BF16), xs, w_out_b, g2, tm=tm_s)
        ss_l.append(s_new)
        ks_l.append(km.reshape(db, ds, H_MOBA, HEAD_DIM))
        vs_l.append(vm.reshape(db, ds, H_MOBA, HEAD_DIM))

        i = l // 2
        if l % 2 == 0:
            wg, wu, wd = (w_ffn_gate[i].astype(BF16), w_ffn_up[i].astype(BF16),
                          w_ffn_down[i].astype(BF16))
            xp = ffn_dense(x1p, h2p, wg, wu, wd, tm=tm_p)
            xs = ffn_dense(x1s, h2s, wg, wu, wd, tm=tm_s)
        else:
            wg, wu, wd = (w_exp_gate[i].astype(BF16), w_exp_up[i].astype(BF16),
                          w_exp_down[i].astype(BF16))
            xp = moe_routed(x1p, h2p, w_router[i], wg, wu, wd, tm=tm_p)
            route_s, _ = router(h2s, w_router[i], tm=tm_s)
            xs = moe_dense(x1s, h2s, combine_matrix(route_s, wg.shape[0]), wg, wu, wd, tm=tm_s)

    return (xp.reshape(b, t, d), xs.reshape(db, ds, d),
            jnp.stack(sp_l), jnp.stack(kp_l), jnp.stack(vp_l),
            jnp.stack(ss_l), jnp.stack(ks_l), jnp.stack(vs_l))
```

```python
import functools
import math

import jax
import jax.numpy as jnp
from jax import lax
from jax.experimental import pallas as pl
from jax.experimental.pallas import tpu as pltpu

F32 = jnp.float32
BF16 = jnp.bfloat16

HEAD_DIM = 64
H_RET = 8
H_MOBA = 8
W_RET = H_RET * HEAD_DIM
W_MOBA = H_MOBA * HEAD_DIM
PAGE_SIZE = 128
ROPE_BASE = 10000.0
MOBA_BLOCK = 256
MOBA_TOPK = 3
TOP_K = 2
EPS = 1e-6
RET_CHUNK = 256
MOBA_KEY_CHUNK = 128
MOBA_UNROLL = 4
MOE_TILE = 512
MOE_ROW_TILE = 256
LANES = 128
NEG = -1e30
VMEM_LIMIT = 56 * 1024 * 1024

HIGHEST = lax.Precision.HIGHEST


def _cparams(sem):
    return pltpu.CompilerParams(dimension_semantics=sem, vmem_limit_bytes=VMEM_LIMIT)


def _silu(x):
    return x / (1.0 + jnp.exp(-x))


def _dot(a, b):
    return jnp.dot(a, b, preferred_element_type=F32)


def _dot_nt(a, b, precision=None):
    return lax.dot_general(a, b, (((1,), (1,)), ((), ())), precision=precision,
                           preferred_element_type=F32)


def _dot_tn(a, b):
    return lax.dot_general(a, b, (((0,), (0,)), ((), ())), preferred_element_type=F32)


def _split3(x):
    hi = x.astype(BF16)
    r = x - hi.astype(F32)
    mid = r.astype(BF16)
    lo = (r - mid.astype(F32)).astype(BF16)
    return hi, mid, lo


def _proj_in_kernel(x_ref, g_ref, w_ref, cos_ref, sin_ref, qg_ref, kg_ref, hm_ref,
                    qr_ref, kr_ref, vr_ref, gr_ref, qm_ref, km_ref, vm_ref,
                    *prompt_refs):
    x = x_ref[...]
    tm = x.shape[0]
    ms = jnp.mean(x * x, axis=-1, keepdims=True)
    h = (x * lax.rsqrt(ms + EPS) * g_ref[...]).astype(BF16)

    def proj(c0, width):
        return _dot(h, w_ref[:, c0:c0 + width])

    cos = cos_ref[...]
    sin = sin_ref[...]
    lane = lax.broadcasted_iota(jnp.int32, (tm, LANES), 1)
    first_half = (lane % HEAD_DIM) < (HEAD_DIM // 2)

    def rope(z):
        outs = []
        for c in range(z.shape[1] // LANES):
            zc = z[:, c * LANES:(c + 1) * LANES]
            rot = jnp.where(first_half,
                            pltpu.roll(zc, LANES - HEAD_DIM // 2, 1),
                            pltpu.roll(zc, HEAD_DIM // 2, 1))
            outs.append(zc * cos + rot * sin)
        return jnp.concatenate(outs, axis=1)

    hm = hm_ref[...]

    def head_norm(z, g):
        sq = z * z
        hi = sq.astype(BF16)
        lo = (sq - hi.astype(F32)).astype(BF16)
        msq = _dot(hi, hm) + _dot(lo, hm)
        return z * lax.rsqrt(msq + EPS) * g

    qr_ref[...] = rope(proj(0, W_RET)).astype(BF16)
    kr_ref[...] = (rope(proj(W_RET, W_RET)) * (HEAD_DIM ** -0.5)).astype(BF16)
    vr_ref[...] = proj(2 * W_RET, W_RET).astype(BF16)
    gr_ref[...] = proj(3 * W_RET, W_RET).astype(BF16)
    base = 4 * W_RET
    qm_ref[...] = head_norm(proj(base, W_MOBA), qg_ref[...])
    km = head_norm(proj(base + W_MOBA, W_MOBA), kg_ref[...])
    km_ref[...] = km
    vm_ref[...] = proj(base + 2 * W_MOBA, W_MOBA)
    if prompt_refs:
        kmb_ref, kmean_ref = prompt_refs
        kmb_ref[...] = km.astype(BF16)
        for r in range(tm // MOBA_BLOCK):
            kmean_ref[0, r:r + 1, :] = jnp.mean(
                km[r * MOBA_BLOCK:(r + 1) * MOBA_BLOCK], axis=0, keepdims=True)


def proj_in(x, g, w_bf, cos, sin, qg, kg, hm, *, tm, with_kmean):
    n, d = x.shape
    n_pos_tiles = cos.shape[0] // tm
    row = lambda i: (i, 0)
    fixed = lambda i: (0, 0)
    wide = lambda dt: jax.ShapeDtypeStruct((n, W_RET), dt)
    out_shape = [wide(BF16), wide(BF16), wide(BF16), wide(BF16),
                 wide(F32), wide(F32), wide(F32)]
    out_specs = [pl.BlockSpec((tm, W_RET), row)] * 7
    if with_kmean:
        nb = tm // MOBA_BLOCK
        out_shape += [wide(BF16), jax.ShapeDtypeStruct((n // tm, nb, W_MOBA), F32)]
        out_specs += [pl.BlockSpec((tm, W_RET), row),
                      pl.BlockSpec((1, nb, W_MOBA), lambda i: (i, 0, 0))]
    return pl.pallas_call(
        _proj_in_kernel,
        grid=(n // tm,),
        in_specs=[
            pl.BlockSpec((tm, d), row),
            pl.BlockSpec((1, d), fixed),
            pl.BlockSpec(w_bf.shape, fixed),
            pl.BlockSpec((tm, LANES), lambda i: (i % n_pos_tiles, 0)),
            pl.BlockSpec((tm, LANES), lambda i: (i % n_pos_tiles, 0)),
            pl.BlockSpec((1, W_MOBA), fixed),
            pl.BlockSpec((1, W_MOBA), fixed),
            pl.BlockSpec((W_MOBA, W_MOBA), fixed),
        ],
        out_specs=out_specs,
        out_shape=out_shape,
        compiler_params=_cparams(("parallel",)),
        name="proj_in",
    )(x, g, w_bf, cos, sin, qg, kg, hm)


def _ret_prompt_kernel(q_ref, k_ref, v_ref, g_ref, gn_ref, dm_ref, cross_ref, kdec_ref,
                       gc_ref, o_ref, sfin_ref, s_scr):
    c = pl.program_id(1)

    @pl.when(c == 0)
    def _():
        s_scr[...] = jnp.zeros_like(s_scr)

    outs = []
    for h in range(H_RET):
        sl = slice(h * HEAD_DIM, (h + 1) * HEAD_DIM)
        q = q_ref[0, :, sl]
        k = k_ref[0, :, sl]
        v = v_ref[0, :, sl]
        s0 = s_scr[h]
        scores = _dot_nt(q, k) * dm_ref[h]
        o = _dot(scores.astype(BF16), v) + _dot(q, s0.astype(BF16)) * cross_ref[h]
        kd = (k.astype(F32) * kdec_ref[h]).astype(BF16)
        s_scr[h] = gc_ref[h] * s0 + _dot_tn(kd, v)
        mu = jnp.mean(o, axis=-1, keepdims=True)
        oc = o - mu
        var = jnp.mean(oc * oc, axis=-1, keepdims=True)
        on = oc * lax.rsqrt(var + EPS) * gn_ref[:, sl]
        outs.append(on * _silu(g_ref[0, :, sl].astype(F32)))
    o_ref[0] = jnp.concatenate(outs, axis=1).astype(BF16)

    @pl.when(c == pl.num_programs(1) - 1)
    def _():
        sfin_ref[0] = s_scr[...]


def _ret_log_decay():
    return jnp.log1p(-jnp.exp2(-5.0 - jnp.arange(H_RET, dtype=F32)))


def retention_prompt(qr, kr, vr, gr, gn):
    b, t, w = qr.shape
    c = min(RET_CHUNK, t)
    ld = _ret_log_decay()
    i = jnp.arange(c, dtype=F32)
    diff = i[:, None] - i[None, :]
    dmask = jnp.where(diff >= 0, jnp.exp(ld[:, None, None] * jnp.maximum(diff, 0.0)), 0.0)
    bc = lambda a: jnp.broadcast_to(a[:, :, None], (H_RET, c, HEAD_DIM))
    cross = bc(jnp.exp(ld[:, None] * (i + 1.0)[None, :]))
    kdec = bc(jnp.exp(ld[:, None] * (c - 1.0 - i)[None, :]))
    gc = jnp.broadcast_to(jnp.exp(ld * c)[:, None, None], (H_RET, HEAD_DIM, HEAD_DIM))
    tile = pl.BlockSpec((1, c, w), lambda bi, ci: (bi, ci, 0))
    const3 = lambda shape: pl.BlockSpec(shape, lambda bi, ci: (0, 0, 0))
    return pl.pallas_call(
        _ret_prompt_kernel,
        grid=(b, t // c),
        in_specs=[tile, tile, tile, tile,
                  pl.BlockSpec((1, w), lambda bi, ci: (0, 0)),
                  const3((H_RET, c, c)), const3((H_RET, c, HEAD_DIM)),
                  const3((H_RET, c, HEAD_DIM)), const3((H_RET, HEAD_DIM, HEAD_DIM))],
        out_specs=[tile,
                   pl.BlockSpec((1, H_RET, HEAD_DIM, HEAD_DIM), lambda bi, ci: (bi, 0, 0, 0))],
        out_shape=[jax.ShapeDtypeStruct((b, t, w), BF16),
                   jax.ShapeDtypeStruct((b, H_RET, HEAD_DIM, HEAD_DIM), F32)],
        scratch_shapes=[pltpu.VMEM((H_RET, HEAD_DIM, HEAD_DIM), F32)],
        compiler_params=_cparams(("parallel", "arbitrary")),
        name="retention_prompt",
    )(qr, kr, vr, gr, gn, dmask, cross, kdec, gc)


def _moba_prompt_kernel(q_ref, k_ref, v_ref, km_ref, o_ref, vt_scr, sel_scr, m_scr, l_scr,
                        acc_scr):
    qi = pl.program_id(2)
    nb = km_ref.shape[1]
    bs = MOBA_BLOCK
    kc = MOBA_KEY_CHUNK
    n_chunks = bs // kc
    n_heads = LANES // HEAD_DIM

    @pl.when(qi == 0)
    def _():
        def transpose_block(j, c):
            start = pl.multiple_of(j * bs, bs)
            vt = v_ref[0, pl.ds(start, bs), :].T
            for hh in range(n_heads):
                vt_scr[j, hh] = vt[hh * HEAD_DIM:(hh + 1) * HEAD_DIM].astype(BF16)
            return c
        lax.fori_loop(0, nb, transpose_block, 0)

    qt = (q_ref[0] * (HEAD_DIM ** -0.5 * math.log2(math.e))).T
    drow = lax.broadcasted_iota(jnp.int32, qt.shape, 0)
    blk = lax.broadcasted_iota(jnp.int32, (nb, bs), 0)
    kmean = km_ref[0]
    qts = []
    for hh in range(n_heads):
        qh = jnp.where(drow // HEAD_DIM == hh, qt, 0.0)
        qts.append(qh.astype(BF16))
        gate = jnp.dot(kmean, qh, precision=HIGHEST, preferred_element_type=F32)
        g = jnp.where(blk < qi, gate, -jnp.inf)
        sel = jnp.zeros((nb, bs), F32)
        for _ in range(min(MOBA_TOPK, nb)):
            mx = jnp.max(g, axis=0, keepdims=True)
            idx = jnp.min(jnp.where(g == mx, blk, nb), axis=0, keepdims=True)
            pick = blk == idx
            sel = jnp.where(pick, 1.0, sel)
            g = jnp.where(pick, -jnp.inf, g)
        sel_scr[hh] = jnp.where(blk < qi, sel, 0.0)

    m_scr[...] = jnp.full(m_scr.shape, NEG, F32)
    l_scr[...] = jnp.zeros(l_scr.shape, F32)
    acc_scr[...] = jnp.zeros(acc_scr.shape, F32)

    def merge(hh, stats, sel_row=None):
        m = m_scr[hh]
        mb, lb, pv = stats
        m_new = jnp.maximum(m, mb)
        wb = jnp.exp2(mb - m_new)
        if sel_row is not None:
            m_new = jnp.where(sel_row, m_new, m)
            wb = jnp.where(sel_row, wb, 0.0)
        wa = jnp.exp2(m - m_new)
        m_scr[hh] = m_new
        l_scr[hh] = wa * l_scr[hh] + wb * lb
        acc_scr[hh] = wa * acc_scr[hh] + wb * pv

    krow = lax.broadcasted_iota(jnp.int32, (kc, bs), 0)
    qcol = lax.broadcasted_iota(jnp.int32, (kc, bs), 1)

    def blocks(j0, n_blocks, own):
        units = [(u, c, hh) for u in range(n_blocks) for c in range(n_chunks)
                 for hh in range(n_heads)]
        scores = []
        for u, c, hh in units:
            start = pl.multiple_of((j0 + u) * bs + c * kc, kc)
            s = _dot(k_ref[0, pl.ds(start, kc), :], qts[hh])
            if own:
                s = jnp.where(krow + c * kc <= qcol, s, NEG)
            scores.append(s)
        parts = []
        for s in scores:
            mb = jnp.max(s, axis=0, keepdims=True)
            p = jnp.exp2(s - mb)
            parts.append((mb, jnp.sum(p, axis=0, keepdims=True), p.astype(BF16)))
        pvs = [_dot(vt_scr[j0 + u, hh, :, c * kc:(c + 1) * kc], parts[i][2])
               for i, (u, c, hh) in enumerate(units)]
        for i, (u, c, hh) in enumerate(units):
            sel_row = None if own else sel_scr[hh, pl.ds(j0 + u, 1), :] > 0.0
            merge(hh, (parts[i][0], parts[i][1], pvs[i]), sel_row)

    def group_body(g, c_):
        blocks(g * MOBA_UNROLL, MOBA_UNROLL, False)
        return c_

    def single_body(j, c_):
        blocks(j, 1, False)
        return c_

    n_groups = qi // MOBA_UNROLL
    lax.fori_loop(0, n_groups, group_body, 0)
    lax.fori_loop(n_groups * MOBA_UNROLL, qi, single_body, 0)
    blocks(qi, 1, True)
    outs = [acc_scr[hh] / l_scr[hh] for hh in range(n_heads)]
    o_ref[0] = jnp.concatenate(outs, axis=0).T.astype(BF16)


def moba_prompt(qm, kmb, vmb, kmean):
    b, t, w = qm.shape
    nb = t // MOBA_BLOCK
    n_heads = LANES // HEAD_DIM
    return pl.pallas_call(
        _moba_prompt_kernel,
        grid=(b, w // LANES, nb),
        in_specs=[
            pl.BlockSpec((1, MOBA_BLOCK, LANES), lambda bi, hp, qi: (bi, qi, hp)),
            pl.BlockSpec((1, t, LANES), lambda bi, hp, qi: (bi, 0, hp)),
            pl.BlockSpec((1, t, LANES), lambda bi, hp, qi: (bi, 0, hp)),
            pl.BlockSpec((1, nb, LANES), lambda bi, hp, qi: (bi, 0, hp)),
        ],
        out_specs=pl.BlockSpec((1, MOBA_BLOCK, LANES), lambda bi, hp, qi: (bi, qi, hp)),
        out_shape=jax.ShapeDtypeStruct((b, t, w), BF16),
        scratch_shapes=[pltpu.VMEM((nb, n_heads, HEAD_DIM, MOBA_BLOCK), BF16),
                        pltpu.VMEM((n_heads, nb, MOBA_BLOCK), F32),
                        pltpu.VMEM((n_heads, 1, MOBA_BLOCK), F32),
                        pltpu.VMEM((n_heads, 1, MOBA_BLOCK), F32),
                        pltpu.VMEM((n_heads, HEAD_DIM, MOBA_BLOCK), F32)],
        compiler_params=_cparams(("parallel", "parallel", "arbitrary")),
        name="moba_prompt",
    )(qm, kmb, vmb, kmean)


def _ret_sample_kernel(q_ref, k_ref, v_ref, g_ref, gam_ref, gn_ref, s0_ref, e1_ref, e2_ref,
                       o_ref, s_ref):
    e1 = e1_ref[...]
    e2 = e2_ref[...]

    def expand(x, e):
        hi, mid, lo = _split3(x)
        return _dot(hi, e) + _dot(mid, e) + _dot(lo, e)

    kexp = expand(k_ref[...], e1)
    vexp = expand(v_ref[...], e2)
    qexp = expand(q_ref[...], e1)
    s_new = gam_ref[:, :1] * s0_ref[...] + kexp * vexp
    s_ref[...] = s_new
    hi, mid, lo = _split3(qexp * s_new)
    o = _dot_nt(hi, e2) + _dot_nt(mid, e2) + _dot_nt(lo, e2)
    mu = jnp.mean(o, axis=-1, keepdims=True)
    oc = o - mu
    var = jnp.mean(oc * oc, axis=-1, keepdims=True)
    o_ref[...] = oc * lax.rsqrt(var + EPS) * gn_ref[...] * _silu(g_ref[...])


def retention_sample(q, k, v, gr, gn, s0):
    n = q.shape[0]
    r = n * H_RET
    dd = HEAD_DIM * HEAD_DIM
    rows = lambda a: a.reshape(r, HEAD_DIM)
    gam = jnp.tile(jnp.broadcast_to(jnp.exp(_ret_log_decay())[:, None], (H_RET, HEAD_DIM)), (n, 1))
    gnr = jnp.tile(gn.reshape(H_RET, HEAD_DIM), (n, 1))
    lane = jnp.arange(dd)
    e1 = (lane[None, :] // HEAD_DIM == jnp.arange(HEAD_DIM)[:, None]).astype(BF16)
    e2 = (lane[None, :] % HEAD_DIM == jnp.arange(HEAD_DIM)[:, None]).astype(BF16)
    tr = min(128, r)
    small = pl.BlockSpec((tr, HEAD_DIM), lambda i: (i, 0))
    big = pl.BlockSpec((tr, dd), lambda i: (i, 0))
    emat = pl.BlockSpec((HEAD_DIM, dd), lambda i: (0, 0))
    o, s_new = pl.pallas_call(
        _ret_sample_kernel,
        grid=(r // tr,),
        in_specs=[small, small, small, small, small, small, big, emat, emat],
        out_specs=[small, big],
        out_shape=[jax.ShapeDtypeStruct((r, HEAD_DIM), F32),
                   jax.ShapeDtypeStruct((r, dd), F32)],
        compiler_params=_cparams(("parallel",)),
        name="retention_sample",
    )(rows(q), rows(k), rows(v), rows(gr), gam, gnr, s0.reshape(r, dd), e1, e2)
    return o.reshape(n, W_RET), s_new.reshape(n, H_RET, HEAD_DIM, HEAD_DIM)


def _moba_sample_kernel(pt_ref, q_ref, qt_ref, kn_ref, vn_ref, *refs, n_pages):
    del pt_ref
    k_refs = refs[:n_pages]
    v_refs = refs[n_pages:2 * n_pages]
    o_ref = refs[2 * n_pages]
    ppb = MOBA_BLOCK // PAGE_SIZE
    nbp = n_pages // ppb
    qscale = HEAD_DIM ** -0.5 * math.log2(math.e)
    qt = qt_ref[0] * qscale
    qcols = [jnp.broadcast_to(qt[:, h:h + 1], (HEAD_DIM, PAGE_SIZE)) for h in range(H_MOBA)]
    s_pages = []
    for p in range(n_pages):
        rows = [jnp.sum(k_refs[p][h] * qcols[h], axis=0, keepdims=True) for h in range(H_MOBA)]
        s_pages.append(jnp.concatenate(rows, axis=0))
    gates, maxes = [], []
    for j in range(nbp):
        tot = s_pages[j * ppb]
        top = s_pages[j * ppb]
        for p in range(1, ppb):
            tot = tot + s_pages[j * ppb + p]
            top = jnp.maximum(top, s_pages[j * ppb + p])
        gates.append(jnp.sum(tot, axis=1, keepdims=True))
        maxes.append(jnp.max(top, axis=1, keepdims=True))
    k_sel = min(MOBA_TOPK, nbp + 1)
    sels = []
    for j in range(nbp):
        rank = jnp.zeros(gates[j].shape, F32)
        for i in range(nbp):
            if i == j:
                continue
            ahead = (gates[i] > gates[j]) | ((gates[i] == gates[j]) & (i < j))
            rank = rank + jnp.where(ahead, 1.0, 0.0)
        sels.append(rank < k_sel)
    s_own = jnp.sum(q_ref[0] * kn_ref[0], axis=1, keepdims=True) * qscale
    m = s_own
    for j in range(nbp):
        m = jnp.maximum(m, jnp.where(sels[j], maxes[j], NEG))
    p_own = jnp.exp2(s_own - m)
    l = p_own
    accs = [jnp.zeros((HEAD_DIM, PAGE_SIZE), F32) for _ in range(H_MOBA)]
    for p in range(n_pages):
        shift = jnp.where(sels[p // ppb], m, -NEG)
        pp = jnp.exp2(s_pages[p] - shift)
        l = l + jnp.sum(pp, axis=1, keepdims=True)
        for h in range(H_MOBA):
            accs[h] = accs[h] + pp[h:h + 1, :] * v_refs[p][h]
    o_past = jnp.concatenate([jnp.sum(a.T, axis=0, keepdims=True) for a in accs], axis=0)
    o_ref[0] = (o_past + p_own * vn_ref[0]) / l


def moba_sample(q, k_new, v_new, cache_k, cache_v, page_table, layer):
    n = q.shape[0]
    n_pages = page_table.shape[1]
    vec = pl.BlockSpec((1, H_MOBA, HEAD_DIM), lambda s, pt: (s, 0, 0))
    vec_t = pl.BlockSpec((1, HEAD_DIM, H_MOBA), lambda s, pt: (s, 0, 0))

    def page_spec(p):
        return pl.BlockSpec((None, None, H_MOBA, HEAD_DIM, PAGE_SIZE),
                            lambda s, pt: (layer, pt[s, p], 0, 0, 0))

    pages = [page_spec(p) for p in range(n_pages)]
    return pl.pallas_call(
        functools.partial(_moba_sample_kernel, n_pages=n_pages),
        grid_spec=pltpu.PrefetchScalarGridSpec(
            num_scalar_prefetch=1,
            grid=(n,),
            in_specs=[vec, vec_t, vec, vec] + pages + pages,
            out_specs=vec,
        ),
        out_shape=jax.ShapeDtypeStruct((n, H_MOBA, HEAD_DIM), F32),
        compiler_params=_cparams(("arbitrary",)),
        name="moba_sample",
    )(page_table, q, q.transpose(0, 2, 1), k_new, v_new,
      *([cache_k] * n_pages), *([cache_v] * n_pages))


def _out_proj_kernel(or_ref, om_ref, x_ref, w_ref, g_ref, x1_ref, h2_ref):
    x1 = (x_ref[...] + _dot(or_ref[...], w_ref[:W_RET, :])
          + _dot(om_ref[...], w_ref[W_RET:, :]))
    x1_ref[...] = x1
    ms = jnp.mean(x1 * x1, axis=-1, keepdims=True)
    h2_ref[...] = x1 * lax.rsqrt(ms + EPS) * g_ref[...]


def out_proj(o_r, o_m, x, w_bf, g, *, tm):
    n, d = x.shape
    row = lambda i: (i, 0)
    fixed = lambda i: (0, 0)
    return pl.pallas_call(
        _out_proj_kernel,
        grid=(n // tm,),
        in_specs=[pl.BlockSpec((tm, W_RET), row), pl.BlockSpec((tm, W_MOBA), row),
                  pl.BlockSpec((tm, d), row), pl.BlockSpec(w_bf.shape, fixed),
                  pl.BlockSpec((1, d), fixed)],
        out_specs=[pl.BlockSpec((tm, d), row), pl.BlockSpec((tm, d), row)],
        out_shape=[jax.ShapeDtypeStruct((n, d), F32), jax.ShapeDtypeStruct((n, d), F32)],
        compiler_params=_cparams(("parallel",)),
        name="out_proj",
    )(o_r, o_m, x, w_bf, g)


def _ffn_kernel(x1_ref, h_ref, wg_ref, wu_ref, wd_ref, o_ref, acc_ref):
    f = pl.program_id(1)

    @pl.when(f == 0)
    def _():
        acc_ref[...] = x1_ref[...]

    h = h_ref[...].astype(BF16)
    a = _silu(_dot(h, wg_ref[...])) * _dot(h, wu_ref[...])
    acc_ref[...] += _dot(a.astype(BF16), wd_ref[...])

    @pl.when(f == pl.num_programs(1) - 1)
    def _():
        o_ref[...] = acc_ref[...]


def _ff_tile(d_ff):
    for tf in (512, 256, 128):
        if d_ff % tf == 0:
            return tf
    return d_ff


def ffn_dense(x1, h2, wg, wu, wd, *, tm):
    n, d = x1.shape
    d_ff = wg.shape[1]
    tf = _ff_tile(d_ff)
    row = lambda i, f: (i, 0)
    return pl.pallas_call(
        _ffn_kernel,
        grid=(n // tm, d_ff // tf),
        in_specs=[pl.BlockSpec((tm, d), row), pl.BlockSpec((tm, d), row),
                  pl.BlockSpec((d, tf), lambda i, f: (0, f)),
                  pl.BlockSpec((d, tf), lambda i, f: (0, f)),
                  pl.BlockSpec((tf, d), lambda i, f: (f, 0))],
        out_specs=pl.BlockSpec((tm, d), row),
        out_shape=jax.ShapeDtypeStruct((n, d), F32),
        scratch_shapes=[pltpu.VMEM((tm, d), F32)],
        compiler_params=_cparams(("parallel", "arbitrary")),
        name="ffn_dense",
    )(x1, h2, wg, wu, wd)


ROUTE_E, ROUTE_RANK, ROUTE_W = 0, 2, 4


def _router_kernel(h_ref, w_ref, route_ref, cnt_ref, carry_scr):
    @pl.when(pl.program_id(0) == 0)
    def _():
        carry_scr[...] = jnp.zeros_like(carry_scr)

    logits = jnp.dot(h_ref[...], w_ref[...], precision=HIGHEST, preferred_element_type=F32)
    tm, n_e = logits.shape
    eid = lax.broadcasted_iota(jnp.int32, logits.shape, 1)
    v1 = jnp.max(logits, axis=1, keepdims=True)
    i1 = jnp.min(jnp.where(logits == v1, eid, n_e), axis=1, keepdims=True)
    rest = jnp.where(eid == i1, -jnp.inf, logits)
    v2 = jnp.max(rest, axis=1, keepdims=True)
    i2 = jnp.min(jnp.where(rest == v2, eid, n_e), axis=1, keepdims=True)
    e2 = jnp.exp(v2 - v1)
    w1 = 1.0 / (1.0 + e2)
    w2 = e2 / (1.0 + e2)
    onehot = jnp.where((eid == i1) | (eid == i2), 1.0, 0.0)
    r = lax.broadcasted_iota(jnp.int32, (tm, tm), 0)
    c = lax.broadcasted_iota(jnp.int32, (tm, tm), 1)
    earlier = jnp.where(c < r, 1.0, 0.0).astype(BF16)
    ranks = _dot(earlier, onehot.astype(BF16)) + carry_scr[...]
    r1 = jnp.sum(jnp.where(eid == i1, ranks, 0.0), axis=1, keepdims=True)
    r2 = jnp.sum(jnp.where(eid == i2, ranks, 0.0), axis=1, keepdims=True)
    total = carry_scr[...] + jnp.sum(onehot, axis=0, keepdims=True)
    carry_scr[...] = total
    cnt_ref[...] = total
    route = jnp.zeros(logits.shape, F32)
    for k, col in enumerate((i1.astype(F32), i2.astype(F32), r1, r2, w1, w2)):
        route = jnp.where(eid == k, col, route)
    route_ref[...] = route


def router(h2, w_router, *, tm):
    n, d = h2.shape
    n_e = w_router.shape[1]
    return pl.pallas_call(
        _router_kernel,
        grid=(n // tm,),
        in_specs=[pl.BlockSpec((tm, d), lambda i: (i, 0)),
                  pl.BlockSpec((d, n_e), lambda i: (0, 0))],
        out_specs=[pl.BlockSpec((tm, n_e), lambda i: (i, 0)),
                   pl.BlockSpec((1, n_e), lambda i: (0, 0))],
        out_shape=[jax.ShapeDtypeStruct((n, n_e), F32), jax.ShapeDtypeStruct((1, n_e), F32)],
        scratch_shapes=[pltpu.VMEM((1, n_e), F32)],
        compiler_params=_cparams(("arbitrary",)),
        name="router",
    )(h2, w_router)


def _dispatch_kernel(slots_ref, h_ref, xs_in_ref, xs_ref, sem):
    del xs_in_ref
    tm = h_ref.shape[0]
    base = pl.program_id(0) * tm

    def start(r, c):
        for k in range(TOP_K):
            slot = slots_ref[k, base + r]
            pltpu.make_async_copy(h_ref.at[pl.ds(r, 1)], xs_ref.at[pl.ds(slot, 1)], sem).start()
        return c

    lax.fori_loop(0, tm, start, 0)

    def wait(r, c):
        for k in range(TOP_K):
            pltpu.make_async_copy(h_ref.at[pl.ds(0, 1)], xs_ref.at[pl.ds(0, 1)], sem).wait()
        return c

    lax.fori_loop(0, tm, wait, 0)


def moe_dispatch(slots, h2, n_slots, *, tm):
    n, d = h2.shape
    xs0 = jnp.zeros((n_slots, d), h2.dtype)
    return pl.pallas_call(
        _dispatch_kernel,
        grid_spec=pltpu.PrefetchScalarGridSpec(
            num_scalar_prefetch=1,
            grid=(n // tm,),
            in_specs=[pl.BlockSpec((tm, d), lambda i, sl: (i, 0)),
                      pl.BlockSpec(memory_space=pl.ANY)],
            out_specs=pl.BlockSpec(memory_space=pl.ANY),
            scratch_shapes=[pltpu.SemaphoreType.DMA(())],
        ),
        out_shape=jax.ShapeDtypeStruct((n_slots, d), h2.dtype),
        input_output_aliases={2: 0},
        compiler_params=_cparams(("arbitrary",)),
        name="moe_dispatch",
    )(slots, h2, xs0)


def _experts_kernel(te_ref, nv_ref, xs_ref, wg_ref, wu_ref, wd_ref, ys_ref, acc_ref):
    del te_ref
    s = pl.program_id(0)
    f = pl.program_id(1)

    @pl.when(s < nv_ref[0])
    def _():
        @pl.when(f == 0)
        def _():
            acc_ref[...] = jnp.zeros_like(acc_ref)

        x = xs_ref[...].astype(BF16)
        a = _silu(_dot(x, wg_ref[0])) * _dot(x, wu_ref[0])
        acc_ref[...] += _dot(a.astype(BF16), wd_ref[0])

        @pl.when(f == pl.num_programs(1) - 1)
        def _():
            ys_ref[...] = acc_ref[...]

    @pl.when((s >= nv_ref[0]) & (f == 0))
    def _():
        ys_ref[...] = jnp.zeros_like(ys_ref)


def moe_experts(tile_expert, n_valid, xs, wg, wu, wd):
    n_slots, d = xs.shape
    d_ff = wg.shape[2]
    tf = _ff_tile(d_ff)
    nf = d_ff // tf

    def live(s, nv):
        return jnp.minimum(s, nv[0] - 1)

    def fcol(s, f, nv):
        return jnp.where(s < nv[0], f, nf - 1)

    return pl.pallas_call(
        _experts_kernel,
        grid_spec=pltpu.PrefetchScalarGridSpec(
            num_scalar_prefetch=2,
            grid=(n_slots // MOE_TILE, nf),
            in_specs=[pl.BlockSpec((MOE_TILE, d), lambda s, f, te, nv: (live(s, nv), 0)),
                      pl.BlockSpec((1, d, tf), lambda s, f, te, nv: (te[s], 0, fcol(s, f, nv))),
                      pl.BlockSpec((1, d, tf), lambda s, f, te, nv: (te[s], 0, fcol(s, f, nv))),
                      pl.BlockSpec((1, tf, d), lambda s, f, te, nv: (te[s], fcol(s, f, nv), 0))],
            out_specs=pl.BlockSpec((MOE_TILE, d), lambda s, f, te, nv: (s, 0)),
            scratch_shapes=[pltpu.VMEM((MOE_TILE, d), F32)],
        ),
        out_shape=jax.ShapeDtypeStruct((n_slots, d), F32),
        compiler_params=_cparams(("arbitrary", "arbitrary")),
        name="moe_experts",
    )(tile_expert, n_valid, xs, wg, wu, wd)


def _combine_kernel(slots_ref, x1_ref, route_ref, ys_ref, o_ref, g_scr, sem):
    tm = x1_ref.shape[0]
    base = pl.program_id(0) * tm

    def start(r, c):
        for k in range(TOP_K):
            slot = slots_ref[k, base + r]
            pltpu.make_async_copy(ys_ref.at[pl.ds(slot, 1)], g_scr.at[k, pl.ds(r, 1)], sem).start()
        return c

    lax.fori_loop(0, tm, start, 0)

    def wait(r, c):
        for k in range(TOP_K):
            pltpu.make_async_copy(ys_ref.at[pl.ds(0, 1)], g_scr.at[0, pl.ds(0, 1)], sem).wait()
        return c

    lax.fori_loop(0, tm, wait, 0)
    route = route_ref[...]
    out = x1_ref[...]
    for k in range(TOP_K):
        out = out + route[:, ROUTE_W + k:ROUTE_W + k + 1] * g_scr[k]
    o_ref[...] = out


def moe_combine(slots, x1, route, ys, *, tm):
    n, d = x1.shape
    return pl.pallas_call(
        _combine_kernel,
        grid_spec=pltpu.PrefetchScalarGridSpec(
            num_scalar_prefetch=1,
            grid=(n // tm,),
            in_specs=[pl.BlockSpec((tm, d), lambda i, sl: (i, 0)),
                      pl.BlockSpec((tm, route.shape[1]), lambda i, sl: (i, 0)),
                      pl.BlockSpec(memory_space=pl.ANY)],
            out_specs=pl.BlockSpec((tm, d), lambda i, sl: (i, 0)),
            scratch_shapes=[pltpu.VMEM((TOP_K, tm, d), F32), pltpu.SemaphoreType.DMA(())],
        ),
        out_shape=jax.ShapeDtypeStruct((n, d), F32),
        compiler_params=_cparams(("arbitrary",)),
        name="moe_combine",
    )(slots, x1, route, ys)


def moe_routed(x1, h2, w_router, wg, wu, wd, *, tm):
    n, _ = x1.shape
    n_e = wg.shape[0]
    route, cnt = router(h2, w_router, tm=tm)
    counts = cnt[0].astype(jnp.int32)
    padded = (counts + MOE_TILE - 1) // MOE_TILE * MOE_TILE
    ends = jnp.cumsum(padded)
    starts = ends - padded
    experts = route[:, ROUTE_E:ROUTE_E + TOP_K].astype(jnp.int32)
    ranks = route[:, ROUTE_RANK:ROUTE_RANK + TOP_K].astype(jnp.int32)
    slots = (starts[experts] + ranks).T
    n_tiles = (TOP_K * n + MOE_TILE - 1) // MOE_TILE + n_e
    tile_start = jnp.arange(n_tiles, dtype=jnp.int32) * MOE_TILE
    tile_expert = jnp.minimum(jnp.sum(tile_start[:, None] >= ends[None, :], axis=1), n_e - 1)
    n_valid = (ends[-1] // MOE_TILE).reshape(1)
    rows = min(MOE_ROW_TILE, n)
    xs = moe_dispatch(slots, h2, n_tiles * MOE_TILE, tm=rows)
    ys = moe_experts(tile_expert.astype(jnp.int32), n_valid.astype(jnp.int32), xs, wg, wu, wd)
    return moe_combine(slots, x1, route, ys, tm=rows)


def combine_matrix(route, n_e):
    eid = jnp.arange(n_e, dtype=F32)[None, :]
    comb = jnp.zeros((route.shape[0], n_e), F32)
    for k in range(TOP_K):
        comb = comb + jnp.where(route[:, ROUTE_E + k:ROUTE_E + k + 1] == eid,
                                route[:, ROUTE_W + k:ROUTE_W + k + 1], 0.0)
    return comb


def _moe_kernel(x1_ref, h_ref, c_ref, wg_ref, wu_ref, wd_ref, o_ref, acc_ref):
    e = pl.program_id(1)
    f = pl.program_id(2)

    @pl.when((e == 0) & (f == 0))
    def _():
        acc_ref[...] = x1_ref[...]

    n_e = c_ref.shape[1]
    eid = lax.broadcasted_iota(jnp.int32, c_ref.shape, 1)
    ce = jnp.sum(jnp.where(eid == e, c_ref[...], 0.0), axis=1, keepdims=True)
    h = h_ref[...].astype(BF16)
    a = _silu(_dot(h, wg_ref[0])) * _dot(h, wu_ref[0])
    acc_ref[...] += ce * _dot(a.astype(BF16), wd_ref[0])

    @pl.when((e == pl.num_programs(1) - 1) & (f == pl.num_programs(2) - 1))
    def _():
        o_ref[...] = acc_ref[...]


def moe_dense(x1, h2, comb, wg, wu, wd, *, tm):
    n, d = x1.shape
    n_e, _, d_ff = wg.shape
    tf = _ff_tile(d_ff)
    row = lambda i, e, f: (i, 0)
    return pl.pallas_call(
        _moe_kernel,
        grid=(n // tm, n_e, d_ff // tf),
        in_specs=[pl.BlockSpec((tm, d), row), pl.BlockSpec((tm, d), row),
                  pl.BlockSpec((tm, n_e), row),
                  pl.BlockSpec((1, d, tf), lambda i, e, f: (e, 0, f)),
                  pl.BlockSpec((1, d, tf), lambda i, e, f: (e, 0, f)),
                  pl.BlockSpec((1, tf, d), lambda i, e, f: (e, f, 0))],
        out_specs=pl.BlockSpec((tm, d), row),
        out_shape=jax.ShapeDtypeStruct((n, d), F32),
        scratch_shapes=[pltpu.VMEM((tm, d), F32)],
        compiler_params=_cparams(("parallel", "arbitrary", "arbitrary")),
        name="moe_dense",
    )(x1, h2, comb, wg, wu, wd)


def _rope_tables(pos):
    half = HEAD_DIM // 2
    inv = ROPE_BASE ** (-jnp.arange(half, dtype=F32) / half)
    ang = pos.astype(F32)[:, None] * inv[None, :]
    cos, sin = jnp.cos(ang), jnp.sin(ang)
    reps = LANES // HEAD_DIM
    cos_t = jnp.tile(jnp.concatenate([cos, cos], axis=1), (1, reps))
    sin_t = jnp.tile(jnp.concatenate([-sin, sin], axis=1), (1, reps))
    return cos_t, sin_t


def _token_tile(n):
    for tm in (512, 256, 128, 64, 32, 16, 8):
        if n % tm == 0:
            return tm
    return n


def kernel(x_prompt, x_sample, cache_k, cache_v, state_ret, page_table, g_mix, w_in, ret_gn,
           q_norm_g, k_norm_g, w_out, g_ffn, w_ffn_gate, w_ffn_up, w_ffn_down, w_router,
           w_exp_gate, w_exp_up, w_exp_down):
    b, t, d = x_prompt.shape
    db, ds, _ = x_sample.shape
    depth = w_in.shape[0]
    n_phys = cache_k.shape[1]
    n_p, n_s = b * t, db * ds
    tm_p, tm_s = _token_tile(n_p), _token_tile(n_s)
    past_len = page_table.shape[1] * PAGE_SIZE

    cos_p, sin_p = _rope_tables(jnp.arange(t, dtype=jnp.int32))
    cos_s, sin_s = _rope_tables(jnp.broadcast_to(
        past_len + jnp.arange(ds, dtype=jnp.int32)[None, :], (db, ds)).reshape(n_s))
    head_mean = (jnp.kron(jnp.eye(H_MOBA, dtype=F32), jnp.ones((HEAD_DIM, HEAD_DIM), F32))
                 / HEAD_DIM).astype(BF16)
    cache_kt = cache_k.transpose(0, 1, 3, 4, 2)
    cache_vt = cache_v.transpose(0, 1, 3, 4, 2)

    xp = x_prompt.reshape(n_p, d)
    xs = x_sample.reshape(n_s, d)
    sp_l, kp_l, vp_l, ss_l, ks_l, vs_l = [], [], [], [], [], []
    for l in range(depth):
        w_in_b = w_in[l].astype(BF16)
        w_out_b = w_out[l].astype(BF16)
        g1 = g_mix[l].reshape(1, d)
        g2 = g_ffn[l].reshape(1, d)
        qg = jnp.tile(q_norm_g[l], H_MOBA).reshape(1, W_MOBA)
        kg = jnp.tile(k_norm_g[l], H_MOBA).reshape(1, W_MOBA)
        gn = ret_gn[l].reshape(1, W_RET)

        (qr, kr, vr, gr, qm, km, vm, kmb, kmean) = proj_in(
            xp, g1, w_in_b, cos_p, sin_p, qg, kg, head_mean, tm=tm_p, with_kmean=True)
        seq = lambda a: a.reshape(b, t, a.shape[-1])
        o_r, s_fin = retention_prompt(seq(qr), seq(kr), seq(vr), seq(gr), gn)
        o_m = moba_prompt(seq(qm), seq(kmb), seq(vm), kmean.reshape(b, t // MOBA_BLOCK, W_MOBA))
        x1p, h2p = out_proj(o_r.reshape(n_p, W_RET), o_m.reshape(n_p, W_MOBA), xp, w_out_b, g2,
                            tm=tm_p)
        sp_l.append(s_fin)
        kp_l.append(km.reshape(b, t, H_MOBA, HEAD_DIM))
        vp_l.append(vm.reshape(b, t, H_MOBA, HEAD_DIM))

        (qr, kr, vr, gr, qm, km, vm) = proj_in(
            xs, g1, w_in_b, cos_s, sin_s, qg, kg, head_mean, tm=tm_s, with_kmean=False)
        f32 = lambda a: a.astype(F32)
        o_r, s_new = retention_sample(f32(qr), f32(kr), f32(vr), f32(gr), gn, state_ret[l])
        heads = lambda a: a.reshape(n_s, H_MOBA, HEAD_DIM)
        o_m = moba_sample(heads(qm), heads(km), heads(vm), cache_kt, cache_vt, page_table,
                          l).reshape(n_s, W_MOBA)
        x1s, h2s = out_proj(o_r.astype(BF16), o_m.astype(BF16), xs, w_out_b, g2, tm=tm_s)
        ss_l.append(s_new)
        ks_l.append(km.reshape(db, ds, H_MOBA, HEAD_DIM))
        vs_l.append(vm.reshape(db, ds, H_MOBA, HEAD_DIM))

        i = l // 2
        if l % 2 == 0:
            wg, wu, wd = (w_ffn_gate[i].astype(BF16), w_ffn_up[i].astype(BF16),
                          w_ffn_down[i].astype(BF16))
            xp = ffn_dense(x1p, h2p, wg, wu, wd, tm=tm_p)
            xs = ffn_dense(x1s, h2s, wg, wu, wd, tm=tm_s)
        else:
            wg, wu, wd = (w_exp_gate[i].astype(BF16), w_exp_up[i].astype(BF16),
                          w_exp_down[i].astype(BF16))
            xp = moe_routed(x1p, h2p, w_router[i], wg, wu, wd, tm=tm_p)
            route_s, _ = router(h2s, w_router[i], tm=tm_s)
            xs = moe_dense(x1s, h2s, combine_matrix(route_s, wg.shape[0]), wg, wu, wd, tm=tm_s)

    return (xp.reshape(b, t, d), xs.reshape(db, ds, d),
            jnp.stack(sp_l), jnp.stack(kp_l), jnp.stack(vp_l),
            jnp.stack(ss_l), jnp.stack(ks_l), jnp.stack(vs_l))
```

```python
import functools
import math

import jax
import jax.numpy as jnp
import numpy as np
from jax import lax
from jax.experimental import pallas as pl
from jax.experimental.pallas import tpu as pltpu

F32 = jnp.float32
BF16 = jnp.bfloat16

HEAD_DIM = 64
H_RET = 8
H_MOBA = 8
W_RET = H_RET * HEAD_DIM
W_MOBA = H_MOBA * HEAD_DIM
PAGE_SIZE = 128
ROPE_BASE = 10000.0
MOBA_BLOCK = 256
MOBA_TOPK = 3
TOP_K = 2
EPS = 1e-6
RET_CHUNK = 256
MOBA_KEY_CHUNK = 128
MOBA_UNROLL = 4
FF_TILE_MAX = 1408
MOE_TILE = 512
MOE_ROW_TILE = 256
LANES = 128
NEG = -1e30
VMEM_LIMIT = 56 * 1024 * 1024

HIGHEST = lax.Precision.HIGHEST


def _cparams(sem):
    return pltpu.CompilerParams(dimension_semantics=sem, vmem_limit_bytes=VMEM_LIMIT)


def _silu(x):
    return x / (1.0 + jnp.exp(-x))


def _dot(a, b):
    return jnp.dot(a, b, preferred_element_type=F32)


def _dot_nt(a, b, precision=None):
    return lax.dot_general(a, b, (((1,), (1,)), ((), ())), precision=precision,
                           preferred_element_type=F32)


def _dot_tn(a, b):
    return lax.dot_general(a, b, (((0,), (0,)), ((), ())), preferred_element_type=F32)


def _split3(x):
    hi = x.astype(BF16)
    r = x - hi.astype(F32)
    mid = r.astype(BF16)
    lo = (r - mid.astype(F32)).astype(BF16)
    return hi, mid, lo


def _proj_in_kernel(x_ref, g_ref, w_ref, cos_ref, sin_ref, qg_ref, kg_ref, hm_ref,
                    qr_ref, kr_ref, vr_ref, gr_ref, qm_ref, km_ref, vm_ref,
                    *prompt_refs):
    x = x_ref[...]
    tm = x.shape[0]
    ms = jnp.mean(x * x, axis=-1, keepdims=True)
    h = (x * lax.rsqrt(ms + EPS) * g_ref[...]).astype(BF16)

    def proj(c0, width):
        return _dot(h, w_ref[:, c0:c0 + width])

    cos = cos_ref[...]
    sin = sin_ref[...]
    lane = lax.broadcasted_iota(jnp.int32, (tm, LANES), 1)
    first_half = (lane % HEAD_DIM) < (HEAD_DIM // 2)

    def rope(z):
        outs = []
        for c in range(z.shape[1] // LANES):
            zc = z[:, c * LANES:(c + 1) * LANES]
            rot = jnp.where(first_half,
                            pltpu.roll(zc, LANES - HEAD_DIM // 2, 1),
                            pltpu.roll(zc, HEAD_DIM // 2, 1))
            outs.append(zc * cos + rot * sin)
        return jnp.concatenate(outs, axis=1)

    hm = hm_ref[...]

    def head_norm(z, g):
        sq = z * z
        hi = sq.astype(BF16)
        lo = (sq - hi.astype(F32)).astype(BF16)
        msq = _dot(hi, hm) + _dot(lo, hm)
        return z * lax.rsqrt(msq + EPS) * g

    qr_ref[...] = rope(proj(0, W_RET)).astype(BF16)
    kr_ref[...] = (rope(proj(W_RET, W_RET)) * (HEAD_DIM ** -0.5)).astype(BF16)
    vr_ref[...] = proj(2 * W_RET, W_RET).astype(BF16)
    gr_ref[...] = proj(3 * W_RET, W_RET).astype(BF16)
    base = 4 * W_RET
    qm_ref[...] = head_norm(proj(base, W_MOBA), qg_ref[...])
    km = head_norm(proj(base + W_MOBA, W_MOBA), kg_ref[...])
    km_ref[...] = km
    vm_ref[...] = proj(base + 2 * W_MOBA, W_MOBA)
    if prompt_refs:
        kmb_ref, kmean_ref = prompt_refs
        kmb_ref[...] = km.astype(BF16)
        for r in range(tm // MOBA_BLOCK):
            kmean_ref[0, r:r + 1, :] = jnp.mean(
                km[r * MOBA_BLOCK:(r + 1) * MOBA_BLOCK], axis=0, keepdims=True)


def proj_in(x, g, w_bf, cos, sin, qg, kg, hm, *, tm, with_kmean):
    n, d = x.shape
    n_pos_tiles = cos.shape[0] // tm
    row = lambda i: (i, 0)
    fixed = lambda i: (0, 0)
    wide = lambda dt: jax.ShapeDtypeStruct((n, W_RET), dt)
    out_shape = [wide(BF16), wide(BF16), wide(BF16), wide(BF16),
                 wide(F32), wide(F32), wide(F32)]
    out_specs = [pl.BlockSpec((tm, W_RET), row)] * 7
    if with_kmean:
        nb = tm // MOBA_BLOCK
        out_shape += [wide(BF16), jax.ShapeDtypeStruct((n // tm, nb, W_MOBA), F32)]
        out_specs += [pl.BlockSpec((tm, W_RET), row),
                      pl.BlockSpec((1, nb, W_MOBA), lambda i: (i, 0, 0))]
    return pl.pallas_call(
        _proj_in_kernel,
        grid=(n // tm,),
        in_specs=[
            pl.BlockSpec((tm, d), row),
            pl.BlockSpec((1, d), fixed),
            pl.BlockSpec(w_bf.shape, fixed),
            pl.BlockSpec((tm, LANES), lambda i: (i % n_pos_tiles, 0)),
            pl.BlockSpec((tm, LANES), lambda i: (i % n_pos_tiles, 0)),
            pl.BlockSpec((1, W_MOBA), fixed),
            pl.BlockSpec((1, W_MOBA), fixed),
            pl.BlockSpec((W_MOBA, W_MOBA), fixed),
        ],
        out_specs=out_specs,
        out_shape=out_shape,
        compiler_params=_cparams(("parallel",)),
        name="proj_in",
    )(x, g, w_bf, cos, sin, qg, kg, hm)


def _ret_prompt_kernel(q_ref, k_ref, v_ref, g_ref, gn_ref, dm_ref, cross_ref, kdec_ref,
                       gc_ref, o_ref, sfin_ref, s_scr):
    c = pl.program_id(1)

    @pl.when(c == 0)
    def _():
        s_scr[...] = jnp.zeros_like(s_scr)

    heads = range(H_RET)
    sls = [slice(h * HEAD_DIM, (h + 1) * HEAD_DIM) for h in heads]
    qs = [q_ref[0, :, sl] for sl in sls]
    ks = [k_ref[0, :, sl] for sl in sls]
    vs = [v_ref[0, :, sl] for sl in sls]
    s0s = [s_scr[h] for h in heads]
    scores = [_dot_nt(qs[h], ks[h]) for h in heads]
    cross = [_dot(qs[h], s0s[h].astype(BF16)) for h in heads]
    kds = [(ks[h].astype(F32) * kdec_ref[h]).astype(BF16) for h in heads]
    kvs = [_dot_tn(kds[h], vs[h]) for h in heads]
    inner = [_dot((scores[h] * dm_ref[h]).astype(BF16), vs[h]) for h in heads]
    outs = []
    for h in heads:
        s_scr[h] = gc_ref[h] * s0s[h] + kvs[h]
        o = inner[h] + cross[h] * cross_ref[h]
        mu = jnp.mean(o, axis=-1, keepdims=True)
        oc = o - mu
        var = jnp.mean(oc * oc, axis=-1, keepdims=True)
        on = oc * lax.rsqrt(var + EPS) * gn_ref[:, sls[h]]
        outs.append(on * _silu(g_ref[0, :, sls[h]].astype(F32)))
    o_ref[0] = jnp.concatenate(outs, axis=1).astype(BF16)

    @pl.when(c == pl.num_programs(1) - 1)
    def _():
        sfin_ref[0] = s_scr[...]


def _ret_log_decay():
    return np.log1p(-np.exp2(-5.0 - np.arange(H_RET, dtype=np.float64)))


def retention_prompt(qr, kr, vr, gr, gn):
    b, t, w = qr.shape
    c = min(RET_CHUNK, t)
    ld = _ret_log_decay()
    i = np.arange(c, dtype=np.float64)
    diff = i[:, None] - i[None, :]
    f32c = lambda a: jnp.asarray(np.ascontiguousarray(a, dtype=np.float32))
    dmask = f32c(np.where(diff >= 0, np.exp(ld[:, None, None] * np.maximum(diff, 0.0)), 0.0))
    bc = lambda a: f32c(np.broadcast_to(a[:, :, None], (H_RET, c, HEAD_DIM)))
    cross = bc(np.exp(ld[:, None] * (i + 1.0)[None, :]))
    kdec = bc(np.exp(ld[:, None] * (c - 1.0 - i)[None, :]))
    gc = f32c(np.broadcast_to(np.exp(ld * c)[:, None, None], (H_RET, HEAD_DIM, HEAD_DIM)))
    tile = pl.BlockSpec((1, c, w), lambda bi, ci: (bi, ci, 0))
    const3 = lambda shape: pl.BlockSpec(shape, lambda bi, ci: (0, 0, 0))
    return pl.pallas_call(
        _ret_prompt_kernel,
        grid=(b, t // c),
        in_specs=[tile, tile, tile, tile,
                  pl.BlockSpec((1, w), lambda bi, ci: (0, 0)),
                  const3((H_RET, c, c)), const3((H_RET, c, HEAD_DIM)),
                  const3((H_RET, c, HEAD_DIM)), const3((H_RET, HEAD_DIM, HEAD_DIM))],
        out_specs=[tile,
                   pl.BlockSpec((1, H_RET, HEAD_DIM, HEAD_DIM), lambda bi, ci: (bi, 0, 0, 0))],
        out_shape=[jax.ShapeDtypeStruct((b, t, w), BF16),
                   jax.ShapeDtypeStruct((b, H_RET, HEAD_DIM, HEAD_DIM), F32)],
        scratch_shapes=[pltpu.VMEM((H_RET, HEAD_DIM, HEAD_DIM), F32)],
        compiler_params=_cparams(("parallel", "arbitrary")),
        name="retention_prompt",
    )(qr, kr, vr, gr, gn, dmask, cross, kdec, gc)


def _moba_prompt_kernel(q_ref, k_ref, v_ref, km_ref, o_ref, vt_scr, sel_scr, m_scr, l_scr,
                        acc_scr):
    qi = pl.program_id(2)
    nb = km_ref.shape[1]
    bs = MOBA_BLOCK
    kc = MOBA_KEY_CHUNK
    n_chunks = bs // kc
    n_heads = LANES // HEAD_DIM

    @pl.when(qi == 0)
    def _():
        def transpose_block(j, c):
            start = pl.multiple_of(j * bs, bs)
            vt = v_ref[0, pl.ds(start, bs), :].T
            for hh in range(n_heads):
                vt_scr[j, hh] = vt[hh * HEAD_DIM:(hh + 1) * HEAD_DIM].astype(BF16)
            return c
        lax.fori_loop(0, nb, transpose_block, 0)

    qt = (q_ref[0] * (HEAD_DIM ** -0.5 * math.log2(math.e))).T
    drow = lax.broadcasted_iota(jnp.int32, qt.shape, 0)
    blk = lax.broadcasted_iota(jnp.int32, (nb, bs), 0)
    kmean = km_ref[0]
    qts = []
    for hh in range(n_heads):
        qh = jnp.where(drow // HEAD_DIM == hh, qt, 0.0)
        qts.append(qh.astype(BF16))
        gate = jnp.dot(kmean, qh, precision=HIGHEST, preferred_element_type=F32)
        g = jnp.where(blk < qi, gate, -jnp.inf)
        sel = jnp.zeros((nb, bs), F32)
        for _ in range(min(MOBA_TOPK, nb)):
            mx = jnp.max(g, axis=0, keepdims=True)
            idx = jnp.min(jnp.where(g == mx, blk, nb), axis=0, keepdims=True)
            pick = blk == idx
            sel = jnp.where(pick, 1.0, sel)
            g = jnp.where(pick, -jnp.inf, g)
        sel_scr[hh] = jnp.where(blk < qi, sel, 0.0)

    m_scr[...] = jnp.full(m_scr.shape, NEG, F32)
    l_scr[...] = jnp.zeros(l_scr.shape, F32)
    acc_scr[...] = jnp.zeros(acc_scr.shape, F32)

    def merge(hh, stats, sel_row=None):
        m = m_scr[hh]
        mb, lb, pv = stats
        m_new = jnp.maximum(m, mb)
        wb = jnp.exp2(mb - m_new)
        if sel_row is not None:
            m_new = jnp.where(sel_row, m_new, m)
            wb = jnp.where(sel_row, wb, 0.0)
        wa = jnp.exp2(m - m_new)
        m_scr[hh] = m_new
        l_scr[hh] = wa * l_scr[hh] + wb * lb
        acc_scr[hh] = wa * acc_scr[hh] + wb * pv

    krow = lax.broadcasted_iota(jnp.int32, (kc, bs), 0)
    qcol = lax.broadcasted_iota(jnp.int32, (kc, bs), 1)

    def blocks(j0, n_blocks, own):
        units = [(u, c, hh) for u in range(n_blocks) for c in range(n_chunks)
                 for hh in range(n_heads)]
        scores = []
        for u, c, hh in units:
            start = pl.multiple_of((j0 + u) * bs + c * kc, kc)
            s = _dot(k_ref[0, pl.ds(start, kc), :], qts[hh])
            if own:
                s = jnp.where(krow + c * kc <= qcol, s, NEG)
            scores.append(s)
        parts = []
        for s in scores:
            mb = jnp.max(s, axis=0, keepdims=True)
            p = jnp.exp2(s - mb)
            parts.append((mb, jnp.sum(p, axis=0, keepdims=True), p.astype(BF16)))
        pvs = [_dot(vt_scr[j0 + u, hh, :, c * kc:(c + 1) * kc], parts[i][2])
               for i, (u, c, hh) in enumerate(units)]
        for i, (u, c, hh) in enumerate(units):
            sel_row = None if own else sel_scr[hh, pl.ds(j0 + u, 1), :] > 0.0
            merge(hh, (parts[i][0], parts[i][1], pvs[i]), sel_row)

    def group_body(g, c_):
        blocks(g * MOBA_UNROLL, MOBA_UNROLL, False)
        return c_

    def single_body(j, c_):
        blocks(j, 1, False)
        return c_

    n_groups = qi // MOBA_UNROLL
    lax.fori_loop(0, n_groups, group_body, 0)
    lax.fori_loop(n_groups * MOBA_UNROLL, qi, single_body, 0)
    blocks(qi, 1, True)
    outs = [acc_scr[hh] / l_scr[hh] for hh in range(n_heads)]
    o_ref[0] = jnp.concatenate(outs, axis=0).T.astype(BF16)


def moba_prompt(qm, kmb, vmb, kmean):
    b, t, w = qm.shape
    nb = t // MOBA_BLOCK
    n_heads = LANES // HEAD_DIM
    return pl.pallas_call(
        _moba_prompt_kernel,
        grid=(b, w // LANES, nb),
        in_specs=[
            pl.BlockSpec((1, MOBA_BLOCK, LANES), lambda bi, hp, qi: (bi, qi, hp)),
            pl.BlockSpec((1, t, LANES), lambda bi, hp, qi: (bi, 0, hp)),
            pl.BlockSpec((1, t, LANES), lambda bi, hp, qi: (bi, 0, hp)),
            pl.BlockSpec((1, nb, LANES), lambda bi, hp, qi: (bi, 0, hp)),
        ],
        out_specs=pl.BlockSpec((1, MOBA_BLOCK, LANES), lambda bi, hp, qi: (bi, qi, hp)),
        out_shape=jax.ShapeDtypeStruct((b, t, w), BF16),
        scratch_shapes=[pltpu.VMEM((nb, n_heads, HEAD_DIM, MOBA_BLOCK), BF16),
                        pltpu.VMEM((n_heads, nb, MOBA_BLOCK), F32),
                        pltpu.VMEM((n_heads, 1, MOBA_BLOCK), F32),
                        pltpu.VMEM((n_heads, 1, MOBA_BLOCK), F32),
                        pltpu.VMEM((n_heads, HEAD_DIM, MOBA_BLOCK), F32)],
        compiler_params=_cparams(("parallel", "parallel", "arbitrary")),
        name="moba_prompt",
    )(qm, kmb, vmb, kmean)


def _ret_sample_kernel(q_ref, k_ref, v_ref, g_ref, gam_ref, gn_ref, s0_ref, e1_ref, e2_ref,
                       o_ref, s_ref):
    e1 = e1_ref[...]
    e2 = e2_ref[...]

    def expand(x, e):
        hi, mid, lo = _split3(x)
        return _dot(hi, e) + _dot(mid, e) + _dot(lo, e)

    kexp = expand(k_ref[...], e1)
    vexp = expand(v_ref[...], e2)
    qexp = expand(q_ref[...], e1)
    s_new = gam_ref[:, :1] * s0_ref[...] + kexp * vexp
    s_ref[...] = s_new
    hi, mid, lo = _split3(qexp * s_new)
    o = _dot_nt(hi, e2) + _dot_nt(mid, e2) + _dot_nt(lo, e2)
    mu = jnp.mean(o, axis=-1, keepdims=True)
    oc = o - mu
    var = jnp.mean(oc * oc, axis=-1, keepdims=True)
    o_ref[...] = oc * lax.rsqrt(var + EPS) * gn_ref[...] * _silu(g_ref[...])


def retention_sample(q, k, v, gr, gn, s0):
    n = q.shape[0]
    r = n * H_RET
    dd = HEAD_DIM * HEAD_DIM
    rows = lambda a: a.reshape(r, HEAD_DIM)
    gam = jnp.asarray(np.tile(np.broadcast_to(np.exp(_ret_log_decay())[:, None],
                                              (H_RET, HEAD_DIM)), (n, 1)).astype(np.float32))
    gnr = jnp.tile(gn.reshape(H_RET, HEAD_DIM), (n, 1))
    lane = np.arange(dd)
    e1 = jnp.asarray(lane[None, :] // HEAD_DIM == np.arange(HEAD_DIM)[:, None], BF16)
    e2 = jnp.asarray(lane[None, :] % HEAD_DIM == np.arange(HEAD_DIM)[:, None], BF16)
    tr = min(128, r)
    small = pl.BlockSpec((tr, HEAD_DIM), lambda i: (i, 0))
    big = pl.BlockSpec((tr, dd), lambda i: (i, 0))
    emat = pl.BlockSpec((HEAD_DIM, dd), lambda i: (0, 0))
    o, s_new = pl.pallas_call(
        _ret_sample_kernel,
        grid=(r // tr,),
        in_specs=[small, small, small, small, small, small, big, emat, emat],
        out_specs=[small, big],
        out_shape=[jax.ShapeDtypeStruct((r, HEAD_DIM), F32),
                   jax.ShapeDtypeStruct((r, dd), F32)],
        compiler_params=_cparams(("parallel",)),
        name="retention_sample",
    )(rows(q), rows(k), rows(v), rows(gr), gam, gnr, s0.reshape(r, dd), e1, e2)
    return o.reshape(n, W_RET), s_new.reshape(n, H_RET, HEAD_DIM, HEAD_DIM)


def _moba_sample_kernel(pt_ref, q_ref, qt_ref, kn_ref, vn_ref, *refs, n_pages):
    del pt_ref
    k_refs = refs[:n_pages]
    v_refs = refs[n_pages:2 * n_pages]
    o_ref = refs[2 * n_pages]
    ppb = MOBA_BLOCK // PAGE_SIZE
    nbp = n_pages // ppb
    qscale = HEAD_DIM ** -0.5 * math.log2(math.e)
    qt = qt_ref[0] * qscale
    qcols = [jnp.broadcast_to(qt[:, h:h + 1], (HEAD_DIM, PAGE_SIZE)) for h in range(H_MOBA)]
    s_pages = []
    for p in range(n_pages):
        rows = [jnp.sum(k_refs[p][h] * qcols[h], axis=0, keepdims=True) for h in range(H_MOBA)]
        s_pages.append(jnp.concatenate(rows, axis=0))
    gates, maxes = [], []
    for j in range(nbp):
        tot = s_pages[j * ppb]
        top = s_pages[j * ppb]
        for p in range(1, ppb):
            tot = tot + s_pages[j * ppb + p]
            top = jnp.maximum(top, s_pages[j * ppb + p])
        gates.append(jnp.sum(tot, axis=1, keepdims=True))
        maxes.append(jnp.max(top, axis=1, keepdims=True))
    k_sel = min(MOBA_TOPK, nbp + 1)
    sels = []
    for j in range(nbp):
        rank = jnp.zeros(gates[j].shape, F32)
        for i in range(nbp):
            if i == j:
                continue
            ahead = (gates[i] > gates[j]) | ((gates[i] == gates[j]) & (i < j))
            rank = rank + jnp.where(ahead, 1.0, 0.0)
        sels.append(rank < k_sel)
    s_own = jnp.sum(q_ref[0] * kn_ref[0], axis=1, keepdims=True) * qscale
    m = s_own
    for j in range(nbp):
        m = jnp.maximum(m, jnp.where(sels[j], maxes[j], NEG))
    p_own = jnp.exp2(s_own - m)
    l = p_own
    accs = [jnp.zeros((HEAD_DIM, PAGE_SIZE), F32) for _ in range(H_MOBA)]
    for p in range(n_pages):
        shift = jnp.where(sels[p // ppb], m, -NEG)
        pp = jnp.exp2(s_pages[p] - shift)
        l = l + jnp.sum(pp, axis=1, keepdims=True)
        for h in range(H_MOBA):
            accs[h] = accs[h] + pp[h:h + 1, :] * v_refs[p][h]
    o_past = jnp.concatenate([jnp.sum(a.T, axis=0, keepdims=True) for a in accs], axis=0)
    o_ref[0] = (o_past + p_own * vn_ref[0]) / l


def moba_sample(q, k_new, v_new, cache_k, cache_v, page_table, layer):
    n = q.shape[0]
    n_pages = page_table.shape[1]
    vec = pl.BlockSpec((1, H_MOBA, HEAD_DIM), lambda s, pt: (s, 0, 0))
    vec_t = pl.BlockSpec((1, HEAD_DIM, H_MOBA), lambda s, pt: (s, 0, 0))

    def page_spec(p):
        return pl.BlockSpec((None, None, H_MOBA, HEAD_DIM, PAGE_SIZE),
                            lambda s, pt: (layer, pt[s, p], 0, 0, 0))

    pages = [page_spec(p) for p in range(n_pages)]
    return pl.pallas_call(
        functools.partial(_moba_sample_kernel, n_pages=n_pages),
        grid_spec=pltpu.PrefetchScalarGridSpec(
            num_scalar_prefetch=1,
            grid=(n,),
            in_specs=[vec, vec_t, vec, vec] + pages + pages,
            out_specs=vec,
        ),
        out_shape=jax.ShapeDtypeStruct((n, H_MOBA, HEAD_DIM), F32),
        compiler_params=_cparams(("arbitrary",)),
        name="moba_sample",
    )(page_table, q, q.transpose(0, 2, 1), k_new, v_new,
      *([cache_k] * n_pages), *([cache_v] * n_pages))


def _out_proj_kernel(or_ref, om_ref, x_ref, w_ref, g_ref, x1_ref, h2_ref):
    x1 = (x_ref[...] + _dot(or_ref[...], w_ref[:W_RET, :])
          + _dot(om_ref[...], w_ref[W_RET:, :]))
    x1_ref[...] = x1
    ms = jnp.mean(x1 * x1, axis=-1, keepdims=True)
    h2_ref[...] = x1 * lax.rsqrt(ms + EPS) * g_ref[...]


def out_proj(o_r, o_m, x, w_bf, g, *, tm):
    n, d = x.shape
    row = lambda i: (i, 0)
    fixed = lambda i: (0, 0)
    return pl.pallas_call(
        _out_proj_kernel,
        grid=(n // tm,),
        in_specs=[pl.BlockSpec((tm, W_RET), row), pl.BlockSpec((tm, W_MOBA), row),
                  pl.BlockSpec((tm, d), row), pl.BlockSpec(w_bf.shape, fixed),
                  pl.BlockSpec((1, d), fixed)],
        out_specs=[pl.BlockSpec((tm, d), row), pl.BlockSpec((tm, d), row)],
        out_shape=[jax.ShapeDtypeStruct((n, d), F32), jax.ShapeDtypeStruct((n, d), F32)],
        compiler_params=_cparams(("parallel",)),
        name="out_proj",
    )(o_r, o_m, x, w_bf, g)


def _ffn_kernel(x1_ref, h_ref, wg_ref, wu_ref, wd_ref, o_ref, acc_ref):
    f = pl.program_id(1)

    @pl.when(f == 0)
    def _():
        acc_ref[...] = x1_ref[...]

    h = h_ref[...].astype(BF16)
    a = _silu(_dot(h, wg_ref[...])) * _dot(h, wu_ref[...])
    acc_ref[...] += _dot(a.astype(BF16), wd_ref[...])

    @pl.when(f == pl.num_programs(1) - 1)
    def _():
        o_ref[...] = acc_ref[...]


def _ff_tile(d_ff):
    for tf in range(FF_TILE_MAX, 0, -LANES):
        if d_ff % tf == 0:
            return tf
    return d_ff


def ffn_dense(x1, h2, wg, wu, wd, *, tm):
    n, d = x1.shape
    d_ff = wg.shape[1]
    tf = _ff_tile(d_ff)
    row = lambda i, f: (i, 0)
    return pl.pallas_call(
        _ffn_kernel,
        grid=(n // tm, d_ff // tf),
        in_specs=[pl.BlockSpec((tm, d), row), pl.BlockSpec((tm, d), row),
                  pl.BlockSpec((d, tf), lambda i, f: (0, f)),
                  pl.BlockSpec((d, tf), lambda i, f: (0, f)),
                  pl.BlockSpec((tf, d), lambda i, f: (f, 0))],
        out_specs=pl.BlockSpec((tm, d), row),
        out_shape=jax.ShapeDtypeStruct((n, d), F32),
        scratch_shapes=[pltpu.VMEM((tm, d), F32)],
        compiler_params=_cparams(("parallel", "arbitrary")),
        name="ffn_dense",
    )(x1, h2, wg, wu, wd)


ROUTE_E, ROUTE_RANK, ROUTE_W = 0, 2, 4


def _router_kernel(h_ref, w_ref, route_ref, cnt_ref, carry_scr):
    @pl.when(pl.program_id(0) == 0)
    def _():
        carry_scr[...] = jnp.zeros_like(carry_scr)

    logits = jnp.dot(h_ref[...], w_ref[...], precision=HIGHEST, preferred_element_type=F32)
    tm, n_e = logits.shape
    eid = lax.broadcasted_iota(jnp.int32, logits.shape, 1)
    v1 = jnp.max(logits, axis=1, keepdims=True)
    i1 = jnp.min(jnp.where(logits == v1, eid, n_e), axis=1, keepdims=True)
    rest = jnp.where(eid == i1, -jnp.inf, logits)
    v2 = jnp.max(rest, axis=1, keepdims=True)
    i2 = jnp.min(jnp.where(rest == v2, eid, n_e), axis=1, keepdims=True)
    e2 = jnp.exp(v2 - v1)
    w1 = 1.0 / (1.0 + e2)
    w2 = e2 / (1.0 + e2)
    onehot = jnp.where((eid == i1) | (eid == i2), 1.0, 0.0)
    r = lax.broadcasted_iota(jnp.int32, (tm, tm), 0)
    c = lax.broadcasted_iota(jnp.int32, (tm, tm), 1)
    earlier = jnp.where(c < r, 1.0, 0.0).astype(BF16)
    ranks = _dot(earlier, onehot.astype(BF16)) + carry_scr[...]
    r1 = jnp.sum(jnp.where(eid == i1, ranks, 0.0), axis=1, keepdims=True)
    r2 = jnp.sum(jnp.where(eid == i2, ranks, 0.0), axis=1, keepdims=True)
    total = carry_scr[...] + jnp.sum(onehot, axis=0, keepdims=True)
    carry_scr[...] = total
    cnt_ref[...] = total
    route = jnp.zeros(logits.shape, F32)
    for k, col in enumerate((i1.astype(F32), i2.astype(F32), r1, r2, w1, w2)):
        route = jnp.where(eid == k, col, route)
    route_ref[...] = route


def router(h2, w_router, *, tm):
    n, d = h2.shape
    n_e = w_router.shape[1]
    return pl.pallas_call(
        _router_kernel,
        grid=(n // tm,),
        in_specs=[pl.BlockSpec((tm, d), lambda i: (i, 0)),
                  pl.BlockSpec((d, n_e), lambda i: (0, 0))],
        out_specs=[pl.BlockSpec((tm, n_e), lambda i: (i, 0)),
                   pl.BlockSpec((1, n_e), lambda i: (0, 0))],
        out_shape=[jax.ShapeDtypeStruct((n, n_e), F32), jax.ShapeDtypeStruct((1, n_e), F32)],
        scratch_shapes=[pltpu.VMEM((1, n_e), F32)],
        compiler_params=_cparams(("arbitrary",)),
        name="router",
    )(h2, w_router)


def _dispatch_kernel(slots_ref, h_ref, xs_in_ref, xs_ref, sem):
    del xs_in_ref
    tm = h_ref.shape[0]
    base = pl.program_id(0) * tm

    def start(r, c):
        for k in range(TOP_K):
            slot = slots_ref[k, base + r]
            pltpu.make_async_copy(h_ref.at[pl.ds(r, 1)], xs_ref.at[pl.ds(slot, 1)],
                                  sem).start(priority=k)
        return c

    lax.fori_loop(0, tm, start, 0)

    def wait(r, c):
        for k in range(TOP_K):
            pltpu.make_async_copy(h_ref.at[pl.ds(0, 1)], xs_ref.at[pl.ds(0, 1)], sem).wait()
        return c

    lax.fori_loop(0, tm, wait, 0)


def moe_dispatch(slots, h2, n_slots, *, tm):
    n, d = h2.shape
    xs0 = jnp.zeros((n_slots, d), h2.dtype)
    return pl.pallas_call(
        _dispatch_kernel,
        grid_spec=pltpu.PrefetchScalarGridSpec(
            num_scalar_prefetch=1,
            grid=(n // tm,),
            in_specs=[pl.BlockSpec((tm, d), lambda i, sl: (i, 0)),
                      pl.BlockSpec(memory_space=pl.ANY)],
            out_specs=pl.BlockSpec(memory_space=pl.ANY),
            scratch_shapes=[pltpu.SemaphoreType.DMA(())],
        ),
        out_shape=jax.ShapeDtypeStruct((n_slots, d), h2.dtype),
        input_output_aliases={2: 0},
        compiler_params=_cparams(("arbitrary",)),
        name="moe_dispatch",
    )(slots, h2, xs0)


def _experts_kernel(te_ref, nv_ref, xs_ref, wg_ref, wu_ref, wd_ref, ys_ref, acc_ref):
    del te_ref
    s = pl.program_id(0)
    f = pl.program_id(1)

    @pl.when(s < nv_ref[0])
    def _():
        @pl.when(f == 0)
        def _():
            acc_ref[...] = jnp.zeros_like(acc_ref)

        x = xs_ref[...].astype(BF16)
        a = _silu(_dot(x, wg_ref[0])) * _dot(x, wu_ref[0])
        acc_ref[...] += _dot(a.astype(BF16), wd_ref[0])

        @pl.when(f == pl.num_programs(1) - 1)
        def _():
            ys_ref[...] = acc_ref[...]

    @pl.when((s >= nv_ref[0]) & (f == 0))
    def _():
        ys_ref[...] = jnp.zeros_like(ys_ref)


def moe_experts(tile_expert, n_valid, xs, wg, wu, wd):
    n_slots, d = xs.shape
    d_ff = wg.shape[2]
    tf = _ff_tile(d_ff)
    nf = d_ff // tf

    def live(s, nv):
        return jnp.minimum(s, nv[0] - 1)

    def fcol(s, f, nv):
        return jnp.where(s < nv[0], f, nf - 1)

    return pl.pallas_call(
        _experts_kernel,
        grid_spec=pltpu.PrefetchScalarGridSpec(
            num_scalar_prefetch=2,
            grid=(n_slots // MOE_TILE, nf),
            in_specs=[pl.BlockSpec((MOE_TILE, d), lambda s, f, te, nv: (live(s, nv), 0)),
                      pl.BlockSpec((1, d, tf), lambda s, f, te, nv: (te[s], 0, fcol(s, f, nv))),
                      pl.BlockSpec((1, d, tf), lambda s, f, te, nv: (te[s], 0, fcol(s, f, nv))),
                      pl.BlockSpec((1, tf, d), lambda s, f, te, nv: (te[s], fcol(s, f, nv), 0))],
            out_specs=pl.BlockSpec((MOE_TILE, d), lambda s, f, te, nv: (s, 0)),
            scratch_shapes=[pltpu.VMEM((MOE_TILE, d), F32)],
        ),
        out_shape=jax.ShapeDtypeStruct((n_slots, d), F32),
        compiler_params=_cparams(("arbitrary", "arbitrary")),
        name="moe_experts",
    )(tile_expert, n_valid, xs, wg, wu, wd)


def _combine_kernel(slots_ref, x1_ref, route_ref, ys_ref, o_ref, g_scr, sem):
    tm = x1_ref.shape[0]
    base = pl.program_id(0) * tm

    def start(r, c):
        for k in range(TOP_K):
            slot = slots_ref[k, base + r]
            pltpu.make_async_copy(ys_ref.at[pl.ds(slot, 1)], g_scr.at[k, pl.ds(r, 1)],
                                  sem).start(priority=k)
        return c

    lax.fori_loop(0, tm, start, 0)

    def wait(r, c):
        for k in range(TOP_K):
            pltpu.make_async_copy(ys_ref.at[pl.ds(0, 1)], g_scr.at[0, pl.ds(0, 1)], sem).wait()
        return c

    lax.fori_loop(0, tm, wait, 0)
    route = route_ref[...]
    out = x1_ref[...]
    for k in range(TOP_K):
        out = out + route[:, ROUTE_W + k:ROUTE_W + k + 1] * g_scr[k]
    o_ref[...] = out


def moe_combine(slots, x1, route, ys, *, tm):
    n, d = x1.shape
    return pl.pallas_call(
        _combine_kernel,
        grid_spec=pltpu.PrefetchScalarGridSpec(
            num_scalar_prefetch=1,
            grid=(n // tm,),
            in_specs=[pl.BlockSpec((tm, d), lambda i, sl: (i, 0)),
                      pl.BlockSpec((tm, route.shape[1]), lambda i, sl: (i, 0)),
                      pl.BlockSpec(memory_space=pl.ANY)],
            out_specs=pl.BlockSpec((tm, d), lambda i, sl: (i, 0)),
            scratch_shapes=[pltpu.VMEM((TOP_K, tm, d), F32), pltpu.SemaphoreType.DMA(())],
        ),
        out_shape=jax.ShapeDtypeStruct((n, d), F32),
        compiler_params=_cparams(("arbitrary",)),
        name="moe_combine",
    )(slots, x1, route, ys)


def moe_routed(x1, h2, w_router, wg, wu, wd, *, tm):
    n, _ = x1.shape
    n_e = wg.shape[0]
    route, cnt = router(h2, w_router, tm=tm)
    counts = cnt[0].astype(jnp.int32)
    padded = (counts + MOE_TILE - 1) // MOE_TILE * MOE_TILE
    ends = jnp.cumsum(padded)
    starts = ends - padded
    experts = route[:, ROUTE_E:ROUTE_E + TOP_K].astype(jnp.int32)
    ranks = route[:, ROUTE_RANK:ROUTE_RANK + TOP_K].astype(jnp.int32)
    slots = (starts[experts] + ranks).T
    n_tiles = (TOP_K * n + MOE_TILE - 1) // MOE_TILE + n_e
    tile_start = jnp.arange(n_tiles, dtype=jnp.int32) * MOE_TILE
    tile_expert = jnp.minimum(jnp.sum(tile_start[:, None] >= ends[None, :], axis=1), n_e - 1)
    n_valid = (ends[-1] // MOE_TILE).reshape(1)
    rows = min(MOE_ROW_TILE, n)
    xs = moe_dispatch(slots, h2, n_tiles * MOE_TILE, tm=rows)
    ys = moe_experts(tile_expert.astype(jnp.int32), n_valid.astype(jnp.int32), xs, wg, wu, wd)
    return moe_combine(slots, x1, route, ys, tm=rows)


def combine_matrix(route, n_e):
    eid = jnp.arange(n_e, dtype=F32)[None, :]
    comb = jnp.zeros((route.shape[0], n_e), F32)
    for k in range(TOP_K):
        comb = comb + jnp.where(route[:, ROUTE_E + k:ROUTE_E + k + 1] == eid,
                                route[:, ROUTE_W + k:ROUTE_W + k + 1], 0.0)
    return comb


def _moe_kernel(x1_ref, h_ref, c_ref, wg_ref, wu_ref, wd_ref, o_ref, acc_ref):
    e = pl.program_id(1)
    f = pl.program_id(2)

    @pl.when((e == 0) & (f == 0))
    def _():
        acc_ref[...] = x1_ref[...]

    n_e = c_ref.shape[1]
    eid = lax.broadcasted_iota(jnp.int32, c_ref.shape, 1)
    ce = jnp.sum(jnp.where(eid == e, c_ref[...], 0.0), axis=1, keepdims=True)
    h = h_ref[...].astype(BF16)
    a = _silu(_dot(h, wg_ref[0])) * _dot(h, wu_ref[0])
    acc_ref[...] += ce * _dot(a.astype(BF16), wd_ref[0])

    @pl.when((e == pl.num_programs(1) - 1) & (f == pl.num_programs(2) - 1))
    def _():
        o_ref[...] = acc_ref[...]


def moe_dense(x1, h2, comb, wg, wu, wd, *, tm):
    n, d = x1.shape
    n_e, _, d_ff = wg.shape
    tf = _ff_tile(d_ff)
    row = lambda i, e, f: (i, 0)
    return pl.pallas_call(
        _moe_kernel,
        grid=(n // tm, n_e, d_ff // tf),
        in_specs=[pl.BlockSpec((tm, d), row), pl.BlockSpec((tm, d), row),
                  pl.BlockSpec((tm, n_e), row),
                  pl.BlockSpec((1, d, tf), lambda i, e, f: (e, 0, f)),
                  pl.BlockSpec((1, d, tf), lambda i, e, f: (e, 0, f)),
                  pl.BlockSpec((1, tf, d), lambda i, e, f: (e, f, 0))],
        out_specs=pl.BlockSpec((tm, d), row),
        out_shape=jax.ShapeDtypeStruct((n, d), F32),
        scratch_shapes=[pltpu.VMEM((tm, d), F32)],
        compiler_params=_cparams(("parallel", "arbitrary", "arbitrary")),
        name="moe_dense",
    )(x1, h2, comb, wg, wu, wd)


def _rope_tables(pos):
    half = HEAD_DIM // 2
    inv = ROPE_BASE ** (-np.arange(half, dtype=np.float64) / half)
    ang = pos.astype(np.float64)[:, None] * inv[None, :]
    cos, sin = np.cos(ang), np.sin(ang)
    reps = LANES // HEAD_DIM
    cos_t = np.tile(np.concatenate([cos, cos], axis=1), (1, reps))
    sin_t = np.tile(np.concatenate([-sin, sin], axis=1), (1, reps))
    return jnp.asarray(cos_t.astype(np.float32)), jnp.asarray(sin_t.astype(np.float32))


def _token_tile(n):
    for tm in (512, 256, 128, 64, 32, 16, 8):
        if n % tm == 0:
            return tm
    return n


def kernel(x_prompt, x_sample, cache_k, cache_v, state_ret, page_table, g_mix, w_in, ret_gn,
           q_norm_g, k_norm_g, w_out, g_ffn, w_ffn_gate, w_ffn_up, w_ffn_down, w_router,
           w_exp_gate, w_exp_up, w_exp_down):
    b, t, d = x_prompt.shape
    db, ds, _ = x_sample.shape
    depth = w_in.shape[0]
    n_phys = cache_k.shape[1]
    n_p, n_s = b * t, db * ds
    tm_p, tm_s = _token_tile(n_p), _token_tile(n_s)
    past_len = page_table.shape[1] * PAGE_SIZE

    cos_p, sin_p = _rope_tables(np.arange(t))
    cos_s, sin_s = _rope_tables(np.broadcast_to(past_len + np.arange(ds)[None, :],
                                                (db, ds)).reshape(n_s))
    head_mean = (jnp.kron(jnp.eye(H_MOBA, dtype=F32), jnp.ones((HEAD_DIM, HEAD_DIM), F32))
                 / HEAD_DIM).astype(BF16)
    cache_kt = cache_k.transpose(0, 1, 3, 4, 2)
    cache_vt = cache_v.transpose(0, 1, 3, 4, 2)

    xp = x_prompt.reshape(n_p, d)
    xs = x_sample.reshape(n_s, d)
    sp_l, kp_l, vp_l, ss_l, ks_l, vs_l = [], [], [], [], [], []
    for l in range(depth):
        w_in_b = w_in[l].astype(BF16)
        w_out_b = w_out[l].astype(BF16)
        g1 = g_mix[l].reshape(1, d)
        g2 = g_ffn[l].reshape(1, d)
        qg = jnp.tile(q_norm_g[l], H_MOBA).reshape(1, W_MOBA)
        kg = jnp.tile(k_norm_g[l], H_MOBA).reshape(1, W_MOBA)
        gn = ret_gn[l].reshape(1, W_RET)

        (qr, kr, vr, gr, qm, km, vm, kmb, kmean) = proj_in(
            xp, g1, w_in_b, cos_p, sin_p, qg, kg, head_mean, tm=tm_p, with_kmean=True)
        seq = lambda a: a.reshape(b, t, a.shape[-1])
        o_r, s_fin = retention_prompt(seq(qr), seq(kr), seq(vr), seq(gr), gn)
        o_m = moba_prompt(seq(qm), seq(kmb), seq(vm), kmean.reshape(b, t // MOBA_BLOCK, W_MOBA))
        x1p, h2p = out_proj(o_r.reshape(n_p, W_RET), o_m.reshape(n_p, W_MOBA), xp, w_out_b, g2,
                            tm=tm_p)
        sp_l.append(s_fin)
        kp_l.append(km.reshape(b, t, H_MOBA, HEAD_DIM))
        vp_l.append(vm.reshape(b, t, H_MOBA, HEAD_DIM))

        (qr, kr, vr, gr, qm, km, vm) = proj_in(
            xs, g1, w_in_b, cos_s, sin_s, qg, kg, head_mean, tm=tm_s, with_kmean=False)
        f32 = lambda a: a.astype(F32)
        o_r, s_new = retention_sample(f32(qr), f32(kr), f32(vr), f32(gr), gn, state_ret[l])
        heads = lambda a: a.reshape(n_s, H_MOBA, HEAD_DIM)
        o_m = moba_sample(heads(qm), heads(km), heads(vm), cache_kt, cache_vt, page_table,
                          l).reshape(n_s, W_MOBA)
        x1s, h2s = out_proj(o_r.astype(BF16), o_m.astype(BF16), xs, w_out_b, g2, tm=tm_s)
        ss_l.append(s_new)
        ks_l.append(km.reshape(db, ds, H_MOBA, HEAD_DIM))
        vs_l.append(vm.reshape(db, ds, H_MOBA, HEAD_DIM))

        i = l // 2
        if l % 2 == 0:
            wg, wu, wd = (w_ffn_gate[i].astype(BF16), w_ffn_up[i].astype(BF16),
                          w_ffn_down[i].astype(BF16))
            xp = ffn_dense(x1p, h2p, wg, wu, wd, tm=tm_p)
            xs = ffn_dense(x1s, h2s, wg, wu, wd, tm=tm_s)
        else:
            wg, wu, wd = (w_exp_gate[i].astype(BF16), w_exp_up[i].astype(BF16),
                          w_exp_down[i].astype(BF16))
            xp = moe_routed(x1p, h2p, w_router[i], wg, wu, wd, tm=tm_p)
            route_s, _ = router(h2s, w_router[i], tm=tm_s)
            xs = moe_dense(x1s, h2s, combine_matrix(route_s, wg.shape[0]), wg, wu, wd, tm=tm_s)

    return (xp.reshape(b, t, d), xs.reshape(db, ds, d),
            jnp.stack(sp_l), jnp.stack(kp_l), jnp.stack(vp_l),
            jnp.stack(ss_l), jnp.stack(ks_l), jnp.stack(vs_l))
```

```python
import functools
import math

import jax
import jax.numpy as jnp
import numpy as np
from jax import lax
from jax.experimental import pallas as pl
from jax.experimental.pallas import tpu as pltpu

F32 = jnp.float32
BF16 = jnp.bfloat16

HEAD_DIM = 64
H_RET = 8
H_MOBA = 8
W_RET = H_RET * HEAD_DIM
W_MOBA = H_MOBA * HEAD_DIM
PAGE_SIZE = 128
ROPE_BASE = 10000.0
MOBA_BLOCK = 256
MOBA_TOPK = 3
TOP_K = 2
EPS = 1e-6
RET_CHUNK = 256
MOBA_KEY_CHUNK = 128
MOBA_UNROLL = 4
SUM_ROWS = 16
FF_TILE_MAX = 1408
MOE_TILE = 512
MOE_ROW_TILE = 256
ROW_DMA_UNROLL = 8
LANES = 128
NEG = -1e30
VMEM_LIMIT = 56 * 1024 * 1024

HIGHEST = lax.Precision.HIGHEST


def _cparams(sem):
    return pltpu.CompilerParams(dimension_semantics=sem, vmem_limit_bytes=VMEM_LIMIT)


def _silu(x):
    return x / (1.0 + jnp.exp(-x))


def _dot(a, b):
    return jnp.dot(a, b, preferred_element_type=F32)


def _dot_nt(a, b, precision=None):
    return lax.dot_general(a, b, (((1,), (1,)), ((), ())), precision=precision,
                           preferred_element_type=F32)


def _dot_tn(a, b):
    return lax.dot_general(a, b, (((0,), (0,)), ((), ())), preferred_element_type=F32)


def _split3(x):
    hi = x.astype(BF16)
    r = x - hi.astype(F32)
    mid = r.astype(BF16)
    lo = (r - mid.astype(F32)).astype(BF16)
    return hi, mid, lo


def _proj_in_kernel(x_ref, g_ref, w_ref, cos_ref, sin_ref, qg_ref, kg_ref, hm_ref, *refs,
                    prompt, first_layer):
    if prompt and not first_layer:
        refs = refs[2:]
    qr_ref, kr_ref, vr_ref, gr_ref, qm_ref = refs[:5]
    x = x_ref[...]
    tm = x.shape[0]
    ms = jnp.mean(x * x, axis=-1, keepdims=True)
    h = (x * lax.rsqrt(ms + EPS) * g_ref[...]).astype(BF16)

    def proj(c0, width):
        return _dot(h, w_ref[:, c0:c0 + width])

    cos = cos_ref[...]
    sin = sin_ref[...]
    lane = lax.broadcasted_iota(jnp.int32, (tm, LANES), 1)
    first_half = (lane % HEAD_DIM) < (HEAD_DIM // 2)

    def rope(z):
        outs = []
        for c in range(z.shape[1] // LANES):
            zc = z[:, c * LANES:(c + 1) * LANES]
            rot = jnp.where(first_half,
                            pltpu.roll(zc, LANES - HEAD_DIM // 2, 1),
                            pltpu.roll(zc, HEAD_DIM // 2, 1))
            outs.append(zc * cos + rot * sin)
        return jnp.concatenate(outs, axis=1)

    hm = hm_ref[...]

    def head_norm(z, g):
        sq = z * z
        hi = sq.astype(BF16)
        lo = (sq - hi.astype(F32)).astype(BF16)
        msq = _dot(hi, hm) + _dot(lo, hm)
        return z * lax.rsqrt(msq + EPS) * g

    qr_ref[...] = rope(proj(0, W_RET)).astype(BF16)
    kr_ref[...] = (rope(proj(W_RET, W_RET)) * (HEAD_DIM ** -0.5)).astype(BF16)
    vr_ref[...] = proj(2 * W_RET, W_RET).astype(BF16)
    gr_ref[...] = proj(3 * W_RET, W_RET).astype(BF16)
    base = 4 * W_RET
    qm_ref[...] = head_norm(proj(base, W_MOBA), qg_ref[...])
    km = head_norm(proj(base + W_MOBA, W_MOBA), kg_ref[...])
    vm = proj(base + 2 * W_MOBA, W_MOBA)
    if not prompt:
        km_ref, vm_ref = refs[5:]
        km_ref[...] = km
        vm_ref[...] = vm
        return
    kmb_ref, kmean_ref, kt_ref, vt_ref = refs[5:]
    kmb_ref[...] = km.astype(BF16)
    for r in range(tm // MOBA_BLOCK):
        kmean_ref[0, r:r + 1, :] = jnp.mean(
            km[r * MOBA_BLOCK:(r + 1) * MOBA_BLOCK], axis=0, keepdims=True)
    for src, dst in ((km, kt_ref), (vm, vt_ref)):
        if first_layer:
            if dst.shape[0] > 1:
                dst[1:] = jnp.zeros((dst.shape[0] - 1,) + dst.shape[1:], F32)
            dst = dst.at[0]
        src_t = src.T
        for hd in range(H_MOBA):
            dst[hd] = src_t[hd * HEAD_DIM:(hd + 1) * HEAD_DIM]


def proj_in(x, g, w_bf, cos, sin, qg, kg, hm, *, tm, prompt=None):
    n, d = x.shape
    n_pos_tiles = cos.shape[0] // tm
    row = lambda i: (i, 0)
    fixed = lambda i: (0, 0)
    wide = lambda dt: jax.ShapeDtypeStruct((n, W_RET), dt)
    out_shape = [wide(BF16), wide(BF16), wide(BF16), wide(BF16), wide(F32)]
    out_specs = [pl.BlockSpec((tm, W_RET), row)] * 5
    in_specs = [
        pl.BlockSpec((tm, d), row),
        pl.BlockSpec((1, d), fixed),
        pl.BlockSpec(w_bf.shape, fixed),
        pl.BlockSpec((tm, LANES), lambda i: (i % n_pos_tiles, 0)),
        pl.BlockSpec((tm, LANES), lambda i: (i % n_pos_tiles, 0)),
        pl.BlockSpec((1, W_MOBA), fixed),
        pl.BlockSpec((1, W_MOBA), fixed),
        pl.BlockSpec((W_MOBA, W_MOBA), fixed),
    ]
    args = [x, g, w_bf, cos, sin, qg, kg, hm]
    aliases = {}
    first_layer = True
    if prompt is None:
        out_shape += [wide(F32), wide(F32)]
        out_specs += [pl.BlockSpec((tm, W_RET), row)] * 2
    else:
        layer, depth, batch, prev = prompt
        first_layer = prev is None
        tiles_per_seq = n // batch // tm
        nb = tm // MOBA_BLOCK
        kv_shape = jax.ShapeDtypeStruct((depth, batch, H_MOBA, HEAD_DIM, n // batch), F32)
        if first_layer:
            kv_spec = pl.BlockSpec((depth, None, H_MOBA, HEAD_DIM, tm),
                                   lambda i: (0, i // tiles_per_seq, 0, 0, i % tiles_per_seq))
        else:
            kv_spec = pl.BlockSpec((None, None, H_MOBA, HEAD_DIM, tm),
                                   lambda i: (layer, i // tiles_per_seq, 0, 0, i % tiles_per_seq))
            aliases = {len(args): len(out_shape) + 2, len(args) + 1: len(out_shape) + 3}
            in_specs += [pl.BlockSpec(memory_space=pl.ANY)] * 2
            args += list(prev)
        out_shape += [wide(BF16), jax.ShapeDtypeStruct((n // tm, nb, W_MOBA), F32),
                      kv_shape, kv_shape]
        out_specs += [pl.BlockSpec((tm, W_RET), row),
                      pl.BlockSpec((1, nb, W_MOBA), lambda i: (i, 0, 0)), kv_spec, kv_spec]
    return pl.pallas_call(
        functools.partial(_proj_in_kernel, prompt=prompt is not None, first_layer=first_layer),
        grid=(n // tm,),
        in_specs=in_specs,
        out_specs=out_specs,
        out_shape=out_shape,
        input_output_aliases=aliases,
        compiler_params=_cparams(("parallel",)),
        name="proj_in",
    )(*args)


def _ret_prompt_kernel(q_ref, k_ref, v_ref, g_ref, gn_ref, dm_ref, cross_ref, kdec_ref,
                       gc_ref, o_ref, sfin_ref, s_scr):
    c = pl.program_id(1)

    @pl.when(c == 0)
    def _():
        s_scr[...] = jnp.zeros_like(s_scr)

    heads = range(H_RET)
    sls = [slice(h * HEAD_DIM, (h + 1) * HEAD_DIM) for h in heads]
    qs = [q_ref[0, :, sl] for sl in sls]
    ks = [k_ref[0, :, sl] for sl in sls]
    vs = [v_ref[0, :, sl] for sl in sls]
    s0s = [s_scr[h] for h in heads]
    scores = [_dot_nt(qs[h], ks[h]) for h in heads]
    cross = [_dot(qs[h], s0s[h].astype(BF16)) for h in heads]
    kds = [(ks[h].astype(F32) * kdec_ref[h]).astype(BF16) for h in heads]
    kvs = [_dot_tn(kds[h], vs[h]) for h in heads]
    inner = [_dot((scores[h] * dm_ref[h]).astype(BF16), vs[h]) for h in heads]
    outs = []
    for h in heads:
        s_scr[h] = gc_ref[h] * s0s[h] + kvs[h]
        o = inner[h] + cross[h] * cross_ref[h]
        mu = jnp.mean(o, axis=-1, keepdims=True)
        oc = o - mu
        var = jnp.mean(oc * oc, axis=-1, keepdims=True)
        on = oc * lax.rsqrt(var + EPS) * gn_ref[:, sls[h]]
        outs.append(on * _silu(g_ref[0, :, sls[h]].astype(F32)))
    o_ref[0] = jnp.concatenate(outs, axis=1).astype(BF16)

    @pl.when(c == pl.num_programs(1) - 1)
    def _():
        sfin_ref[0] = s_scr[...]


def _ret_log_decay():
    return np.log1p(-np.exp2(-5.0 - np.arange(H_RET, dtype=np.float64)))


def retention_prompt(qr, kr, vr, gr, gn):
    b, t, w = qr.shape
    c = min(RET_CHUNK, t)
    ld = _ret_log_decay()
    i = np.arange(c, dtype=np.float64)
    diff = i[:, None] - i[None, :]
    f32c = lambda a: jnp.asarray(np.ascontiguousarray(a, dtype=np.float32))
    dmask = f32c(np.where(diff >= 0, np.exp(ld[:, None, None] * np.maximum(diff, 0.0)), 0.0))
    bc = lambda a: f32c(np.broadcast_to(a[:, :, None], (H_RET, c, HEAD_DIM)))
    cross = bc(np.exp(ld[:, None] * (i + 1.0)[None, :]))
    kdec = bc(np.exp(ld[:, None] * (c - 1.0 - i)[None, :]))
    gc = f32c(np.broadcast_to(np.exp(ld * c)[:, None, None], (H_RET, HEAD_DIM, HEAD_DIM)))
    tile = pl.BlockSpec((1, c, w), lambda bi, ci: (bi, ci, 0))
    const3 = lambda shape: pl.BlockSpec(shape, lambda bi, ci: (0, 0, 0))
    return pl.pallas_call(
        _ret_prompt_kernel,
        grid=(b, t // c),
        in_specs=[tile, tile, tile, tile,
                  pl.BlockSpec((1, w), lambda bi, ci: (0, 0)),
                  const3((H_RET, c, c)), const3((H_RET, c, HEAD_DIM)),
                  const3((H_RET, c, HEAD_DIM)), const3((H_RET, HEAD_DIM, HEAD_DIM))],
        out_specs=[tile,
                   pl.BlockSpec((1, H_RET, HEAD_DIM, HEAD_DIM), lambda bi, ci: (bi, 0, 0, 0))],
        out_shape=[jax.ShapeDtypeStruct((b, t, w), BF16),
                   jax.ShapeDtypeStruct((b, H_RET, HEAD_DIM, HEAD_DIM), F32)],
        scratch_shapes=[pltpu.VMEM((H_RET, HEAD_DIM, HEAD_DIM), F32)],
        compiler_params=_cparams(("parallel", "arbitrary")),
        name="retention_prompt",
    )(qr, kr, vr, gr, gn, dmask, cross, kdec, gc)


def _moba_prompt_kernel(q_ref, k_ref, v_ref, km_ref, o_ref, vt_scr, sel_scr, m_scr, acc_scr):
    qi = pl.program_id(2)
    nb = km_ref.shape[1]
    bs = MOBA_BLOCK
    kc = MOBA_KEY_CHUNK
    n_chunks = bs // kc
    n_heads = LANES // HEAD_DIM

    @pl.when(qi == 0)
    def _():
        for j in range(nb):
            for hh in range(n_heads):
                vt_scr[j, hh, :HEAD_DIM] = v_ref[hh, :, j * bs:(j + 1) * bs].astype(BF16)
                vt_scr[j, hh, HEAD_DIM:] = jnp.ones((SUM_ROWS, bs), BF16)

    qt = (q_ref[0] * (HEAD_DIM ** -0.5 * math.log2(math.e))).T
    drow = lax.broadcasted_iota(jnp.int32, qt.shape, 0)
    blk = lax.broadcasted_iota(jnp.int32, (nb, bs), 0)
    kmean = km_ref[0]
    qts = []
    for hh in range(n_heads):
        qh = jnp.where(drow // HEAD_DIM == hh, qt, 0.0)
        qts.append(qh.astype(BF16))
        gate = jnp.dot(kmean, qh, precision=HIGHEST, preferred_element_type=F32)
        g = jnp.where(blk < qi, gate, -jnp.inf)
        sel = jnp.zeros((nb, bs), F32)
        for _ in range(min(MOBA_TOPK, nb)):
            mx = jnp.max(g, axis=0, keepdims=True)
            idx = jnp.min(jnp.where(g == mx, blk, nb), axis=0, keepdims=True)
            pick = blk == idx
            sel = jnp.where(pick, 1.0, sel)
            g = jnp.where(pick, -jnp.inf, g)
        sel_scr[hh] = jnp.where(blk < qi, sel, 0.0)

    m_scr[...] = jnp.full(m_scr.shape, NEG, F32)
    acc_scr[...] = jnp.zeros(acc_scr.shape, F32)

    def merge(hh, mb, pv, sel_row=None):
        m = m_scr[hh]
        m_new = jnp.maximum(m, mb)
        wb = jnp.exp2(mb - m_new)
        if sel_row is not None:
            m_new = jnp.where(sel_row, m_new, m)
            wb = jnp.where(sel_row, wb, 0.0)
        wa = jnp.exp2(m - m_new)
        m_scr[hh] = m_new
        acc_scr[hh] = wa * acc_scr[hh] + wb * pv

    krow = lax.broadcasted_iota(jnp.int32, (kc, bs), 0)
    qcol = lax.broadcasted_iota(jnp.int32, (kc, bs), 1)

    def blocks(j0, n_blocks, own):
        units = [(u, c, hh) for u in range(n_blocks) for c in range(n_chunks)
                 for hh in range(n_heads)]
        k_rows = k_ref[0, pl.ds(pl.multiple_of(j0 * bs, bs), n_blocks * bs), :]
        wide = [_dot(k_rows, qts[hh]) for hh in range(n_heads)]
        parts = []
        for u, c, hh in units:
            s = wide[hh][u * bs + c * kc:u * bs + (c + 1) * kc]
            if own:
                s = jnp.where(krow + c * kc <= qcol, s, NEG)
            mb = jnp.max(s, axis=0, keepdims=True)
            parts.append((mb, jnp.exp2(s - mb).astype(BF16)))
        pvs = [_dot(vt_scr[j0 + u, hh, :, c * kc:(c + 1) * kc], parts[i][1])
               for i, (u, c, hh) in enumerate(units)]
        for i, (u, c, hh) in enumerate(units):
            sel_row = None if own else sel_scr[hh, pl.ds(j0 + u, 1), :] > 0.0
            merge(hh, parts[i][0], pvs[i], sel_row)

    def group_body(g, c_):
        blocks(g * MOBA_UNROLL, MOBA_UNROLL, False)
        return c_

    n_groups = qi // MOBA_UNROLL
    lax.fori_loop(0, n_groups, group_body, 0)
    done = n_groups * MOBA_UNROLL
    piece = MOBA_UNROLL // 2
    while piece >= 1:
        @pl.when((qi & piece) != 0)
        def _(done=done, piece=piece):
            blocks(done, piece, False)
        done = done + (qi & piece)
        piece //= 2
    blocks(qi, 1, True)
    outs = [acc_scr[hh, :HEAD_DIM] / acc_scr[hh, HEAD_DIM:HEAD_DIM + 1]
            for hh in range(n_heads)]
    o_ref[0] = jnp.concatenate(outs, axis=0).T.astype(BF16)


def moba_prompt(qm, kmb, vt_all, kmean, layer):
    b, t, w = qm.shape
    nb = t // MOBA_BLOCK
    n_heads = LANES // HEAD_DIM
    return pl.pallas_call(
        _moba_prompt_kernel,
        grid=(b, w // LANES, nb),
        in_specs=[
            pl.BlockSpec((1, MOBA_BLOCK, LANES), lambda bi, hp, qi: (bi, qi, hp)),
            pl.BlockSpec((1, t, LANES), lambda bi, hp, qi: (bi, 0, hp)),
            pl.BlockSpec((None, None, n_heads, HEAD_DIM, t),
                         lambda bi, hp, qi: (layer, bi, hp, 0, 0)),
            pl.BlockSpec((1, nb, LANES), lambda bi, hp, qi: (bi, 0, hp)),
        ],
        out_specs=pl.BlockSpec((1, MOBA_BLOCK, LANES), lambda bi, hp, qi: (bi, qi, hp)),
        out_shape=jax.ShapeDtypeStruct((b, t, w), BF16),
        scratch_shapes=[pltpu.VMEM((nb, n_heads, HEAD_DIM + SUM_ROWS, MOBA_BLOCK), BF16),
                        pltpu.VMEM((n_heads, nb, MOBA_BLOCK), F32),
                        pltpu.VMEM((n_heads, 1, MOBA_BLOCK), F32),
                        pltpu.VMEM((n_heads, HEAD_DIM + SUM_ROWS, MOBA_BLOCK), F32)],
        compiler_params=_cparams(("parallel", "parallel", "arbitrary")),
        name="moba_prompt",
    )(qm, kmb, vt_all, kmean)


def _ret_sample_kernel(q_ref, k_ref, v_ref, g_ref, gam_ref, gn_ref, s0_ref, e1_ref, e2_ref,
                       o_ref, s_ref):
    e1 = e1_ref[...]
    e2 = e2_ref[...]

    def expand(x, e):
        hi, mid, lo = _split3(x)
        return _dot(hi, e) + _dot(mid, e) + _dot(lo, e)

    kexp = expand(k_ref[...], e1)
    vexp = expand(v_ref[...], e2)
    qexp = expand(q_ref[...], e1)
    s_new = gam_ref[:, :1] * s0_ref[...] + kexp * vexp
    s_ref[...] = s_new
    hi, mid, lo = _split3(qexp * s_new)
    o = _dot_nt(hi, e2) + _dot_nt(mid, e2) + _dot_nt(lo, e2)
    mu = jnp.mean(o, axis=-1, keepdims=True)
    oc = o - mu
    var = jnp.mean(oc * oc, axis=-1, keepdims=True)
    o_ref[...] = oc * lax.rsqrt(var + EPS) * gn_ref[...] * _silu(g_ref[...])


def retention_sample(q, k, v, gr, gn, s0):
    n = q.shape[0]
    r = n * H_RET
    dd = HEAD_DIM * HEAD_DIM
    rows = lambda a: a.reshape(r, HEAD_DIM)
    gam = jnp.asarray(np.tile(np.broadcast_to(np.exp(_ret_log_decay())[:, None],
                                              (H_RET, HEAD_DIM)), (n, 1)).astype(np.float32))
    gnr = jnp.tile(gn.reshape(H_RET, HEAD_DIM), (n, 1))
    lane = np.arange(dd)
    e1 = jnp.asarray(lane[None, :] // HEAD_DIM == np.arange(HEAD_DIM)[:, None], BF16)
    e2 = jnp.asarray(lane[None, :] % HEAD_DIM == np.arange(HEAD_DIM)[:, None], BF16)
    tr = min(128, r)
    small = pl.BlockSpec((tr, HEAD_DIM), lambda i: (i, 0))
    big = pl.BlockSpec((tr, dd), lambda i: (i, 0))
    emat = pl.BlockSpec((HEAD_DIM, dd), lambda i: (0, 0))
    o, s_new = pl.pallas_call(
        _ret_sample_kernel,
        grid=(r // tr,),
        in_specs=[small, small, small, small, small, small, big, emat, emat],
        out_specs=[small, big],
        out_shape=[jax.ShapeDtypeStruct((r, HEAD_DIM), F32),
                   jax.ShapeDtypeStruct((r, dd), F32)],
        compiler_params=_cparams(("parallel",)),
        name="retention_sample",
    )(rows(q), rows(k), rows(v), rows(gr), gam, gnr, s0.reshape(r, dd), e1, e2)
    return o.reshape(n, W_RET), s_new.reshape(n, H_RET, HEAD_DIM, HEAD_DIM)


def _moba_sample_kernel(pt_ref, q_ref, qt_ref, kn_ref, vn_ref, *refs, n_pages):
    del pt_ref
    k_refs = refs[:n_pages]
    v_refs = refs[n_pages:2 * n_pages]
    o_ref = refs[2 * n_pages]
    ppb = MOBA_BLOCK // PAGE_SIZE
    nbp = n_pages // ppb
    qscale = HEAD_DIM ** -0.5 * math.log2(math.e)
    qt = qt_ref[0] * qscale
    qcols = [jnp.broadcast_to(qt[:, h:h + 1], (HEAD_DIM, PAGE_SIZE)) for h in range(H_MOBA)]
    s_pages = []
    for p in range(n_pages):
        rows = [jnp.sum(k_refs[p][h] * qcols[h], axis=0, keepdims=True) for h in range(H_MOBA)]
        s_pages.append(jnp.concatenate(rows, axis=0))
    gates, maxes = [], []
    for j in range(nbp):
        tot = s_pages[j * ppb]
        top = s_pages[j * ppb]
        for p in range(1, ppb):
            tot = tot + s_pages[j * ppb + p]
            top = jnp.maximum(top, s_pages[j * ppb + p])
        gates.append(jnp.sum(tot, axis=1, keepdims=True))
        maxes.append(jnp.max(top, axis=1, keepdims=True))
    k_sel = min(MOBA_TOPK, nbp + 1)
    sels = []
    for j in range(nbp):
        rank = jnp.zeros(gates[j].shape, F32)
        for i in range(nbp):
            if i == j:
                continue
            ahead = (gates[i] > gates[j]) | ((gates[i] == gates[j]) & (i < j))
            rank = rank + jnp.where(ahead, 1.0, 0.0)
        sels.append(rank < k_sel)
    s_own = jnp.sum(q_ref[0] * kn_ref[0], axis=1, keepdims=True) * qscale
    m = s_own
    for j in range(nbp):
        m = jnp.maximum(m, jnp.where(sels[j], maxes[j], NEG))
    p_own = jnp.exp2(s_own - m)
    l = p_own
    accs = [jnp.zeros((HEAD_DIM, PAGE_SIZE), F32) for _ in range(H_MOBA)]
    for p in range(n_pages):
        shift = jnp.where(sels[p // ppb], m, -NEG)
        pp = jnp.exp2(s_pages[p] - shift)
        l = l + jnp.sum(pp, axis=1, keepdims=True)
        for h in range(H_MOBA):
            accs[h] = accs[h] + pp[h:h + 1, :] * v_refs[p][h]
    o_past = jnp.concatenate([jnp.sum(a.T, axis=0, keepdims=True) for a in accs], axis=0)
    o_ref[0] = (o_past + p_own * vn_ref[0]) / l


def moba_sample(q, k_new, v_new, cache_k, cache_v, page_table, layer):
    n = q.shape[0]
    n_pages = page_table.shape[1]
    vec = pl.BlockSpec((1, H_MOBA, HEAD_DIM), lambda s, pt: (s, 0, 0))
    vec_t = pl.BlockSpec((1, HEAD_DIM, H_MOBA), lambda s, pt: (s, 0, 0))

    def page_spec(p):
        return pl.BlockSpec((None, None, H_MOBA, HEAD_DIM, PAGE_SIZE),
                            lambda s, pt: (layer, pt[s, p], 0, 0, 0))

    pages = [page_spec(p) for p in range(n_pages)]
    return pl.pallas_call(
        functools.partial(_moba_sample_kernel, n_pages=n_pages),
        grid_spec=pltpu.PrefetchScalarGridSpec(
            num_scalar_prefetch=1,
            grid=(n,),
            in_specs=[vec, vec_t, vec, vec] + pages + pages,
            out_specs=vec,
        ),
        out_shape=jax.ShapeDtypeStruct((n, H_MOBA, HEAD_DIM), F32),
        compiler_params=_cparams(("arbitrary",)),
        name="moba_sample",
    )(page_table, q, q.transpose(0, 2, 1), k_new, v_new,
      *([cache_k] * n_pages), *([cache_v] * n_pages))


def _out_proj_kernel(or_ref, om_ref, x_ref, w_ref, g_ref, x1_ref, h2_ref):
    x1 = (x_ref[...] + _dot(or_ref[...], w_ref[:W_RET, :])
          + _dot(om_ref[...], w_ref[W_RET:, :]))
    x1_ref[...] = x1
    ms = jnp.mean(x1 * x1, axis=-1, keepdims=True)
    h2_ref[...] = x1 * lax.rsqrt(ms + EPS) * g_ref[...]


def out_proj(o_r, o_m, x, w_bf, g, *, tm):
    n, d = x.shape
    row = lambda i: (i, 0)
    fixed = lambda i: (0, 0)
    return pl.pallas_call(
        _out_proj_kernel,
        grid=(n // tm,),
        in_specs=[pl.BlockSpec((tm, W_RET), row), pl.BlockSpec((tm, W_MOBA), row),
                  pl.BlockSpec((tm, d), row), pl.BlockSpec(w_bf.shape, fixed),
                  pl.BlockSpec((1, d), fixed)],
        out_specs=[pl.BlockSpec((tm, d), row), pl.BlockSpec((tm, d), row)],
        out_shape=[jax.ShapeDtypeStruct((n, d), F32), jax.ShapeDtypeStruct((n, d), F32)],
        compiler_params=_cparams(("parallel",)),
        name="out_proj",
    )(o_r, o_m, x, w_bf, g)


def _ffn_kernel(x1_ref, h_ref, wg_ref, wu_ref, wd_ref, o_ref, acc_ref):
    f = pl.program_id(1)

    @pl.when(f == 0)
    def _():
        acc_ref[...] = x1_ref[...]

    h = h_ref[...].astype(BF16)
    a = _silu(_dot(h, wg_ref[...])) * _dot(h, wu_ref[...])
    acc_ref[...] += _dot(a.astype(BF16), wd_ref[...])

    @pl.when(f == pl.num_programs(1) - 1)
    def _():
        o_ref[...] = acc_ref[...]


def _ff_tile(d_ff):
    for tf in range(FF_TILE_MAX, 0, -LANES):
        if d_ff % tf == 0:
            return tf
    return d_ff


def ffn_dense(x1, h2, wg, wu, wd, *, tm):
    n, d = x1.shape
    d_ff = wg.shape[1]
    tf = _ff_tile(d_ff)
    row = lambda i, f: (i, 0)
    return pl.pallas_call(
        _ffn_kernel,
        grid=(n // tm, d_ff // tf),
        in_specs=[pl.BlockSpec((tm, d), row), pl.BlockSpec((tm, d), row),
                  pl.BlockSpec((d, tf), lambda i, f: (0, f)),
                  pl.BlockSpec((d, tf), lambda i, f: (0, f)),
                  pl.BlockSpec((tf, d), lambda i, f: (f, 0))],
        out_specs=pl.BlockSpec((tm, d), row),
        out_shape=jax.ShapeDtypeStruct((n, d), F32),
        scratch_shapes=[pltpu.VMEM((tm, d), F32)],
        compiler_params=_cparams(("parallel", "arbitrary")),
        name="ffn_dense",
    )(x1, h2, wg, wu, wd)


ROUTE_E, ROUTE_RANK, ROUTE_W = 0, 2, 4


def _router_kernel(h_ref, w_ref, route_ref, cnt_ref, carry_scr):
    @pl.when(pl.program_id(0) == 0)
    def _():
        carry_scr[...] = jnp.zeros_like(carry_scr)

    logits = jnp.dot(h_ref[...], w_ref[...], precision=HIGHEST, preferred_element_type=F32)
    tm, n_e = logits.shape
    eid = lax.broadcasted_iota(jnp.int32, logits.shape, 1)
    v1 = jnp.max(logits, axis=1, keepdims=True)
    i1 = jnp.min(jnp.where(logits == v1, eid, n_e), axis=1, keepdims=True)
    rest = jnp.where(eid == i1, -jnp.inf, logits)
    v2 = jnp.max(rest, axis=1, keepdims=True)
    i2 = jnp.min(jnp.where(rest == v2, eid, n_e), axis=1, keepdims=True)
    e2 = jnp.exp(v2 - v1)
    w1 = 1.0 / (1.0 + e2)
    w2 = e2 / (1.0 + e2)
    onehot = jnp.where((eid == i1) | (eid == i2), 1.0, 0.0)
    r = lax.broadcasted_iota(jnp.int32, (tm, tm), 0)
    c = lax.broadcasted_iota(jnp.int32, (tm, tm), 1)
    earlier = jnp.where(c < r, 1.0, 0.0).astype(BF16)
    ranks = _dot(earlier, onehot.astype(BF16)) + carry_scr[...]
    r1 = jnp.sum(jnp.where(eid == i1, ranks, 0.0), axis=1, keepdims=True)
    r2 = jnp.sum(jnp.where(eid == i2, ranks, 0.0), axis=1, keepdims=True)
    total = carry_scr[...] + jnp.sum(onehot, axis=0, keepdims=True)
    carry_scr[...] = total
    cnt_ref[...] = total
    route = jnp.zeros(logits.shape, F32)
    for k, col in enumerate((i1.astype(F32), i2.astype(F32), r1, r2, w1, w2)):
        route = jnp.where(eid == k, col, route)
    route_ref[...] = route


def router(h2, w_router, *, tm):
    n, d = h2.shape
    n_e = w_router.shape[1]
    return pl.pallas_call(
        _router_kernel,
        grid=(n // tm,),
        in_specs=[pl.BlockSpec((tm, d), lambda i: (i, 0)),
                  pl.BlockSpec((d, n_e), lambda i: (0, 0))],
        out_specs=[pl.BlockSpec((tm, n_e), lambda i: (i, 0)),
                   pl.BlockSpec((1, n_e), lambda i: (0, 0))],
        out_shape=[jax.ShapeDtypeStruct((n, n_e), F32), jax.ShapeDtypeStruct((1, n_e), F32)],
        scratch_shapes=[pltpu.VMEM((1, n_e), F32)],
        compiler_params=_cparams(("arbitrary",)),
        name="router",
    )(h2, w_router)


def _dispatch_kernel(slots_ref, h_ref, xs_in_ref, xs_ref, sem):
    del xs_in_ref
    tm = h_ref.shape[0]
    base = pl.program_id(0) * tm

    def start(r, c):
        for k in range(TOP_K):
            slot = slots_ref[k, base + r]
            pltpu.make_async_copy(h_ref.at[pl.ds(r, 1)], xs_ref.at[pl.ds(slot, 1)],
                                  sem).start(priority=k)
        return c

    lax.fori_loop(0, tm, start, 0, unroll=ROW_DMA_UNROLL)

    def wait(r, c):
        for k in range(TOP_K):
            pltpu.make_async_copy(h_ref.at[pl.ds(0, 1)], xs_ref.at[pl.ds(0, 1)], sem).wait()
        return c

    lax.fori_loop(0, tm, wait, 0, unroll=ROW_DMA_UNROLL)


def moe_dispatch(slots, h2, n_slots, *, tm):
    n, d = h2.shape
    xs0 = jnp.zeros((n_slots, d), h2.dtype)
    return pl.pallas_call(
        _dispatch_kernel,
        grid_spec=pltpu.PrefetchScalarGridSpec(
            num_scalar_prefetch=1,
            grid=(n // tm,),
            in_specs=[pl.BlockSpec((tm, d), lambda i, sl: (i, 0)),
                      pl.BlockSpec(memory_space=pl.ANY)],
            out_specs=pl.BlockSpec(memory_space=pl.ANY),
            scratch_shapes=[pltpu.SemaphoreType.DMA(())],
        ),
        out_shape=jax.ShapeDtypeStruct((n_slots, d), h2.dtype),
        input_output_aliases={2: 0},
        compiler_params=_cparams(("arbitrary",)),
        name="moe_dispatch",
    )(slots, h2, xs0)


def _experts_kernel(te_ref, nv_ref, xs_ref, wg_ref, wu_ref, wd_ref, ys_ref, acc_ref):
    del te_ref
    s = pl.program_id(0)
    f = pl.program_id(1)

    @pl.when(s < nv_ref[0])
    def _():
        @pl.when(f == 0)
        def _():
            acc_ref[...] = jnp.zeros_like(acc_ref)

        x = xs_ref[...].astype(BF16)
        a = _silu(_dot(x, wg_ref[0])) * _dot(x, wu_ref[0])
        acc_ref[...] += _dot(a.astype(BF16), wd_ref[0])

        @pl.when(f == pl.num_programs(1) - 1)
        def _():
            ys_ref[...] = acc_ref[...]

    @pl.when((s >= nv_ref[0]) & (f == 0))
    def _():
        ys_ref[...] = jnp.zeros_like(ys_ref)


def moe_experts(tile_expert, n_valid, xs, wg, wu, wd):
    n_slots, d = xs.shape
    d_ff = wg.shape[2]
    tf = _ff_tile(d_ff)
    nf = d_ff // tf

    def live(s, nv):
        return jnp.minimum(s, nv[0] - 1)

    def fcol(s, f, nv):
        return jnp.where(s < nv[0], f, nf - 1)

    return pl.pallas_call(
        _experts_kernel,
        grid_spec=pltpu.PrefetchScalarGridSpec(
            num_scalar_prefetch=2,
            grid=(n_slots // MOE_TILE, nf),
            in_specs=[pl.BlockSpec((MOE_TILE, d), lambda s, f, te, nv: (live(s, nv), 0)),
                      pl.BlockSpec((1, d, tf), lambda s, f, te, nv: (te[s], 0, fcol(s, f, nv))),
                      pl.BlockSpec((1, d, tf), lambda s, f, te, nv: (te[s], 0, fcol(s, f, nv))),
                      pl.BlockSpec((1, tf, d), lambda s, f, te, nv: (te[s], fcol(s, f, nv), 0))],
            out_specs=pl.BlockSpec((MOE_TILE, d), lambda s, f, te, nv: (s, 0)),
            scratch_shapes=[pltpu.VMEM((MOE_TILE, d), F32)],
        ),
        out_shape=jax.ShapeDtypeStruct((n_slots, d), F32),
        compiler_params=_cparams(("arbitrary", "arbitrary")),
        name="moe_experts",
    )(tile_expert, n_valid, xs, wg, wu, wd)


def _combine_kernel(slots_ref, x1_ref, route_ref, ys_ref, o_ref, g_scr, sem):
    tm = x1_ref.shape[0]
    base = pl.program_id(0) * tm

    def start(r, c):
        for k in range(TOP_K):
            slot = slots_ref[k, base + r]
            pltpu.make_async_copy(ys_ref.at[pl.ds(slot, 1)], g_scr.at[k, pl.ds(r, 1)],
                                  sem).start(priority=k)
        return c

    lax.fori_loop(0, tm, start, 0, unroll=ROW_DMA_UNROLL)

    def wait(r, c):
        for k in range(TOP_K):
            pltpu.make_async_copy(ys_ref.at[pl.ds(0, 1)], g_scr.at[0, pl.ds(0, 1)], sem).wait()
        return c

    lax.fori_loop(0, tm, wait, 0, unroll=ROW_DMA_UNROLL)
    route = route_ref[...]
    out = x1_ref[...]
    for k in range(TOP_K):
        out = out + route[:, ROUTE_W + k:ROUTE_W + k + 1] * g_scr[k]
    o_ref[...] = out


def moe_combine(slots, x1, route, ys, *, tm):
    n, d = x1.shape
    return pl.pallas_call(
        _combine_kernel,
        grid_spec=pltpu.PrefetchScalarGridSpec(
            num_scalar_prefetch=1,
            grid=(n // tm,),
            in_specs=[pl.BlockSpec((tm, d), lambda i, sl: (i, 0)),
                      pl.BlockSpec((tm, route.shape[1]), lambda i, sl: (i, 0)),
                      pl.BlockSpec(memory_space=pl.ANY)],
            out_specs=pl.BlockSpec((tm, d), lambda i, sl: (i, 0)),
            scratch_shapes=[pltpu.VMEM((TOP_K, tm, d), F32), pltpu.SemaphoreType.DMA(())],
        ),
        out_shape=jax.ShapeDtypeStruct((n, d), F32),
        compiler_params=_cparams(("arbitrary",)),
        name="moe_combine",
    )(slots, x1, route, ys)


def moe_routed(x1, h2, w_router, wg, wu, wd, *, tm):
    n, _ = x1.shape
    n_e = wg.shape[0]
    route, cnt = router(h2, w_router, tm=tm)
    counts = cnt[0].astype(jnp.int32)
    padded = (counts + MOE_TILE - 1) // MOE_TILE * MOE_TILE
    ends = jnp.cumsum(padded)
    starts = ends - padded
    experts = route[:, ROUTE_E:ROUTE_E + TOP_K].astype(jnp.int32)
    ranks = route[:, ROUTE_RANK:ROUTE_RANK + TOP_K].astype(jnp.int32)
    slots = (starts[experts] + ranks).T
    n_tiles = (TOP_K * n + MOE_TILE - 1) // MOE_TILE + n_e
    tile_start = jnp.arange(n_tiles, dtype=jnp.int32) * MOE_TILE
    tile_expert = jnp.minimum(jnp.sum(tile_start[:, None] >= ends[None, :], axis=1), n_e - 1)
    n_valid = (ends[-1] // MOE_TILE).reshape(1)
    rows = min(MOE_ROW_TILE, n)
    xs = moe_dispatch(slots, h2, n_tiles * MOE_TILE, tm=rows)
    ys = moe_experts(tile_expert.astype(jnp.int32), n_valid.astype(jnp.int32), xs, wg, wu, wd)
    return moe_combine(slots, x1, route, ys, tm=rows)


def combine_matrix(route, n_e):
    eid = jnp.arange(n_e, dtype=F32)[None, :]
    comb = jnp.zeros((route.shape[0], n_e), F32)
    for k in range(TOP_K):
        comb = comb + jnp.where(route[:, ROUTE_E + k:ROUTE_E + k + 1] == eid,
                                route[:, ROUTE_W + k:ROUTE_W + k + 1], 0.0)
    return comb


def _moe_kernel(x1_ref, h_ref, c_ref, wg_ref, wu_ref, wd_ref, o_ref, acc_ref):
    e = pl.program_id(1)
    f = pl.program_id(2)

    @pl.when((e == 0) & (f == 0))
    def _():
        acc_ref[...] = x1_ref[...]

    n_e = c_ref.shape[1]
    eid = lax.broadcasted_iota(jnp.int32, c_ref.shape, 1)
    ce = jnp.sum(jnp.where(eid == e, c_ref[...], 0.0), axis=1, keepdims=True)
    h = h_ref[...].astype(BF16)
    a = _silu(_dot(h, wg_ref[0])) * _dot(h, wu_ref[0])
    acc_ref[...] += ce * _dot(a.astype(BF16), wd_ref[0])

    @pl.when((e == pl.num_programs(1) - 1) & (f == pl.num_programs(2) - 1))
    def _():
        o_ref[...] = acc_ref[...]


def moe_dense(x1, h2, comb, wg, wu, wd, *, tm):
    n, d = x1.shape
    n_e, _, d_ff = wg.shape
    tf = _ff_tile(d_ff)
    row = lambda i, e, f: (i, 0)
    return pl.pallas_call(
        _moe_kernel,
        grid=(n // tm, n_e, d_ff // tf),
        in_specs=[pl.BlockSpec((tm, d), row), pl.BlockSpec((tm, d), row),
                  pl.BlockSpec((tm, n_e), row),
                  pl.BlockSpec((1, d, tf), lambda i, e, f: (e, 0, f)),
                  pl.BlockSpec((1, d, tf), lambda i, e, f: (e, 0, f)),
                  pl.BlockSpec((1, tf, d), lambda i, e, f: (e, f, 0))],
        out_specs=pl.BlockSpec((tm, d), row),
        out_shape=jax.ShapeDtypeStruct((n, d), F32),
        scratch_shapes=[pltpu.VMEM((tm, d), F32)],
        compiler_params=_cparams(("parallel", "arbitrary", "arbitrary")),
        name="moe_dense",
    )(x1, h2, comb, wg, wu, wd)


def _rope_tables(pos):
    half = HEAD_DIM // 2
    inv = ROPE_BASE ** (-np.arange(half, dtype=np.float64) / half)
    ang = pos.astype(np.float64)[:, None] * inv[None, :]
    cos, sin = np.cos(ang), np.sin(ang)
    reps = LANES // HEAD_DIM
    cos_t = np.tile(np.concatenate([cos, cos], axis=1), (1, reps))
    sin_t = np.tile(np.concatenate([-sin, sin], axis=1), (1, reps))
    return jnp.asarray(cos_t.astype(np.float32)), jnp.asarray(sin_t.astype(np.float32))


def _token_tile(n):
    for tm in (512, 256, 128, 64, 32, 16, 8):
        if n % tm == 0:
            return tm
    return n


def kernel(x_prompt, x_sample, cache_k, cache_v, state_ret, page_table, g_mix, w_in, ret_gn,
           q_norm_g, k_norm_g, w_out, g_ffn, w_ffn_gate, w_ffn_up, w_ffn_down, w_router,
           w_exp_gate, w_exp_up, w_exp_down):
    b, t, d = x_prompt.shape
    db, ds, _ = x_sample.shape
    depth = w_in.shape[0]
    n_phys = cache_k.shape[1]
    n_p, n_s = b * t, db * ds
    tm_p, tm_s = _token_tile(n_p), _token_tile(n_s)
    past_len = page_table.shape[1] * PAGE_SIZE

    cos_p, sin_p = _rope_tables(np.arange(t))
    cos_s, sin_s = _rope_tables(np.broadcast_to(past_len + np.arange(ds)[None, :],
                                                (db, ds)).reshape(n_s))
    head_mean = (jnp.kron(jnp.eye(H_MOBA, dtype=F32), jnp.ones((HEAD_DIM, HEAD_DIM), F32))
                 / HEAD_DIM).astype(BF16)
    cache_kt = cache_k.transpose(0, 1, 3, 4, 2)
    cache_vt = cache_v.transpose(0, 1, 3, 4, 2)

    xp = x_prompt.reshape(n_p, d)
    xs = x_sample.reshape(n_s, d)
    sp_l, ss_l, ks_l, vs_l = [], [], [], []
    kv_t = None
    for l in range(depth):
        w_in_b = w_in[l].astype(BF16)
        w_out_b = w_out[l].astype(BF16)
        g1 = g_mix[l].reshape(1, d)
        g2 = g_ffn[l].reshape(1, d)
        qg = jnp.tile(q_norm_g[l], H_MOBA).reshape(1, W_MOBA)
        kg = jnp.tile(k_norm_g[l], H_MOBA).reshape(1, W_MOBA)
        gn = ret_gn[l].reshape(1, W_RET)

        (qr, kr, vr, gr, qm, kmb, kmean, kt_all, vt_all) = proj_in(
            xp, g1, w_in_b, cos_p, sin_p, qg, kg, head_mean, tm=tm_p,
            prompt=(l, depth, b, kv_t))
        kv_t = (kt_all, vt_all)
        seq = lambda a: a.reshape(b, t, a.shape[-1])
        o_r, s_fin = retention_prompt(seq(qr), seq(kr), seq(vr), seq(gr), gn)
        o_m = moba_prompt(seq(qm), seq(kmb), vt_all,
                          kmean.reshape(b, t // MOBA_BLOCK, W_MOBA), l)
        x1p, h2p = out_proj(o_r.reshape(n_p, W_RET), o_m.reshape(n_p, W_MOBA), xp, w_out_b, g2,
                            tm=tm_p)
        sp_l.append(s_fin)

        (qr, kr, vr, gr, qm, km, vm) = proj_in(
            xs, g1, w_in_b, cos_s, sin_s, qg, kg, head_mean, tm=tm_s)
        f32 = lambda a: a.astype(F32)
        o_r, s_new = retention_sample(f32(qr), f32(kr), f32(vr), f32(gr), gn, state_ret[l])
        heads = lambda a: a.reshape(n_s, H_MOBA, HEAD_DIM)
        o_m = moba_sample(heads(qm), heads(km), heads(vm), cache_kt, cache_vt, page_table,
                          l).reshape(n_s, W_MOBA)
        x1s, h2s = out_proj(o_r.astype(BF16), o_m.astype(BF16), xs, w_out_b, g2, tm=tm_s)
        ss_l.append(s_new)
        ks_l.append(km.reshape(db, ds, H_MOBA, HEAD_DIM))
        vs_l.append(vm.reshape(db, ds, H_MOBA, HEAD_DIM))

        i = l // 2
        if l % 2 == 0:
            wg, wu, wd = (w_ffn_gate[i].astype(BF16), w_ffn_up[i].astype(BF16),
                          w_ffn_down[i].astype(BF16))
            xp = ffn_dense(x1p, h2p, wg, wu, wd, tm=tm_p)
            xs = ffn_dense(x1s, h2s, wg, wu, wd, tm=tm_s)
        else:
            wg, wu, wd = (w_exp_gate[i].astype(BF16), w_exp_up[i].astype(BF16),
                          w_exp_down[i].astype(BF16))
            xp = moe_routed(x1p, h2p, w_router[i], wg, wu, wd, tm=tm_p)
            route_s, _ = router(h2s, w_router[i], tm=tm_s)
            xs = moe_dense(x1s, h2s, combine_matrix(route_s, wg.shape[0]), wg, wu, wd, tm=tm_s)

    token_major = lambda a: a.transpose(0, 1, 4, 2, 3)
    return (xp.reshape(b, t, d), xs.reshape(db, ds, d),
            jnp.stack(sp_l), token_major(kv_t[0]), token_major(kv_t[1]),
            jnp.stack(ss_l), jnp.stack(ks_l), jnp.stack(vs_l))
```

```python
import functools
import math

import jax
import jax.numpy as jnp
import numpy as np
from jax import lax
from jax.experimental import pallas as pl
from jax.experimental.pallas import tpu as pltpu

F32 = jnp.float32
BF16 = jnp.bfloat16

HEAD_DIM = 64
H_RET = 8
H_MOBA = 8
W_RET = H_RET * HEAD_DIM
W_MOBA = H_MOBA * HEAD_DIM
PAGE_SIZE = 128
ROPE_BASE = 10000.0
MOBA_BLOCK = 256
MOBA_TOPK = 3
TOP_K = 2
EPS = 1e-6
RET_CHUNK = 256
MOBA_KEY_CHUNK = 256
MOBA_UNROLL = 8
SUM_ROWS = 16
FF_TILE_MAX = 1792
MOE_TILE = 512
MOE_ROW_TILE = 512
ROW_DMA_UNROLL = 8
LANES = 128
NEG = -1e30
VMEM_LIMIT = 56 * 1024 * 1024

HIGHEST = lax.Precision.HIGHEST


def _cparams(sem):
    return pltpu.CompilerParams(dimension_semantics=sem, vmem_limit_bytes=VMEM_LIMIT)


def _silu(x):
    return x / (1.0 + jnp.exp(-x))


def _dot(a, b):
    return jnp.dot(a, b, preferred_element_type=F32)


def _dot_nt(a, b, precision=None):
    return lax.dot_general(a, b, (((1,), (1,)), ((), ())), precision=precision,
                           preferred_element_type=F32)


def _dot_tn(a, b):
    return lax.dot_general(a, b, (((0,), (0,)), ((), ())), preferred_element_type=F32)


def _split3(x):
    hi = x.astype(BF16)
    r = x - hi.astype(F32)
    mid = r.astype(BF16)
    lo = (r - mid.astype(F32)).astype(BF16)
    return hi, mid, lo


def _proj_in_kernel(x_ref, g_ref, w_ref, cos_ref, sin_ref, qg_ref, kg_ref, hm_ref, *refs,
                    prompt, first_layer):
    if prompt and not first_layer:
        refs = refs[2:]
    qr_ref, kr_ref, vr_ref, gr_ref, qm_ref = refs[:5]
    x = x_ref[...]
    tm = x.shape[0]
    ms = jnp.mean(x * x, axis=-1, keepdims=True)
    h = (x * lax.rsqrt(ms + EPS) * g_ref[...]).astype(BF16)

    def proj(c0, width):
        return _dot(h, w_ref[:, c0:c0 + width])

    cos = cos_ref[...]
    sin = sin_ref[...]
    lane = lax.broadcasted_iota(jnp.int32, (tm, LANES), 1)
    first_half = (lane % HEAD_DIM) < (HEAD_DIM // 2)

    def rope(z):
        outs = []
        for c in range(z.shape[1] // LANES):
            zc = z[:, c * LANES:(c + 1) * LANES]
            rot = jnp.where(first_half,
                            pltpu.roll(zc, LANES - HEAD_DIM // 2, 1),
                            pltpu.roll(zc, HEAD_DIM // 2, 1))
            outs.append(zc * cos + rot * sin)
        return jnp.concatenate(outs, axis=1)

    hm = hm_ref[...]

    def head_norm(z, g):
        sq = z * z
        hi = sq.astype(BF16)
        lo = (sq - hi.astype(F32)).astype(BF16)
        msq = _dot(hi, hm) + _dot(lo, hm)
        return z * lax.rsqrt(msq + EPS) * g

    qr_ref[...] = rope(proj(0, W_RET)).astype(BF16)
    kr_ref[...] = (rope(proj(W_RET, W_RET)) * (HEAD_DIM ** -0.5)).astype(BF16)
    vr_ref[...] = proj(2 * W_RET, W_RET).astype(BF16)
    gr_ref[...] = proj(3 * W_RET, W_RET).astype(BF16)
    base = 4 * W_RET
    qm_ref[...] = head_norm(proj(base, W_MOBA), qg_ref[...])
    km = head_norm(proj(base + W_MOBA, W_MOBA), kg_ref[...])
    vm = proj(base + 2 * W_MOBA, W_MOBA)
    if not prompt:
        km_ref, vm_ref = refs[5:]
        km_ref[...] = km
        vm_ref[...] = vm
        return
    kmb_ref, kmean_ref, kt_ref, vt_ref = refs[5:]
    kmb_ref[...] = km.astype(BF16)
    for r in range(tm // MOBA_BLOCK):
        kmean_ref[0, r:r + 1, :] = jnp.mean(
            km[r * MOBA_BLOCK:(r + 1) * MOBA_BLOCK], axis=0, keepdims=True)
    for src, dst in ((km, kt_ref), (vm, vt_ref)):
        if first_layer:
            if dst.shape[0] > 1:
                dst[1:] = jnp.zeros((dst.shape[0] - 1,) + dst.shape[1:], F32)
            dst = dst.at[0]
        src_t = src.T
        for hd in range(H_MOBA):
            dst[hd] = src_t[hd * HEAD_DIM:(hd + 1) * HEAD_DIM]


def proj_in(x, g, w_bf, cos, sin, qg, kg, hm, *, tm, prompt=None):
    n, d = x.shape
    n_pos_tiles = cos.shape[0] // tm
    row = lambda i: (i, 0)
    fixed = lambda i: (0, 0)
    wide = lambda dt: jax.ShapeDtypeStruct((n, W_RET), dt)
    out_shape = [wide(BF16), wide(BF16), wide(BF16), wide(BF16), wide(F32)]
    out_specs = [pl.BlockSpec((tm, W_RET), row)] * 5
    in_specs = [
        pl.BlockSpec((tm, d), row),
        pl.BlockSpec((1, d), fixed),
        pl.BlockSpec(w_bf.shape, fixed),
        pl.BlockSpec((tm, LANES), lambda i: (i % n_pos_tiles, 0)),
        pl.BlockSpec((tm, LANES), lambda i: (i % n_pos_tiles, 0)),
        pl.BlockSpec((1, W_MOBA), fixed),
        pl.BlockSpec((1, W_MOBA), fixed),
        pl.BlockSpec((W_MOBA, W_MOBA), fixed),
    ]
    args = [x, g, w_bf, cos, sin, qg, kg, hm]
    aliases = {}
    first_layer = True
    if prompt is None:
        out_shape += [wide(F32), wide(F32)]
        out_specs += [pl.BlockSpec((tm, W_RET), row)] * 2
    else:
        layer, depth, batch, prev = prompt
        first_layer = prev is None
        tiles_per_seq = n // batch // tm
        nb = tm // MOBA_BLOCK
        kv_shape = jax.ShapeDtypeStruct((depth, batch, H_MOBA, HEAD_DIM, n // batch), F32)
        if first_layer:
            kv_spec = pl.BlockSpec((depth, None, H_MOBA, HEAD_DIM, tm),
                                   lambda i: (0, i // tiles_per_seq, 0, 0, i % tiles_per_seq))
        else:
            kv_spec = pl.BlockSpec((None, None, H_MOBA, HEAD_DIM, tm),
                                   lambda i: (layer, i // tiles_per_seq, 0, 0, i % tiles_per_seq))
            aliases = {len(args): len(out_shape) + 2, len(args) + 1: len(out_shape) + 3}
            in_specs += [pl.BlockSpec(memory_space=pl.ANY)] * 2
            args += list(prev)
        out_shape += [wide(BF16), jax.ShapeDtypeStruct((n // tm, nb, W_MOBA), F32),
                      kv_shape, kv_shape]
        out_specs += [pl.BlockSpec((tm, W_RET), row),
                      pl.BlockSpec((1, nb, W_MOBA), lambda i: (i, 0, 0)), kv_spec, kv_spec]
    return pl.pallas_call(
        functools.partial(_proj_in_kernel, prompt=prompt is not None, first_layer=first_layer),
        grid=(n // tm,),
        in_specs=in_specs,
        out_specs=out_specs,
        out_shape=out_shape,
        input_output_aliases=aliases,
        compiler_params=_cparams(("parallel",)),
        name="proj_in",
    )(*args)


def _ret_prompt_kernel(q_ref, k_ref, v_ref, g_ref, gn_ref, dm_ref, cross_ref, kdec_ref,
                       gc_ref, o_ref, sfin_ref, s_scr):
    c = pl.program_id(1)

    @pl.when(c == 0)
    def _():
        s_scr[...] = jnp.zeros_like(s_scr)

    heads = range(H_RET)
    sls = [slice(h * HEAD_DIM, (h + 1) * HEAD_DIM) for h in heads]
    qs = [q_ref[0, :, sl] for sl in sls]
    ks = [k_ref[0, :, sl] for sl in sls]
    vs = [v_ref[0, :, sl] for sl in sls]
    s0s = [s_scr[h] for h in heads]
    scores = [_dot_nt(qs[h], ks[h]) for h in heads]
    cross = [_dot(qs[h], s0s[h].astype(BF16)) for h in heads]
    kds = [(ks[h].astype(F32) * kdec_ref[h]).astype(BF16) for h in heads]
    kvs = [_dot_tn(kds[h], vs[h]) for h in heads]
    inner = [_dot((scores[h] * dm_ref[h]).astype(BF16), vs[h]) for h in heads]
    outs = []
    for h in heads:
        s_scr[h] = gc_ref[h] * s0s[h] + kvs[h]
        o = inner[h] + cross[h] * cross_ref[h]
        mu = jnp.mean(o, axis=-1, keepdims=True)
        oc = o - mu
        var = jnp.mean(oc * oc, axis=-1, keepdims=True)
        on = oc * lax.rsqrt(var + EPS) * gn_ref[:, sls[h]]
        outs.append(on * _silu(g_ref[0, :, sls[h]].astype(F32)))
    o_ref[0] = jnp.concatenate(outs, axis=1).astype(BF16)

    @pl.when(c == pl.num_programs(1) - 1)
    def _():
        sfin_ref[0] = s_scr[...]


def _ret_log_decay():
    return np.log1p(-np.exp2(-5.0 - np.arange(H_RET, dtype=np.float64)))


def retention_prompt(qr, kr, vr, gr, gn):
    b, t, w = qr.shape
    c = min(RET_CHUNK, t)
    ld = _ret_log_decay()
    i = np.arange(c, dtype=np.float64)
    diff = i[:, None] - i[None, :]
    f32c = lambda a: jnp.asarray(np.ascontiguousarray(a, dtype=np.float32))
    dmask = f32c(np.where(diff >= 0, np.exp(ld[:, None, None] * np.maximum(diff, 0.0)), 0.0))
    bc = lambda a: f32c(np.broadcast_to(a[:, :, None], (H_RET, c, HEAD_DIM)))
    cross = bc(np.exp(ld[:, None] * (i + 1.0)[None, :]))
    kdec = bc(np.exp(ld[:, None] * (c - 1.0 - i)[None, :]))
    gc = f32c(np.broadcast_to(np.exp(ld * c)[:, None, None], (H_RET, HEAD_DIM, HEAD_DIM)))
    tile = pl.BlockSpec((1, c, w), lambda bi, ci: (bi, ci, 0))
    const3 = lambda shape: pl.BlockSpec(shape, lambda bi, ci: (0, 0, 0))
    return pl.pallas_call(
        _ret_prompt_kernel,
        grid=(b, t // c),
        in_specs=[tile, tile, tile, tile,
                  pl.BlockSpec((1, w), lambda bi, ci: (0, 0)),
                  const3((H_RET, c, c)), const3((H_RET, c, HEAD_DIM)),
                  const3((H_RET, c, HEAD_DIM)), const3((H_RET, HEAD_DIM, HEAD_DIM))],
        out_specs=[tile,
                   pl.BlockSpec((1, H_RET, HEAD_DIM, HEAD_DIM), lambda bi, ci: (bi, 0, 0, 0))],
        out_shape=[jax.ShapeDtypeStruct((b, t, w), BF16),
                   jax.ShapeDtypeStruct((b, H_RET, HEAD_DIM, HEAD_DIM), F32)],
        scratch_shapes=[pltpu.VMEM((H_RET, HEAD_DIM, HEAD_DIM), F32)],
        compiler_params=_cparams(("parallel", "arbitrary")),
        name="retention_prompt",
    )(qr, kr, vr, gr, gn, dmask, cross, kdec, gc)


def _moba_prompt_kernel(q_ref, k_ref, v_ref, km_ref, o_ref, vt_scr, sel_scr, m_scr, acc_scr):
    qi = pl.program_id(2)
    nb = km_ref.shape[1]
    bs = MOBA_BLOCK
    kc = MOBA_KEY_CHUNK
    n_chunks = bs // kc
    n_heads = LANES // HEAD_DIM

    @pl.when(qi == 0)
    def _():
        for j in range(nb):
            for hh in range(n_heads):
                vt_scr[j, hh, :HEAD_DIM] = v_ref[hh, :, j * bs:(j + 1) * bs].astype(BF16)
                vt_scr[j, hh, HEAD_DIM:] = jnp.ones((SUM_ROWS, bs), BF16)

    qt = (q_ref[0] * (HEAD_DIM ** -0.5 * math.log2(math.e))).T
    drow = lax.broadcasted_iota(jnp.int32, qt.shape, 0)
    blk = lax.broadcasted_iota(jnp.int32, (nb, bs), 0)
    kmean = km_ref[0]
    qts = []
    for hh in range(n_heads):
        qh = jnp.where(drow // HEAD_DIM == hh, qt, 0.0)
        qts.append(qh.astype(BF16))
        gate = jnp.dot(kmean, qh, precision=HIGHEST, preferred_element_type=F32)
        g = jnp.where(blk < qi, gate, -jnp.inf)
        sel = jnp.zeros((nb, bs), F32)
        for _ in range(min(MOBA_TOPK, nb)):
            mx = jnp.max(g, axis=0, keepdims=True)
            idx = jnp.min(jnp.where(g == mx, blk, nb), axis=0, keepdims=True)
            pick = blk == idx
            sel = jnp.where(pick, 1.0, sel)
            g = jnp.where(pick, -jnp.inf, g)
        sel_scr[hh] = jnp.where(blk < qi, sel, 0.0)

    m_scr[...] = jnp.full(m_scr.shape, NEG, F32)
    acc_scr[...] = jnp.zeros(acc_scr.shape, F32)

    def merge(hh, mb, pv, sel_row=None):
        m = m_scr[hh]
        m_new = jnp.maximum(m, mb)
        wb = jnp.exp2(mb - m_new)
        if sel_row is not None:
            m_new = jnp.where(sel_row, m_new, m)
            wb = jnp.where(sel_row, wb, 0.0)
        wa = jnp.exp2(m - m_new)
        m_scr[hh] = m_new
        acc_scr[hh] = wa * acc_scr[hh] + wb * pv

    krow = lax.broadcasted_iota(jnp.int32, (kc, bs), 0)
    qcol = lax.broadcasted_iota(jnp.int32, (kc, bs), 1)

    def blocks(j0, n_blocks, own):
        units = [(u, c, hh) for u in range(n_blocks) for c in range(n_chunks)
                 for hh in range(n_heads)]
        k_rows = k_ref[0, pl.ds(pl.multiple_of(j0 * bs, bs), n_blocks * bs), :]
        wide = [_dot(k_rows, qts[hh]) for hh in range(n_heads)]
        parts = []
        for u, c, hh in units:
            s = wide[hh][u * bs + c * kc:u * bs + (c + 1) * kc]
            if own:
                s = jnp.where(krow + c * kc <= qcol, s, NEG)
            mb = jnp.max(s, axis=0, keepdims=True)
            parts.append((mb, jnp.exp2(s - mb).astype(BF16)))
        pvs = [_dot(vt_scr[j0 + u, hh, :, c * kc:(c + 1) * kc], parts[i][1])
               for i, (u, c, hh) in enumerate(units)]
        for i, (u, c, hh) in enumerate(units):
            sel_row = None if own else sel_scr[hh, pl.ds(j0 + u, 1), :] > 0.0
            merge(hh, parts[i][0], pvs[i], sel_row)

    def group_body(g, c_):
        blocks(g * MOBA_UNROLL, MOBA_UNROLL, False)
        return c_

    n_groups = qi // MOBA_UNROLL
    lax.fori_loop(0, n_groups, group_body, 0)
    done = n_groups * MOBA_UNROLL
    piece = MOBA_UNROLL // 2
    while piece >= 1:
        @pl.when((qi & piece) != 0)
        def _(done=done, piece=piece):
            blocks(done, piece, False)
        done = done + (qi & piece)
        piece //= 2
    blocks(qi, 1, True)
    outs = [acc_scr[hh, :HEAD_DIM] / acc_scr[hh, HEAD_DIM:HEAD_DIM + 1]
            for hh in range(n_heads)]
    o_ref[0] = jnp.concatenate(outs, axis=0).T.astype(BF16)


def moba_prompt(qm, kmb, vt_all, kmean, layer):
    b, t, w = qm.shape
    nb = t // MOBA_BLOCK
    n_heads = LANES // HEAD_DIM
    return pl.pallas_call(
        _moba_prompt_kernel,
        grid=(b, w // LANES, nb),
        in_specs=[
            pl.BlockSpec((1, MOBA_BLOCK, LANES), lambda bi, hp, qi: (bi, qi, hp)),
            pl.BlockSpec((1, t, LANES), lambda bi, hp, qi: (bi, 0, hp)),
            pl.BlockSpec((None, None, n_heads, HEAD_DIM, t),
                         lambda bi, hp, qi: (layer, bi, hp, 0, 0)),
            pl.BlockSpec((1, nb, LANES), lambda bi, hp, qi: (bi, 0, hp)),
        ],
        out_specs=pl.BlockSpec((1, MOBA_BLOCK, LANES), lambda bi, hp, qi: (bi, qi, hp)),
        out_shape=jax.ShapeDtypeStruct((b, t, w), BF16),
        scratch_shapes=[pltpu.VMEM((nb, n_heads, HEAD_DIM + SUM_ROWS, MOBA_BLOCK), BF16),
                        pltpu.VMEM((n_heads, nb, MOBA_BLOCK), F32),
                        pltpu.VMEM((n_heads, 1, MOBA_BLOCK), F32),
                        pltpu.VMEM((n_heads, HEAD_DIM + SUM_ROWS, MOBA_BLOCK), F32)],
        compiler_params=_cparams(("parallel", "parallel", "arbitrary")),
        name="moba_prompt",
    )(qm, kmb, vt_all, kmean)


def _ret_sample_kernel(q_ref, k_ref, v_ref, g_ref, gam_ref, gn_ref, s0_ref, e1_ref, e2_ref,
                       o_ref, s_ref):
    e1 = e1_ref[...]
    e2 = e2_ref[...]

    def expand(x, e):
        hi, mid, lo = _split3(x)
        return _dot(hi, e) + _dot(mid, e) + _dot(lo, e)

    kexp = expand(k_ref[...], e1)
    vexp = expand(v_ref[...], e2)
    qexp = expand(q_ref[...], e1)
    s_new = gam_ref[:, :1] * s0_ref[...] + kexp * vexp
    s_ref[...] = s_new
    hi, mid, lo = _split3(qexp * s_new)
    o = _dot_nt(hi, e2) + _dot_nt(mid, e2) + _dot_nt(lo, e2)
    mu = jnp.mean(o, axis=-1, keepdims=True)
    oc = o - mu
    var = jnp.mean(oc * oc, axis=-1, keepdims=True)
    o_ref[...] = oc * lax.rsqrt(var + EPS) * gn_ref[...] * _silu(g_ref[...])


def retention_sample(q, k, v, gr, gn, s0):
    n = q.shape[0]
    r = n * H_RET
    dd = HEAD_DIM * HEAD_DIM
    rows = lambda a: a.reshape(r, HEAD_DIM)
    gam = jnp.asarray(np.tile(np.broadcast_to(np.exp(_ret_log_decay())[:, None],
                                              (H_RET, HEAD_DIM)), (n, 1)).astype(np.float32))
    gnr = jnp.tile(gn.reshape(H_RET, HEAD_DIM), (n, 1))
    lane = np.arange(dd)
    e1 = jnp.asarray(lane[None, :] // HEAD_DIM == np.arange(HEAD_DIM)[:, None], BF16)
    e2 = jnp.asarray(lane[None, :] % HEAD_DIM == np.arange(HEAD_DIM)[:, None], BF16)
    tr = min(128, r)
    small = pl.BlockSpec((tr, HEAD_DIM), lambda i: (i, 0))
    big = pl.BlockSpec((tr, dd), lambda i: (i, 0))
    emat = pl.BlockSpec((HEAD_DIM, dd), lambda i: (0, 0))
    o, s_new = pl.pallas_call(
        _ret_sample_kernel,
        grid=(r // tr,),
        in_specs=[small, small, small, small, small, small, big, emat, emat],
        out_specs=[small, big],
        out_shape=[jax.ShapeDtypeStruct((r, HEAD_DIM), F32),
                   jax.ShapeDtypeStruct((r, dd), F32)],
        compiler_params=_cparams(("parallel",)),
        name="retention_sample",
    )(rows(q), rows(k), rows(v), rows(gr), gam, gnr, s0.reshape(r, dd), e1, e2)
    return o.reshape(n, W_RET), s_new.reshape(n, H_RET, HEAD_DIM, HEAD_DIM)


def _moba_sample_kernel(pt_ref, q_ref, qt_ref, kn_ref, vn_ref, *refs, n_pages):
    del pt_ref
    k_refs = refs[:n_pages]
    v_refs = refs[n_pages:2 * n_pages]
    o_ref = refs[2 * n_pages]
    ppb = MOBA_BLOCK // PAGE_SIZE
    nbp = n_pages // ppb
    qscale = HEAD_DIM ** -0.5 * math.log2(math.e)
    qt = qt_ref[0] * qscale
    qcols = [jnp.broadcast_to(qt[:, h:h + 1], (HEAD_DIM, PAGE_SIZE)) for h in range(H_MOBA)]
    s_pages = []
    for p in range(n_pages):
        rows = [jnp.sum(k_refs[p][h] * qcols[h], axis=0, keepdims=True) for h in range(H_MOBA)]
        s_pages.append(jnp.concatenate(rows, axis=0))
    gates, maxes = [], []
    for j in range(nbp):
        tot = s_pages[j * ppb]
        top = s_pages[j * ppb]
        for p in range(1, ppb):
            tot = tot + s_pages[j * ppb + p]
            top = jnp.maximum(top, s_pages[j * ppb + p])
        gates.append(jnp.sum(tot, axis=1, keepdims=True))
        maxes.append(jnp.max(top, axis=1, keepdims=True))
    k_sel = min(MOBA_TOPK, nbp + 1)
    sels = []
    for j in range(nbp):
        rank = jnp.zeros(gates[j].shape, F32)
        for i in range(nbp):
            if i == j:
                continue
            ahead = (gates[i] > gates[j]) | ((gates[i] == gates[j]) & (i < j))
            rank = rank + jnp.where(ahead, 1.0, 0.0)
        sels.append(rank < k_sel)
    s_own = jnp.sum(q_ref[0] * kn_ref[0], axis=1, keepdims=True) * qscale
    m = s_own
    for j in range(nbp):
        m = jnp.maximum(m, jnp.where(sels[j], maxes[j], NEG))
    p_own = jnp.exp2(s_own - m)
    l = p_own
    accs = [jnp.zeros((HEAD_DIM, PAGE_SIZE), F32) for _ in range(H_MOBA)]
    for p in range(n_pages):
        shift = jnp.where(sels[p // ppb], m, -NEG)
        pp = jnp.exp2(s_pages[p] - shift)
        l = l + jnp.sum(pp, axis=1, keepdims=True)
        for h in range(H_MOBA):
            accs[h] = accs[h] + pp[h:h + 1, :] * v_refs[p][h]
    o_past = jnp.concatenate([jnp.sum(a.T, axis=0, keepdims=True) for a in accs], axis=0)
    o_ref[0] = (o_past + p_own * vn_ref[0]) / l


def moba_sample(q, k_new, v_new, cache_k, cache_v, page_table, layer):
    n = q.shape[0]
    n_pages = page_table.shape[1]
    vec = pl.BlockSpec((1, H_MOBA, HEAD_DIM), lambda s, pt: (s, 0, 0))
    vec_t = pl.BlockSpec((1, HEAD_DIM, H_MOBA), lambda s, pt: (s, 0, 0))

    def page_spec(p):
        return pl.BlockSpec((None, None, H_MOBA, HEAD_DIM, PAGE_SIZE),
                            lambda s, pt: (layer, pt[s, p], 0, 0, 0))

    pages = [page_spec(p) for p in range(n_pages)]
    return pl.pallas_call(
        functools.partial(_moba_sample_kernel, n_pages=n_pages),
        grid_spec=pltpu.PrefetchScalarGridSpec(
            num_scalar_prefetch=1,
            grid=(n,),
            in_specs=[vec, vec_t, vec, vec] + pages + pages,
            out_specs=vec,
        ),
        out_shape=jax.ShapeDtypeStruct((n, H_MOBA, HEAD_DIM), F32),
        compiler_params=_cparams(("arbitrary",)),
        name="moba_sample",
    )(page_table, q, q.transpose(0, 2, 1), k_new, v_new,
      *([cache_k] * n_pages), *([cache_v] * n_pages))


def _out_proj_kernel(or_ref, om_ref, x_ref, w_ref, g_ref, x1_ref, h2_ref):
    x1 = (x_ref[...] + _dot(or_ref[...], w_ref[:W_RET, :])
          + _dot(om_ref[...], w_ref[W_RET:, :]))
    x1_ref[...] = x1
    ms = jnp.mean(x1 * x1, axis=-1, keepdims=True)
    h2_ref[...] = x1 * lax.rsqrt(ms + EPS) * g_ref[...]


def out_proj(o_r, o_m, x, w_bf, g, *, tm):
    n, d = x.shape
    row = lambda i: (i, 0)
    fixed = lambda i: (0, 0)
    return pl.pallas_call(
        _out_proj_kernel,
        grid=(n // tm,),
        in_specs=[pl.BlockSpec((tm, W_RET), row), pl.BlockSpec((tm, W_MOBA), row),
                  pl.BlockSpec((tm, d), row), pl.BlockSpec(w_bf.shape, fixed),
                  pl.BlockSpec((1, d), fixed)],
        out_specs=[pl.BlockSpec((tm, d), row), pl.BlockSpec((tm, d), row)],
        out_shape=[jax.ShapeDtypeStruct((n, d), F32), jax.ShapeDtypeStruct((n, d), F32)],
        compiler_params=_cparams(("parallel",)),
        name="out_proj",
    )(o_r, o_m, x, w_bf, g)


def _ffn_kernel(x1_ref, h_ref, wg_ref, wu_ref, wd_ref, o_ref, acc_ref):
    f = pl.program_id(1)

    @pl.when(f == 0)
    def _():
        acc_ref[...] = x1_ref[...]

    h = h_ref[...].astype(BF16)
    a = _silu(_dot(h, wg_ref[...])) * _dot(h, wu_ref[...])
    acc_ref[...] += _dot(a.astype(BF16), wd_ref[...])

    @pl.when(f == pl.num_programs(1) - 1)
    def _():
        o_ref[...] = acc_ref[...]


def _ff_tile(d_ff):
    for tf in range(FF_TILE_MAX, 0, -LANES):
        if d_ff % tf == 0:
            return tf
    return d_ff


def ffn_dense(x1, h2, wg, wu, wd, *, tm):
    n, d = x1.shape
    d_ff = wg.shape[1]
    tf = _ff_tile(d_ff)
    row = lambda i, f: (i, 0)
    return pl.pallas_call(
        _ffn_kernel,
        grid=(n // tm, d_ff // tf),
        in_specs=[pl.BlockSpec((tm, d), row), pl.BlockSpec((tm, d), row),
                  pl.BlockSpec((d, tf), lambda i, f: (0, f)),
                  pl.BlockSpec((d, tf), lambda i, f: (0, f)),
                  pl.BlockSpec((tf, d), lambda i, f: (f, 0))],
        out_specs=pl.BlockSpec((tm, d), row),
        out_shape=jax.ShapeDtypeStruct((n, d), F32),
        scratch_shapes=[pltpu.VMEM((tm, d), F32)],
        compiler_params=_cparams(("parallel", "arbitrary")),
        name="ffn_dense",
    )(x1, h2, wg, wu, wd)


ROUTE_E, ROUTE_RANK, ROUTE_W = 0, 2, 4


def _router_kernel(h_ref, w_ref, route_ref, cnt_ref, carry_scr):
    @pl.when(pl.program_id(0) == 0)
    def _():
        carry_scr[...] = jnp.zeros_like(carry_scr)

    logits = jnp.dot(h_ref[...], w_ref[...], precision=HIGHEST, preferred_element_type=F32)
    tm, n_e = logits.shape
    eid = lax.broadcasted_iota(jnp.int32, logits.shape, 1)
    v1 = jnp.max(logits, axis=1, keepdims=True)
    i1 = jnp.min(jnp.where(logits == v1, eid, n_e), axis=1, keepdims=True)
    rest = jnp.where(eid == i1, -jnp.inf, logits)
    v2 = jnp.max(rest, axis=1, keepdims=True)
    i2 = jnp.min(jnp.where(rest == v2, eid, n_e), axis=1, keepdims=True)
    e2 = jnp.exp(v2 - v1)
    w1 = 1.0 / (1.0 + e2)
    w2 = e2 / (1.0 + e2)
    onehot = jnp.where((eid == i1) | (eid == i2), 1.0, 0.0)
    r = lax.broadcasted_iota(jnp.int32, (tm, tm), 0)
    c = lax.broadcasted_iota(jnp.int32, (tm, tm), 1)
    earlier = jnp.where(c < r, 1.0, 0.0).astype(BF16)
    ranks = _dot(earlier, onehot.astype(BF16)) + carry_scr[...]
    r1 = jnp.sum(jnp.where(eid == i1, ranks, 0.0), axis=1, keepdims=True)
    r2 = jnp.sum(jnp.where(eid == i2, ranks, 0.0), axis=1, keepdims=True)
    total = carry_scr[...] + jnp.sum(onehot, axis=0, keepdims=True)
    carry_scr[...] = total
    cnt_ref[...] = total
    route = jnp.zeros(logits.shape, F32)
    for k, col in enumerate((i1.astype(F32), i2.astype(F32), r1, r2, w1, w2)):
        route = jnp.where(eid == k, col, route)
    route_ref[...] = route


def router(h2, w_router, *, tm):
    n, d = h2.shape
    n_e = w_router.shape[1]
    return pl.pallas_call(
        _router_kernel,
        grid=(n // tm,),
        in_specs=[pl.BlockSpec((tm, d), lambda i: (i, 0)),
                  pl.BlockSpec((d, n_e), lambda i: (0, 0))],
        out_specs=[pl.BlockSpec((tm, n_e), lambda i: (i, 0)),
                   pl.BlockSpec((1, n_e), lambda i: (0, 0))],
        out_shape=[jax.ShapeDtypeStruct((n, n_e), F32), jax.ShapeDtypeStruct((1, n_e), F32)],
        scratch_shapes=[pltpu.VMEM((1, n_e), F32)],
        compiler_params=_cparams(("arbitrary",)),
        name="router",
    )(h2, w_router)


def _dispatch_kernel(slots_ref, h_ref, xs_in_ref, xs_ref, sem):
    del xs_in_ref
    tm = h_ref.shape[0]
    base = pl.program_id(0) * tm

    def start(r, c):
        for k in range(TOP_K):
            slot = slots_ref[k, base + r]
            pltpu.make_async_copy(h_ref.at[pl.ds(r, 1)], xs_ref.at[pl.ds(slot, 1)],
                                  sem).start(priority=k)
        return c

    lax.fori_loop(0, tm, start, 0, unroll=ROW_DMA_UNROLL)

    def wait(r, c):
        for k in range(TOP_K):
            pltpu.make_async_copy(h_ref.at[pl.ds(0, 1)], xs_ref.at[pl.ds(0, 1)], sem).wait()
        return c

    lax.fori_loop(0, tm, wait, 0, unroll=ROW_DMA_UNROLL)


def moe_dispatch(slots, h2, n_slots, *, tm):
    n, d = h2.shape
    xs0 = jnp.zeros((n_slots, d), h2.dtype)
    return pl.pallas_call(
        _dispatch_kernel,
        grid_spec=pltpu.PrefetchScalarGridSpec(
            num_scalar_prefetch=1,
            grid=(n // tm,),
            in_specs=[pl.BlockSpec((tm, d), lambda i, sl: (i, 0)),
                      pl.BlockSpec(memory_space=pl.ANY)],
            out_specs=pl.BlockSpec(memory_space=pl.ANY),
            scratch_shapes=[pltpu.SemaphoreType.DMA(())],
        ),
        out_shape=jax.ShapeDtypeStruct((n_slots, d), h2.dtype),
        input_output_aliases={2: 0},
        compiler_params=_cparams(("arbitrary",)),
        name="moe_dispatch",
    )(slots, h2, xs0)


def _experts_kernel(te_ref, nv_ref, xs_ref, wg_ref, wu_ref, wd_ref, ys_ref, acc_ref):
    del te_ref
    s = pl.program_id(0)
    f = pl.program_id(1)

    @pl.when(s < nv_ref[0])
    def _():
        @pl.when(f == 0)
        def _():
            acc_ref[...] = jnp.zeros_like(acc_ref)

        x = xs_ref[...].astype(BF16)
        a = _silu(_dot(x, wg_ref[0])) * _dot(x, wu_ref[0])
        acc_ref[...] += _dot(a.astype(BF16), wd_ref[0])

        @pl.when(f == pl.num_programs(1) - 1)
        def _():
            ys_ref[...] = acc_ref[...]

    @pl.when((s >= nv_ref[0]) & (f == 0))
    def _():
        ys_ref[...] = jnp.zeros_like(ys_ref)


def moe_experts(tile_expert, n_valid, xs, wg, wu, wd):
    n_slots, d = xs.shape
    d_ff = wg.shape[2]
    tf = _ff_tile(d_ff)
    nf = d_ff // tf

    def live(s, nv):
        return jnp.minimum(s, nv[0] - 1)

    def fcol(s, f, nv):
        return jnp.where(s < nv[0], f, nf - 1)

    return pl.pallas_call(
        _experts_kernel,
        grid_spec=pltpu.PrefetchScalarGridSpec(
            num_scalar_prefetch=2,
            grid=(n_slots // MOE_TILE, nf),
            in_specs=[pl.BlockSpec((MOE_TILE, d), lambda s, f, te, nv: (live(s, nv), 0)),
                      pl.BlockSpec((1, d, tf), lambda s, f, te, nv: (te[s], 0, fcol(s, f, nv))),
                      pl.BlockSpec((1, d, tf), lambda s, f, te, nv: (te[s], 0, fcol(s, f, nv))),
                      pl.BlockSpec((1, tf, d), lambda s, f, te, nv: (te[s], fcol(s, f, nv), 0))],
            out_specs=pl.BlockSpec((MOE_TILE, d), lambda s, f, te, nv: (s, 0)),
            scratch_shapes=[pltpu.VMEM((MOE_TILE, d), F32)],
        ),
        out_shape=jax.ShapeDtypeStruct((n_slots, d), F32),
        compiler_params=_cparams(("arbitrary", "arbitrary")),
        name="moe_experts",
    )(tile_expert, n_valid, xs, wg, wu, wd)


def _combine_kernel(slots_ref, x1_ref, route_ref, ys_ref, o_ref, g_scr, sem):
    tm = x1_ref.shape[0]
    base = pl.program_id(0) * tm

    def start(r, c):
        for k in range(TOP_K):
            slot = slots_ref[k, base + r]
            pltpu.make_async_copy(ys_ref.at[pl.ds(slot, 1)], g_scr.at[k, pl.ds(r, 1)],
                                  sem).start(priority=k)
        return c

    lax.fori_loop(0, tm, start, 0, unroll=ROW_DMA_UNROLL)

    def wait(r, c):
        for k in range(TOP_K):
            pltpu.make_async_copy(ys_ref.at[pl.ds(0, 1)], g_scr.at[0, pl.ds(0, 1)], sem).wait()
        return c

    lax.fori_loop(0, tm, wait, 0, unroll=ROW_DMA_UNROLL)
    route = route_ref[...]
    out = x1_ref[...]
    for k in range(TOP_K):
        out = out + route[:, ROUTE_W + k:ROUTE_W + k + 1] * g_scr[k]
    o_ref[...] = out


def moe_combine(slots, x1, route, ys, *, tm):
    n, d = x1.shape
    return pl.pallas_call(
        _combine_kernel,
        grid_spec=pltpu.PrefetchScalarGridSpec(
            num_scalar_prefetch=1,
            grid=(n // tm,),
            in_specs=[pl.BlockSpec((tm, d), lambda i, sl: (i, 0)),
                      pl.BlockSpec((tm, route.shape[1]), lambda i, sl: (i, 0)),
                      pl.BlockSpec(memory_space=pl.ANY)],
            out_specs=pl.BlockSpec((tm, d), lambda i, sl: (i, 0)),
            scratch_shapes=[pltpu.VMEM((TOP_K, tm, d), F32), pltpu.SemaphoreType.DMA(())],
        ),
        out_shape=jax.ShapeDtypeStruct((n, d), F32),
        compiler_params=_cparams(("arbitrary",)),
        name="moe_combine",
    )(slots, x1, route, ys)


def moe_routed(x1, h2, w_router, wg, wu, wd, *, tm):
    n, _ = x1.shape
    n_e = wg.shape[0]
    route, cnt = router(h2, w_router, tm=tm)
    counts = cnt[0].astype(jnp.int32)
    padded = (counts + MOE_TILE - 1) // MOE_TILE * MOE_TILE
    ends = jnp.cumsum(padded)
    starts = ends - padded
    experts = route[:, ROUTE_E:ROUTE_E + TOP_K].astype(jnp.int32)
    ranks = route[:, ROUTE_RANK:ROUTE_RANK + TOP_K].astype(jnp.int32)
    slots = (starts[experts] + ranks).T
    n_tiles = (TOP_K * n + MOE_TILE - 1) // MOE_TILE + n_e
    tile_start = jnp.arange(n_tiles, dtype=jnp.int32) * MOE_TILE
    tile_expert = jnp.minimum(jnp.sum(tile_start[:, None] >= ends[None, :], axis=1), n_e - 1)
    n_valid = (ends[-1] // MOE_TILE).reshape(1)
    rows = min(MOE_ROW_TILE, n)
    xs = moe_dispatch(slots, h2, n_tiles * MOE_TILE, tm=rows)
    ys = moe_experts(tile_expert.astype(jnp.int32), n_valid.astype(jnp.int32), xs, wg, wu, wd)
    return moe_combine(slots, x1, route, ys, tm=rows)


def combine_matrix(route, n_e):
    eid = jnp.arange(n_e, dtype=F32)[None, :]
    comb = jnp.zeros((route.shape[0], n_e), F32)
    for k in range(TOP_K):
        comb = comb + jnp.where(route[:, ROUTE_E + k:ROUTE_E + k + 1] == eid,
                                route[:, ROUTE_W + k:ROUTE_W + k + 1], 0.0)
    return comb


def _moe_kernel(x1_ref, h_ref, c_ref, wg_ref, wu_ref, wd_ref, o_ref, acc_ref):
    e = pl.program_id(1)
    f = pl.program_id(2)

    @pl.when((e == 0) & (f == 0))
    def _():
        acc_ref[...] = x1_ref[...]

    n_e = c_ref.shape[1]
    eid = lax.broadcasted_iota(jnp.int32, c_ref.shape, 1)
    ce = jnp.sum(jnp.where(eid == e, c_ref[...], 0.0), axis=1, keepdims=True)
    h = h_ref[...].astype(BF16)
    a = _silu(_dot(h, wg_ref[0])) * _dot(h, wu_ref[0])
    acc_ref[...] += ce * _dot(a.astype(BF16), wd_ref[0])

    @pl.when((e == pl.num_programs(1) - 1) & (f == pl.num_programs(2) - 1))
    def _():
        o_ref[...] = acc_ref[...]


def moe_dense(x1, h2, comb, wg, wu, wd, *, tm):
    n, d = x1.shape
    n_e, _, d_ff = wg.shape
    tf = _ff_tile(d_ff)
    row = lambda i, e, f: (i, 0)
    return pl.pallas_call(
        _moe_kernel,
        grid=(n // tm, n_e, d_ff // tf),
        in_specs=[pl.BlockSpec((tm, d), row), pl.BlockSpec((tm, d), row),
                  pl.BlockSpec((tm, n_e), row),
                  pl.BlockSpec((1, d, tf), lambda i, e, f: (e, 0, f)),
                  pl.BlockSpec((1, d, tf), lambda i, e, f: (e, 0, f)),
                  pl.BlockSpec((1, tf, d), lambda i, e, f: (e, f, 0))],
        out_specs=pl.BlockSpec((tm, d), row),
        out_shape=jax.ShapeDtypeStruct((n, d), F32),
        scratch_shapes=[pltpu.VMEM((tm, d), F32)],
        compiler_params=_cparams(("parallel", "arbitrary", "arbitrary")),
        name="moe_dense",
    )(x1, h2, comb, wg, wu, wd)


def _rope_tables(pos):
    half = HEAD_DIM // 2
    inv = ROPE_BASE ** (-np.arange(half, dtype=np.float64) / half)
    ang = pos.astype(np.float64)[:, None] * inv[None, :]
    cos, sin = np.cos(ang), np.sin(ang)
    reps = LANES // HEAD_DIM
    cos_t = np.tile(np.concatenate([cos, cos], axis=1), (1, reps))
    sin_t = np.tile(np.concatenate([-sin, sin], axis=1), (1, reps))
    return jnp.asarray(cos_t.astype(np.float32)), jnp.asarray(sin_t.astype(np.float32))


def _token_tile(n):
    for tm in (512, 256, 128, 64, 32, 16, 8):
        if n % tm == 0:
            return tm
    return n


def kernel(x_prompt, x_sample, cache_k, cache_v, state_ret, page_table, g_mix, w_in, ret_gn,
           q_norm_g, k_norm_g, w_out, g_ffn, w_ffn_gate, w_ffn_up, w_ffn_down, w_router,
           w_exp_gate, w_exp_up, w_exp_down):
    b, t, d = x_prompt.shape
    db, ds, _ = x_sample.shape
    depth = w_in.shape[0]
    n_phys = cache_k.shape[1]
    n_p, n_s = b * t, db * ds
    tm_p, tm_s = _token_tile(n_p), _token_tile(n_s)
    past_len = page_table.shape[1] * PAGE_SIZE

    cos_p, sin_p = _rope_tables(np.arange(t))
    cos_s, sin_s = _rope_tables(np.broadcast_to(past_len + np.arange(ds)[None, :],
                                                (db, ds)).reshape(n_s))
    head_mean = (jnp.kron(jnp.eye(H_MOBA, dtype=F32), jnp.ones((HEAD_DIM, HEAD_DIM), F32))
                 / HEAD_DIM).astype(BF16)
    cache_kt = cache_k.transpose(0, 1, 3, 4, 2)
    cache_vt = cache_v.transpose(0, 1, 3, 4, 2)

    xp = x_prompt.reshape(n_p, d)
    xs = x_sample.reshape(n_s, d)
    sp_l, ss_l, ks_l, vs_l = [], [], [], []
    kv_t = None
    for l in range(depth):
        w_in_b = w_in[l].astype(BF16)
        w_out_b = w_out[l].astype(BF16)
        g1 = g_mix[l].reshape(1, d)
        g2 = g_ffn[l].reshape(1, d)
        qg = jnp.tile(q_norm_g[l], H_MOBA).reshape(1, W_MOBA)
        kg = jnp.tile(k_norm_g[l], H_MOBA).reshape(1, W_MOBA)
        gn = ret_gn[l].reshape(1, W_RET)

        (qr, kr, vr, gr, qm, kmb, kmean, kt_all, vt_all) = proj_in(
            xp, g1, w_in_b, cos_p, sin_p, qg, kg, head_mean, tm=tm_p,
            prompt=(l, depth, b, kv_t))
        kv_t = (kt_all, vt_all)
        seq = lambda a: a.reshape(b, t, a.shape[-1])
        o_r, s_fin = retention_prompt(seq(qr), seq(kr), seq(vr), seq(gr), gn)
        o_m = moba_prompt(seq(qm), seq(kmb), vt_all,
                          kmean.reshape(b, t // MOBA_BLOCK, W_MOBA), l)
        x1p, h2p = out_proj(o_r.reshape(n_p, W_RET), o_m.reshape(n_p, W_MOBA), xp, w_out_b, g2,
                            tm=tm_p)
        sp_l.append(s_fin)

        (qr, kr, vr, gr, qm, km, vm) = proj_in(
            xs, g1, w_in_b, cos_s, sin_s, qg, kg, head_mean, tm=tm_s)
        f32 = lambda a: a.astype(F32)
        o_r, s_new = retention_sample(f32(qr), f32(kr), f32(vr), f32(gr), gn, state_ret[l])
        heads = lambda a: a.reshape(n_s, H_MOBA, HEAD_DIM)
        o_m = moba_sample(heads(qm), heads(km), heads(vm), cache_kt, cache_vt, page_table,
                          l).reshape(n_s, W_MOBA)
        x1s, h2s = out_proj(o_r.astype(BF16), o_m.astype(BF16), xs, w_out_b, g2, tm=tm_s)
        ss_l.append(s_new)
        ks_l.append(km.reshape(db, ds, H_MOBA, HEAD_DIM))
        vs_l.append(vm.reshape(db, ds, H_MOBA, HEAD_DIM))

        i = l // 2
        if l % 2 == 0:
            wg, wu, wd = (w_ffn_gate[i].astype(BF16), w_ffn_up[i].astype(BF16),
                          w_ffn_down[i].astype(BF16))
            xp = ffn_dense(x1p, h2p, wg, wu, wd, tm=tm_p)
            xs = ffn_dense(x1s, h2s, wg, wu, wd, tm=tm_s)
        else:
            wg, wu, wd = (w_exp_gate[i].astype(BF16), w_exp_up[i].astype(BF16),
                          w_exp_down[i].astype(BF16))
            xp = moe_routed(x1p, h2p, w_router[i], wg, wu, wd, tm=tm_p)
            route_s, _ = router(h2s, w_router[i], tm=tm_s)
            xs = moe_dense(x1s, h2s, combine_matrix(route_s, wg.shape[0]), wg, wu, wd, tm=tm_s)

    token_major = lambda a: a.transpose(0, 1, 4, 2, 3)
    return (xp.reshape(b, t, d), xs.reshape(db, ds, d),
            jnp.stack(sp_l), token_major(kv_t[0]), token_major(kv_t[1]),
            jnp.stack(ss_l), jnp.stack(ks_l), jnp.stack(vs_l))
```

```python
import functools
import math

import jax
import jax.numpy as jnp
import numpy as np
from jax import lax
from jax.experimental import pallas as pl
from jax.experimental.pallas import tpu as pltpu

F32 = jnp.float32
BF16 = jnp.bfloat16

HEAD_DIM = 64
H_RET = 8
H_MOBA = 8
W_RET = H_RET * HEAD_DIM
W_MOBA = H_MOBA * HEAD_DIM
PAGE_SIZE = 128
ROPE_BASE = 10000.0
MOBA_BLOCK = 256
MOBA_TOPK = 3
TOP_K = 2
EPS = 1e-6
RET_CHUNK = 256
MOBA_KEY_CHUNK = 256
MOBA_UNROLL = 8
SUM_ROWS = 16
FF_TILE_MAX = 1792
MOE_TILE = 512
MOE_ROW_TILE = 512
ROW_DMA_UNROLL = 8
LANES = 128
NEG = -1e30
VMEM_LIMIT = 56 * 1024 * 1024

HIGHEST = lax.Precision.HIGHEST


def _cparams(sem):
    return pltpu.CompilerParams(dimension_semantics=sem, vmem_limit_bytes=VMEM_LIMIT)


def _silu(x):
    return x / (1.0 + jnp.exp(-x))


def _dot(a, b):
    return jnp.dot(a, b, preferred_element_type=F32)


def _dot_nt(a, b, precision=None):
    return lax.dot_general(a, b, (((1,), (1,)), ((), ())), precision=precision,
                           preferred_element_type=F32)


def _dot_tn(a, b):
    return lax.dot_general(a, b, (((0,), (0,)), ((), ())), preferred_element_type=F32)


def _proj_in_kernel(x_ref, g_ref, w_ref, cos_ref, sin_ref, qg_ref, kg_ref, hm_ref, *refs,
                    prompt, first_layer):
    if prompt and not first_layer:
        refs = refs[2:]
    qr_ref, kr_ref, vr_ref, gr_ref, qm_ref = refs[:5]
    x = x_ref[...]
    tm = x.shape[0]
    ms = jnp.mean(x * x, axis=-1, keepdims=True)
    h = (x * lax.rsqrt(ms + EPS) * g_ref[...]).astype(BF16)

    def proj(c0, width):
        return _dot(h, w_ref[:, c0:c0 + width])

    cos = cos_ref[...]
    sin = sin_ref[...]
    lane = lax.broadcasted_iota(jnp.int32, (tm, LANES), 1)
    first_half = (lane % HEAD_DIM) < (HEAD_DIM // 2)

    def rope(z):
        outs = []
        for c in range(z.shape[1] // LANES):
            zc = z[:, c * LANES:(c + 1) * LANES]
            rot = jnp.where(first_half,
                            pltpu.roll(zc, LANES - HEAD_DIM // 2, 1),
                            pltpu.roll(zc, HEAD_DIM // 2, 1))
            outs.append(zc * cos + rot * sin)
        return jnp.concatenate(outs, axis=1)

    hm = hm_ref[...]

    def head_norm(z, g):
        sq = z * z
        hi = sq.astype(BF16)
        lo = (sq - hi.astype(F32)).astype(BF16)
        msq = _dot(hi, hm) + _dot(lo, hm)
        return z * lax.rsqrt(msq + EPS) * g

    qr_ref[...] = rope(proj(0, W_RET)).astype(BF16)
    kr_ref[...] = (rope(proj(W_RET, W_RET)) * (HEAD_DIM ** -0.5)).astype(BF16)
    vr_ref[...] = proj(2 * W_RET, W_RET).astype(BF16)
    gr_ref[...] = proj(3 * W_RET, W_RET).astype(BF16)
    base = 4 * W_RET
    qm_ref[...] = head_norm(proj(base, W_MOBA), qg_ref[...])
    km = head_norm(proj(base + W_MOBA, W_MOBA), kg_ref[...])
    vm = proj(base + 2 * W_MOBA, W_MOBA)
    if not prompt:
        km_ref, vm_ref = refs[5:]
        km_ref[...] = km
        vm_ref[...] = vm
        return
    kmb_ref, kmean_ref, kt_ref, vt_ref = refs[5:]
    kmb_ref[...] = km.astype(BF16)
    for r in range(tm // MOBA_BLOCK):
        kmean_ref[0, r:r + 1, :] = jnp.mean(
            km[r * MOBA_BLOCK:(r + 1) * MOBA_BLOCK], axis=0, keepdims=True)
    for src, dst in ((km, kt_ref), (vm, vt_ref)):
        if first_layer:
            if dst.shape[0] > 1:
                dst[1:] = jnp.zeros((dst.shape[0] - 1,) + dst.shape[1:], F32)
            dst = dst.at[0]
        src_t = src.T
        for hd in range(H_MOBA):
            dst[hd] = src_t[hd * HEAD_DIM:(hd + 1) * HEAD_DIM]


def proj_in(x, g, w_bf, cos, sin, qg, kg, hm, *, tm, prompt=None):
    n, d = x.shape
    n_pos_tiles = cos.shape[0] // tm
    row = lambda i: (i, 0)
    fixed = lambda i: (0, 0)
    wide = lambda dt: jax.ShapeDtypeStruct((n, W_RET), dt)
    out_shape = [wide(BF16), wide(BF16), wide(BF16), wide(BF16), wide(F32)]
    out_specs = [pl.BlockSpec((tm, W_RET), row)] * 5
    in_specs = [
        pl.BlockSpec((tm, d), row),
        pl.BlockSpec((1, d), fixed),
        pl.BlockSpec(w_bf.shape, fixed),
        pl.BlockSpec((tm, LANES), lambda i: (i % n_pos_tiles, 0)),
        pl.BlockSpec((tm, LANES), lambda i: (i % n_pos_tiles, 0)),
        pl.BlockSpec((1, W_MOBA), fixed),
        pl.BlockSpec((1, W_MOBA), fixed),
        pl.BlockSpec((W_MOBA, W_MOBA), fixed),
    ]
    args = [x, g, w_bf, cos, sin, qg, kg, hm]
    aliases = {}
    first_layer = True
    if prompt is None:
        out_shape += [wide(F32), wide(F32)]
        out_specs += [pl.BlockSpec((tm, W_RET), row)] * 2
    else:
        layer, depth, batch, prev = prompt
        first_layer = prev is None
        tiles_per_seq = n // batch // tm
        nb = tm // MOBA_BLOCK
        kv_shape = jax.ShapeDtypeStruct((depth, batch, H_MOBA, HEAD_DIM, n // batch), F32)
        if first_layer:
            kv_spec = pl.BlockSpec((depth, None, H_MOBA, HEAD_DIM, tm),
                                   lambda i: (0, i // tiles_per_seq, 0, 0, i % tiles_per_seq))
        else:
            kv_spec = pl.BlockSpec((None, None, H_MOBA, HEAD_DIM, tm),
                                   lambda i: (layer, i // tiles_per_seq, 0, 0, i % tiles_per_seq))
            aliases = {len(args): len(out_shape) + 2, len(args) + 1: len(out_shape) + 3}
            in_specs += [pl.BlockSpec(memory_space=pl.ANY)] * 2
            args += list(prev)
        out_shape += [wide(BF16), jax.ShapeDtypeStruct((n // tm, nb, W_MOBA), F32),
                      kv_shape, kv_shape]
        out_specs += [pl.BlockSpec((tm, W_RET), row),
                      pl.BlockSpec((1, nb, W_MOBA), lambda i: (i, 0, 0)), kv_spec, kv_spec]
    return pl.pallas_call(
        functools.partial(_proj_in_kernel, prompt=prompt is not None, first_layer=first_layer),
        grid=(n // tm,),
        in_specs=in_specs,
        out_specs=out_specs,
        out_shape=out_shape,
        input_output_aliases=aliases,
        compiler_params=_cparams(("parallel",)),
        name="proj_in",
    )(*args)


def _ret_prompt_kernel(q_ref, k_ref, v_ref, g_ref, gn_ref, dm_ref, cross_ref, kdec_ref,
                       gc_ref, o_ref, sfin_ref, s_scr):
    c = pl.program_id(1)

    @pl.when(c == 0)
    def _():
        s_scr[...] = jnp.zeros_like(s_scr)

    heads = range(H_RET)
    sls = [slice(h * HEAD_DIM, (h + 1) * HEAD_DIM) for h in heads]
    qs = [q_ref[0, :, sl] for sl in sls]
    ks = [k_ref[0, :, sl] for sl in sls]
    vs = [v_ref[0, :, sl] for sl in sls]
    s0s = [s_scr[h] for h in heads]
    scores = [_dot_nt(qs[h], ks[h]) for h in heads]
    cross = [_dot(qs[h], s0s[h].astype(BF16)) for h in heads]
    kds = [(ks[h].astype(F32) * kdec_ref[h]).astype(BF16) for h in heads]
    kvs = [_dot_tn(kds[h], vs[h]) for h in heads]
    inner = [_dot((scores[h] * dm_ref[h]).astype(BF16), vs[h]) for h in heads]
    outs = []
    for h in heads:
        s_scr[h] = gc_ref[h] * s0s[h] + kvs[h]
        o = inner[h] + cross[h] * cross_ref[h]
        mu = jnp.mean(o, axis=-1, keepdims=True)
        oc = o - mu
        var = jnp.mean(oc * oc, axis=-1, keepdims=True)
        on = oc * lax.rsqrt(var + EPS) * gn_ref[:, sls[h]]
        outs.append(on * _silu(g_ref[0, :, sls[h]].astype(F32)))
    o_ref[0] = jnp.concatenate(outs, axis=1).astype(BF16)

    @pl.when(c == pl.num_programs(1) - 1)
    def _():
        sfin_ref[0] = s_scr[...]


def _ret_log_decay():
    return np.log1p(-np.exp2(-5.0 - np.arange(H_RET, dtype=np.float64)))


def retention_prompt(qr, kr, vr, gr, gn):
    b, t, w = qr.shape
    c = min(RET_CHUNK, t)
    ld = _ret_log_decay()
    i = np.arange(c, dtype=np.float64)
    diff = i[:, None] - i[None, :]
    f32c = lambda a: jnp.asarray(np.ascontiguousarray(a, dtype=np.float32))
    dmask = f32c(np.where(diff >= 0, np.exp(ld[:, None, None] * np.maximum(diff, 0.0)), 0.0))
    bc = lambda a: f32c(np.broadcast_to(a[:, :, None], (H_RET, c, HEAD_DIM)))
    cross = bc(np.exp(ld[:, None] * (i + 1.0)[None, :]))
    kdec = bc(np.exp(ld[:, None] * (c - 1.0 - i)[None, :]))
    gc = f32c(np.broadcast_to(np.exp(ld * c)[:, None, None], (H_RET, HEAD_DIM, HEAD_DIM)))
    tile = pl.BlockSpec((1, c, w), lambda bi, ci: (bi, ci, 0))
    const3 = lambda shape: pl.BlockSpec(shape, lambda bi, ci: (0, 0, 0))
    return pl.pallas_call(
        _ret_prompt_kernel,
        grid=(b, t // c),
        in_specs=[tile, tile, tile, tile,
                  pl.BlockSpec((1, w), lambda bi, ci: (0, 0)),
                  const3((H_RET, c, c)), const3((H_RET, c, HEAD_DIM)),
                  const3((H_RET, c, HEAD_DIM)), const3((H_RET, HEAD_DIM, HEAD_DIM))],
        out_specs=[tile,
                   pl.BlockSpec((1, H_RET, HEAD_DIM, HEAD_DIM), lambda bi, ci: (bi, 0, 0, 0))],
        out_shape=[jax.ShapeDtypeStruct((b, t, w), BF16),
                   jax.ShapeDtypeStruct((b, H_RET, HEAD_DIM, HEAD_DIM), F32)],
        scratch_shapes=[pltpu.VMEM((H_RET, HEAD_DIM, HEAD_DIM), F32)],
        compiler_params=_cparams(("parallel", "arbitrary")),
        name="retention_prompt",
    )(qr, kr, vr, gr, gn, dmask, cross, kdec, gc)


def _moba_prompt_kernel(q_ref, k_ref, v_ref, km_ref, o_ref, vt_scr, sel_scr, m_scr, acc_scr):
    qi = pl.program_id(2)
    nb = km_ref.shape[1]
    bs = MOBA_BLOCK
    kc = MOBA_KEY_CHUNK
    n_chunks = bs // kc
    n_heads = LANES // HEAD_DIM

    @pl.when(qi == 0)
    def _():
        for j in range(nb):
            for hh in range(n_heads):
                vt_scr[j, hh, :HEAD_DIM] = v_ref[hh, :, j * bs:(j + 1) * bs].astype(BF16)
                vt_scr[j, hh, HEAD_DIM:] = jnp.ones((SUM_ROWS, bs), BF16)

    qt = (q_ref[0] * (HEAD_DIM ** -0.5 * math.log2(math.e))).T
    drow = lax.broadcasted_iota(jnp.int32, qt.shape, 0)
    blk = lax.broadcasted_iota(jnp.int32, (nb, bs), 0)
    kmean = km_ref[0]
    qts = []
    for hh in range(n_heads):
        qh = jnp.where(drow // HEAD_DIM == hh, qt, 0.0)
        qts.append(qh.astype(BF16))
        gate = jnp.dot(kmean, qh, precision=HIGHEST, preferred_element_type=F32)
        g = jnp.where(blk < qi, gate, -jnp.inf)
        sel = jnp.zeros((nb, bs), F32)
        for _ in range(min(MOBA_TOPK, nb)):
            mx = jnp.max(g, axis=0, keepdims=True)
            idx = jnp.min(jnp.where(g == mx, blk, nb), axis=0, keepdims=True)
            pick = blk == idx
            sel = jnp.where(pick, 1.0, sel)
            g = jnp.where(pick, -jnp.inf, g)
        sel_scr[hh] = jnp.where(blk < qi, sel, 0.0)

    m_scr[...] = jnp.full(m_scr.shape, NEG, F32)
    acc_scr[...] = jnp.zeros(acc_scr.shape, F32)

    def merge(hh, mb, pv, sel_row=None):
        m = m_scr[hh]
        m_new = jnp.maximum(m, mb)
        wb = jnp.exp2(mb - m_new)
        if sel_row is not None:
            m_new = jnp.where(sel_row, m_new, m)
            wb = jnp.where(sel_row, wb, 0.0)
        wa = jnp.exp2(m - m_new)
        m_scr[hh] = m_new
        acc_scr[hh] = wa * acc_scr[hh] + wb * pv

    krow = lax.broadcasted_iota(jnp.int32, (kc, bs), 0)
    qcol = lax.broadcasted_iota(jnp.int32, (kc, bs), 1)

    def blocks(j0, n_blocks, own):
        units = [(u, c, hh) for u in range(n_blocks) for c in range(n_chunks)
                 for hh in range(n_heads)]
        k_rows = k_ref[0, pl.ds(pl.multiple_of(j0 * bs, bs), n_blocks * bs), :]
        wide = [_dot(k_rows, qts[hh]) for hh in range(n_heads)]
        parts = []
        for u, c, hh in units:
            s = wide[hh][u * bs + c * kc:u * bs + (c + 1) * kc]
            if own:
                s = jnp.where(krow + c * kc <= qcol, s, NEG)
            mb = jnp.max(s, axis=0, keepdims=True)
            parts.append((mb, jnp.exp2(s - mb).astype(BF16)))
        pvs = [_dot(vt_scr[j0 + u, hh, :, c * kc:(c + 1) * kc], parts[i][1])
               for i, (u, c, hh) in enumerate(units)]
        for i, (u, c, hh) in enumerate(units):
            sel_row = None if own else sel_scr[hh, pl.ds(j0 + u, 1), :] > 0.0
            merge(hh, parts[i][0], pvs[i], sel_row)

    def group_body(g, c_):
        blocks(g * MOBA_UNROLL, MOBA_UNROLL, False)
        return c_

    n_groups = qi // MOBA_UNROLL
    lax.fori_loop(0, n_groups, group_body, 0)
    done = n_groups * MOBA_UNROLL
    piece = MOBA_UNROLL // 2
    while piece >= 1:
        @pl.when((qi & piece) != 0)
        def _(done=done, piece=piece):
            blocks(done, piece, False)
        done = done + (qi & piece)
        piece //= 2
    blocks(qi, 1, True)
    outs = [acc_scr[hh, :HEAD_DIM] / acc_scr[hh, HEAD_DIM:HEAD_DIM + 1]
            for hh in range(n_heads)]
    o_ref[0] = jnp.concatenate(outs, axis=0).T.astype(BF16)


def moba_prompt(qm, kmb, vt_all, kmean, layer):
    b, t, w = qm.shape
    nb = t // MOBA_BLOCK
    n_heads = LANES // HEAD_DIM
    return pl.pallas_call(
        _moba_prompt_kernel,
        grid=(b, w // LANES, nb),
        in_specs=[
            pl.BlockSpec((1, MOBA_BLOCK, LANES), lambda bi, hp, qi: (bi, qi, hp)),
            pl.BlockSpec((1, t, LANES), lambda bi, hp, qi: (bi, 0, hp)),
            pl.BlockSpec((None, None, n_heads, HEAD_DIM, t),
                         lambda bi, hp, qi: (layer, bi, hp, 0, 0)),
            pl.BlockSpec((1, nb, LANES), lambda bi, hp, qi: (bi, 0, hp)),
        ],
        out_specs=pl.BlockSpec((1, MOBA_BLOCK, LANES), lambda bi, hp, qi: (bi, qi, hp)),
        out_shape=jax.ShapeDtypeStruct((b, t, w), BF16),
        scratch_shapes=[pltpu.VMEM((nb, n_heads, HEAD_DIM + SUM_ROWS, MOBA_BLOCK), BF16),
                        pltpu.VMEM((n_heads, nb, MOBA_BLOCK), F32),
                        pltpu.VMEM((n_heads, 1, MOBA_BLOCK), F32),
                        pltpu.VMEM((n_heads, HEAD_DIM + SUM_ROWS, MOBA_BLOCK), F32)],
        compiler_params=_cparams(("parallel", "parallel", "arbitrary")),
        name="moba_prompt",
    )(qm, kmb, vt_all, kmean)


def _ret_sample_kernel(q_ref, k_ref, v_ref, g_ref, gam_ref, gn_ref, s0_ref, o_ref, s_ref):
    k_t = k_ref[...]
    v_t = v_ref[...]
    q_t = q_ref[...]
    gam = gam_ref[...]
    acc = jnp.zeros(v_t.shape, F32)
    for dk in range(HEAD_DIM):
        s_new = gam * s0_ref[dk] + k_t[dk:dk + 1, :] * v_t
        s_ref[dk] = s_new
        acc = acc + q_t[dk:dk + 1, :] * s_new
    mu = jnp.mean(acc, axis=0, keepdims=True)
    oc = acc - mu
    var = jnp.mean(oc * oc, axis=0, keepdims=True)
    o_ref[...] = oc * lax.rsqrt(var + EPS) * gn_ref[...] * _silu(g_ref[...])


def retention_sample(q, k, v, gr, gn, state_t, layer):
    n = q.shape[0]
    heads_t = lambda a: a.reshape(n, H_RET, HEAD_DIM).transpose(1, 2, 0)
    gam = jnp.asarray(np.broadcast_to(np.exp(_ret_log_decay())[:, None, None],
                                      (H_RET, 1, n)).astype(np.float32))
    gn_t = jnp.broadcast_to(gn.reshape(H_RET, HEAD_DIM)[:, :, None], (H_RET, HEAD_DIM, n))
    vec = pl.BlockSpec((None, HEAD_DIM, n), lambda h: (h, 0, 0))
    o_t, s_new = pl.pallas_call(
        _ret_sample_kernel,
        grid=(H_RET,),
        in_specs=[vec, vec, vec, vec,
                  pl.BlockSpec((None, 1, n), lambda h: (h, 0, 0)), vec,
                  pl.BlockSpec((None, None, HEAD_DIM, HEAD_DIM, n),
                               lambda h: (layer, h, 0, 0, 0))],
        out_specs=[vec, pl.BlockSpec((None, HEAD_DIM, HEAD_DIM, n), lambda h: (h, 0, 0, 0))],
        out_shape=[jax.ShapeDtypeStruct((H_RET, HEAD_DIM, n), F32),
                   jax.ShapeDtypeStruct((H_RET, HEAD_DIM, HEAD_DIM, n), F32)],
        compiler_params=_cparams(("parallel",)),
        name="retention_sample",
    )(heads_t(q), heads_t(k), heads_t(v), heads_t(gr), gam, gn_t, state_t)
    return o_t.transpose(2, 0, 1).reshape(n, W_RET), s_new


def _moba_sample_kernel(pt_ref, q_ref, qt_ref, kn_ref, vn_ref, *refs, n_pages):
    del pt_ref
    k_refs = refs[:n_pages]
    v_refs = refs[n_pages:2 * n_pages]
    o_ref = refs[2 * n_pages]
    ppb = MOBA_BLOCK // PAGE_SIZE
    nbp = n_pages // ppb
    qscale = HEAD_DIM ** -0.5 * math.log2(math.e)
    qt = qt_ref[0] * qscale
    qcols = [jnp.broadcast_to(qt[:, h:h + 1], (HEAD_DIM, PAGE_SIZE)) for h in range(H_MOBA)]
    s_pages = []
    for p in range(n_pages):
        rows = [jnp.sum(k_refs[p][h] * qcols[h], axis=0, keepdims=True) for h in range(H_MOBA)]
        s_pages.append(jnp.concatenate(rows, axis=0))
    gates, maxes = [], []
    for j in range(nbp):
        tot = s_pages[j * ppb]
        top = s_pages[j * ppb]
        for p in range(1, ppb):
            tot = tot + s_pages[j * ppb + p]
            top = jnp.maximum(top, s_pages[j * ppb + p])
        gates.append(jnp.sum(tot, axis=1, keepdims=True))
        maxes.append(jnp.max(top, axis=1, keepdims=True))
    k_sel = min(MOBA_TOPK, nbp + 1)
    sels = []
    for j in range(nbp):
        rank = jnp.zeros(gates[j].shape, F32)
        for i in range(nbp):
            if i == j:
                continue
            ahead = (gates[i] > gates[j]) | ((gates[i] == gates[j]) & (i < j))
            rank = rank + jnp.where(ahead, 1.0, 0.0)
        sels.append(rank < k_sel)
    s_own = jnp.sum(q_ref[0] * kn_ref[0], axis=1, keepdims=True) * qscale
    m = s_own
    for j in range(nbp):
        m = jnp.maximum(m, jnp.where(sels[j], maxes[j], NEG))
    p_own = jnp.exp2(s_own - m)
    l = p_own
    accs = [jnp.zeros((HEAD_DIM, PAGE_SIZE), F32) for _ in range(H_MOBA)]
    for p in range(n_pages):
        shift = jnp.where(sels[p // ppb], m, -NEG)
        pp = jnp.exp2(s_pages[p] - shift)
        l = l + jnp.sum(pp, axis=1, keepdims=True)
        for h in range(H_MOBA):
            accs[h] = accs[h] + pp[h:h + 1, :] * v_refs[p][h]
    o_past = jnp.concatenate([jnp.sum(a.T, axis=0, keepdims=True) for a in accs], axis=0)
    o_ref[0] = (o_past + p_own * vn_ref[0]) / l


def moba_sample(q, k_new, v_new, cache_k, cache_v, page_table, layer):
    n = q.shape[0]
    n_pages = page_table.shape[1]
    vec = pl.BlockSpec((1, H_MOBA, HEAD_DIM), lambda s, pt: (s, 0, 0))
    vec_t = pl.BlockSpec((1, HEAD_DIM, H_MOBA), lambda s, pt: (s, 0, 0))

    def page_spec(p):
        return pl.BlockSpec((None, None, H_MOBA, HEAD_DIM, PAGE_SIZE),
                            lambda s, pt: (layer, pt[s, p], 0, 0, 0))

    pages = [page_spec(p) for p in range(n_pages)]
    return pl.pallas_call(
        functools.partial(_moba_sample_kernel, n_pages=n_pages),
        grid_spec=pltpu.PrefetchScalarGridSpec(
            num_scalar_prefetch=1,
            grid=(n,),
            in_specs=[vec, vec_t, vec, vec] + pages + pages,
            out_specs=vec,
        ),
        out_shape=jax.ShapeDtypeStruct((n, H_MOBA, HEAD_DIM), F32),
        compiler_params=_cparams(("arbitrary",)),
        name="moba_sample",
    )(page_table, q, q.transpose(0, 2, 1), k_new, v_new,
      *([cache_k] * n_pages), *([cache_v] * n_pages))


def _out_proj_kernel(or_ref, om_ref, x_ref, w_ref, g_ref, x1_ref, h2_ref):
    x1 = (x_ref[...] + _dot(or_ref[...], w_ref[:W_RET, :])
          + _dot(om_ref[...], w_ref[W_RET:, :]))
    x1_ref[...] = x1
    ms = jnp.mean(x1 * x1, axis=-1, keepdims=True)
    h2_ref[...] = x1 * lax.rsqrt(ms + EPS) * g_ref[...]


def out_proj(o_r, o_m, x, w_bf, g, *, tm):
    n, d = x.shape
    row = lambda i: (i, 0)
    fixed = lambda i: (0, 0)
    return pl.pallas_call(
        _out_proj_kernel,
        grid=(n // tm,),
        in_specs=[pl.BlockSpec((tm, W_RET), row), pl.BlockSpec((tm, W_MOBA), row),
                  pl.BlockSpec((tm, d), row), pl.BlockSpec(w_bf.shape, fixed),
                  pl.BlockSpec((1, d), fixed)],
        out_specs=[pl.BlockSpec((tm, d), row), pl.BlockSpec((tm, d), row)],
        out_shape=[jax.ShapeDtypeStruct((n, d), F32), jax.ShapeDtypeStruct((n, d), F32)],
        compiler_params=_cparams(("parallel",)),
        name="out_proj",
    )(o_r, o_m, x, w_bf, g)


def _ffn_kernel(x1_ref, h_ref, wg_ref, wu_ref, wd_ref, o_ref, acc_ref):
    f = pl.program_id(1)

    @pl.when(f == 0)
    def _():
        acc_ref[...] = x1_ref[...]

    h = h_ref[...].astype(BF16)
    a = _silu(_dot(h, wg_ref[...])) * _dot(h, wu_ref[...])
    acc_ref[...] += _dot(a.astype(BF16), wd_ref[...])

    @pl.when(f == pl.num_programs(1) - 1)
    def _():
        o_ref[...] = acc_ref[...]


def _ff_tile(d_ff):
    for tf in range(FF_TILE_MAX, 0, -LANES):
        if d_ff % tf == 0:
            return tf
    return d_ff


def ffn_dense(x1, h2, wg, wu, wd, *, tm):
    n, d = x1.shape
    d_ff = wg.shape[1]
    tf = _ff_tile(d_ff)
    row = lambda i, f: (i, 0)
    return pl.pallas_call(
        _ffn_kernel,
        grid=(n // tm, d_ff // tf),
        in_specs=[pl.BlockSpec((tm, d), row), pl.BlockSpec((tm, d), row),
                  pl.BlockSpec((d, tf), lambda i, f: (0, f)),
                  pl.BlockSpec((d, tf), lambda i, f: (0, f)),
                  pl.BlockSpec((tf, d), lambda i, f: (f, 0))],
        out_specs=pl.BlockSpec((tm, d), row),
        out_shape=jax.ShapeDtypeStruct((n, d), F32),
        scratch_shapes=[pltpu.VMEM((tm, d), F32)],
        compiler_params=_cparams(("parallel", "arbitrary")),
        name="ffn_dense",
    )(x1, h2, wg, wu, wd)


ROUTE_E, ROUTE_RANK, ROUTE_W = 0, 2, 4


def _router_kernel(h_ref, w_ref, route_ref, cnt_ref, carry_scr):
    @pl.when(pl.program_id(0) == 0)
    def _():
        carry_scr[...] = jnp.zeros_like(carry_scr)

    logits = jnp.dot(h_ref[...], w_ref[...], precision=HIGHEST, preferred_element_type=F32)
    tm, n_e = logits.shape
    eid = lax.broadcasted_iota(jnp.int32, logits.shape, 1)
    v1 = jnp.max(logits, axis=1, keepdims=True)
    i1 = jnp.min(jnp.where(logits == v1, eid, n_e), axis=1, keepdims=True)
    rest = jnp.where(eid == i1, -jnp.inf, logits)
    v2 = jnp.max(rest, axis=1, keepdims=True)
    i2 = jnp.min(jnp.where(rest == v2, eid, n_e), axis=1, keepdims=True)
    e2 = jnp.exp(v2 - v1)
    w1 = 1.0 / (1.0 + e2)
    w2 = e2 / (1.0 + e2)
    onehot = jnp.where((eid == i1) | (eid == i2), 1.0, 0.0)
    r = lax.broadcasted_iota(jnp.int32, (tm, tm), 0)
    c = lax.broadcasted_iota(jnp.int32, (tm, tm), 1)
    earlier = jnp.where(c < r, 1.0, 0.0).astype(BF16)
    ranks = _dot(earlier, onehot.astype(BF16)) + carry_scr[...]
    r1 = jnp.sum(jnp.where(eid == i1, ranks, 0.0), axis=1, keepdims=True)
    r2 = jnp.sum(jnp.where(eid == i2, ranks, 0.0), axis=1, keepdims=True)
    total = carry_scr[...] + jnp.sum(onehot, axis=0, keepdims=True)
    carry_scr[...] = total
    cnt_ref[...] = total
    route = jnp.zeros(logits.shape, F32)
    for k, col in enumerate((i1.astype(F32), i2.astype(F32), r1, r2, w1, w2)):
        route = jnp.where(eid == k, col, route)
    route_ref[...] = route


def router(h2, w_router, *, tm):
    n, d = h2.shape
    n_e = w_router.shape[1]
    return pl.pallas_call(
        _router_kernel,
        grid=(n // tm,),
        in_specs=[pl.BlockSpec((tm, d), lambda i: (i, 0)),
                  pl.BlockSpec((d, n_e), lambda i: (0, 0))],
        out_specs=[pl.BlockSpec((tm, n_e), lambda i: (i, 0)),
                   pl.BlockSpec((1, n_e), lambda i: (0, 0))],
        out_shape=[jax.ShapeDtypeStruct((n, n_e), F32), jax.ShapeDtypeStruct((1, n_e), F32)],
        scratch_shapes=[pltpu.VMEM((1, n_e), F32)],
        compiler_params=_cparams(("arbitrary",)),
        name="router",
    )(h2, w_router)


def _dispatch_kernel(slots_ref, h_ref, xs_in_ref, xs_ref, sem):
    del xs_in_ref
    tm = h_ref.shape[0]
    base = pl.program_id(0) * tm

    def start(r, c):
        for k in range(TOP_K):
            slot = slots_ref[k, base + r]
            pltpu.make_async_copy(h_ref.at[pl.ds(r, 1)], xs_ref.at[pl.ds(slot, 1)],
                                  sem).start(priority=k)
        return c

    lax.fori_loop(0, tm, start, 0, unroll=ROW_DMA_UNROLL)

    def wait(r, c):
        for k in range(TOP_K):
            pltpu.make_async_copy(h_ref.at[pl.ds(0, 1)], xs_ref.at[pl.ds(0, 1)], sem).wait()
        return c

    lax.fori_loop(0, tm, wait, 0, unroll=ROW_DMA_UNROLL)


def moe_dispatch(slots, h2, n_slots, *, tm):
    n, d = h2.shape
    xs0 = jnp.zeros((n_slots, d), h2.dtype)
    return pl.pallas_call(
        _dispatch_kernel,
        grid_spec=pltpu.PrefetchScalarGridSpec(
            num_scalar_prefetch=1,
            grid=(n // tm,),
            in_specs=[pl.BlockSpec((tm, d), lambda i, sl: (i, 0)),
                      pl.BlockSpec(memory_space=pl.ANY)],
            out_specs=pl.BlockSpec(memory_space=pl.ANY),
            scratch_shapes=[pltpu.SemaphoreType.DMA(())],
        ),
        out_shape=jax.ShapeDtypeStruct((n_slots, d), h2.dtype),
        input_output_aliases={2: 0},
        compiler_params=_cparams(("arbitrary",)),
        name="moe_dispatch",
    )(slots, h2, xs0)


def _experts_kernel(te_ref, nv_ref, xs_ref, wg_ref, wu_ref, wd_ref, ys_ref, acc_ref):
    del te_ref
    s = pl.program_id(0)
    f = pl.program_id(1)

    @pl.when(s < nv_ref[0])
    def _():
        @pl.when(f == 0)
        def _():
            acc_ref[...] = jnp.zeros_like(acc_ref)

        x = xs_ref[...].astype(BF16)
        a = _silu(_dot(x, wg_ref[0])) * _dot(x, wu_ref[0])
        acc_ref[...] += _dot(a.astype(BF16), wd_ref[0])

        @pl.when(f == pl.num_programs(1) - 1)
        def _():
            ys_ref[...] = acc_ref[...]

    @pl.when((s >= nv_ref[0]) & (f == 0))
    def _():
        ys_ref[...] = jnp.zeros_like(ys_ref)


def moe_experts(tile_expert, n_valid, xs, wg, wu, wd):
    n_slots, d = xs.shape
    d_ff = wg.shape[2]
    tf = _ff_tile(d_ff)
    nf = d_ff // tf

    def live(s, nv):
        return jnp.minimum(s, nv[0] - 1)

    def fcol(s, f, nv):
        return jnp.where(s < nv[0], f, nf - 1)

    return pl.pallas_call(
        _experts_kernel,
        grid_spec=pltpu.PrefetchScalarGridSpec(
            num_scalar_prefetch=2,
            grid=(n_slots // MOE_TILE, nf),
            in_specs=[pl.BlockSpec((MOE_TILE, d), lambda s, f, te, nv: (live(s, nv), 0)),
                      pl.BlockSpec((1, d, tf), lambda s, f, te, nv: (te[s], 0, fcol(s, f, nv))),
                      pl.BlockSpec((1, d, tf), lambda s, f, te, nv: (te[s], 0, fcol(s, f, nv))),
                      pl.BlockSpec((1, tf, d), lambda s, f, te, nv: (te[s], fcol(s, f, nv), 0))],
            out_specs=pl.BlockSpec((MOE_TILE, d), lambda s, f, te, nv: (s, 0)),
            scratch_shapes=[pltpu.VMEM((MOE_TILE, d), F32)],
        ),
        out_shape=jax.ShapeDtypeStruct((n_slots, d), F32),
        compiler_params=_cparams(("arbitrary", "arbitrary")),
        name="moe_experts",
    )(tile_expert, n_valid, xs, wg, wu, wd)


def _combine_kernel(slots_ref, x1_ref, route_ref, ys_ref, o_ref, g_scr, sem):
    tm = x1_ref.shape[0]
    base = pl.program_id(0) * tm

    def start(r, c):
        for k in range(TOP_K):
            slot = slots_ref[k, base + r]
            pltpu.make_async_copy(ys_ref.at[pl.ds(slot, 1)], g_scr.at[k, pl.ds(r, 1)],
                                  sem).start(priority=k)
        return c

    lax.fori_loop(0, tm, start, 0, unroll=ROW_DMA_UNROLL)

    def wait(r, c):
        for k in range(TOP_K):
            pltpu.make_async_copy(ys_ref.at[pl.ds(0, 1)], g_scr.at[0, pl.ds(0, 1)], sem).wait()
        return c

    lax.fori_loop(0, tm, wait, 0, unroll=ROW_DMA_UNROLL)
    route = route_ref[...]
    out = x1_ref[...]
    for k in range(TOP_K):
        out = out + route[:, ROUTE_W + k:ROUTE_W + k + 1] * g_scr[k]
    o_ref[...] = out


def moe_combine(slots, x1, route, ys, *, tm):
    n, d = x1.shape
    return pl.pallas_call(
        _combine_kernel,
        grid_spec=pltpu.PrefetchScalarGridSpec(
            num_scalar_prefetch=1,
            grid=(n // tm,),
            in_specs=[pl.BlockSpec((tm, d), lambda i, sl: (i, 0)),
                      pl.BlockSpec((tm, route.shape[1]), lambda i, sl: (i, 0)),
                      pl.BlockSpec(memory_space=pl.ANY)],
            out_specs=pl.BlockSpec((tm, d), lambda i, sl: (i, 0)),
            scratch_shapes=[pltpu.VMEM((TOP_K, tm, d), F32), pltpu.SemaphoreType.DMA(())],
        ),
        out_shape=jax.ShapeDtypeStruct((n, d), F32),
        compiler_params=_cparams(("arbitrary",)),
        name="moe_combine",
    )(slots, x1, route, ys)


def moe_routed(x1, h2, w_router, wg, wu, wd, *, tm):
    n, _ = x1.shape
    n_e = wg.shape[0]
    route, cnt = router(h2, w_router, tm=tm)
    counts = cnt[0].astype(jnp.int32)
    padded = (counts + MOE_TILE - 1) // MOE_TILE * MOE_TILE
    ends = jnp.cumsum(padded)
    starts = ends - padded
    experts = route[:, ROUTE_E:ROUTE_E + TOP_K].astype(jnp.int32)
    ranks = route[:, ROUTE_RANK:ROUTE_RANK + TOP_K].astype(jnp.int32)
    slots = (starts[experts] + ranks).T
    n_tiles = (TOP_K * n + MOE_TILE - 1) // MOE_TILE + n_e
    tile_start = jnp.arange(n_tiles, dtype=jnp.int32) * MOE_TILE
    tile_expert = jnp.minimum(jnp.sum(tile_start[:, None] >= ends[None, :], axis=1), n_e - 1)
    n_valid = (ends[-1] // MOE_TILE).reshape(1)
    rows = min(MOE_ROW_TILE, n)
    xs = moe_dispatch(slots, h2, n_tiles * MOE_TILE, tm=rows)
    ys = moe_experts(tile_expert.astype(jnp.int32), n_valid.astype(jnp.int32), xs, wg, wu, wd)
    return moe_combine(slots, x1, route, ys, tm=rows)


def combine_matrix(route, n_e):
    eid = jnp.arange(n_e, dtype=F32)[None, :]
    comb = jnp.zeros((route.shape[0], n_e), F32)
    for k in range(TOP_K):
        comb = comb + jnp.where(route[:, ROUTE_E + k:ROUTE_E + k + 1] == eid,
                                route[:, ROUTE_W + k:ROUTE_W + k + 1], 0.0)
    return comb


def _moe_kernel(x1_ref, h_ref, c_ref, wg_ref, wu_ref, wd_ref, o_ref, acc_ref):
    e = pl.program_id(1)
    f = pl.program_id(2)

    @pl.when((e == 0) & (f == 0))
    def _():
        acc_ref[...] = x1_ref[...]

    n_e = c_ref.shape[1]
    eid = lax.broadcasted_iota(jnp.int32, c_ref.shape, 1)
    ce = jnp.sum(jnp.where(eid == e, c_ref[...], 0.0), axis=1, keepdims=True)
    h = h_ref[...].astype(BF16)
    a = _silu(_dot(h, wg_ref[0])) * _dot(h, wu_ref[0])
    acc_ref[...] += ce * _dot(a.astype(BF16), wd_ref[0])

    @pl.when((e == pl.num_programs(1) - 1) & (f == pl.num_programs(2) - 1))
    def _():
        o_ref[...] = acc_ref[...]


def moe_dense(x1, h2, comb, wg, wu, wd, *, tm):
    n, d = x1.shape
    n_e, _, d_ff = wg.shape
    tf = _ff_tile(d_ff)
    row = lambda i, e, f: (i, 0)
    return pl.pallas_call(
        _moe_kernel,
        grid=(n // tm, n_e, d_ff // tf),
        in_specs=[pl.BlockSpec((tm, d), row), pl.BlockSpec((tm, d), row),
                  pl.BlockSpec((tm, n_e), row),
                  pl.BlockSpec((1, d, tf), lambda i, e, f: (e, 0, f)),
                  pl.BlockSpec((1, d, tf), lambda i, e, f: (e, 0, f)),
                  pl.BlockSpec((1, tf, d), lambda i, e, f: (e, f, 0))],
        out_specs=pl.BlockSpec((tm, d), row),
        out_shape=jax.ShapeDtypeStruct((n, d), F32),
        scratch_shapes=[pltpu.VMEM((tm, d), F32)],
        compiler_params=_cparams(("parallel", "arbitrary", "arbitrary")),
        name="moe_dense",
    )(x1, h2, comb, wg, wu, wd)


def _rope_tables(pos):
    half = HEAD_DIM // 2
    inv = ROPE_BASE ** (-np.arange(half, dtype=np.float64) / half)
    ang = pos.astype(np.float64)[:, None] * inv[None, :]
    cos, sin = np.cos(ang), np.sin(ang)
    reps = LANES // HEAD_DIM
    cos_t = np.tile(np.concatenate([cos, cos], axis=1), (1, reps))
    sin_t = np.tile(np.concatenate([-sin, sin], axis=1), (1, reps))
    return jnp.asarray(cos_t.astype(np.float32)), jnp.asarray(sin_t.astype(np.float32))


def _token_tile(n):
    for tm in (512, 256, 128, 64, 32, 16, 8):
        if n % tm == 0:
            return tm
    return n


def kernel(x_prompt, x_sample, cache_k, cache_v, state_ret, page_table, g_mix, w_in, ret_gn,
           q_norm_g, k_norm_g, w_out, g_ffn, w_ffn_gate, w_ffn_up, w_ffn_down, w_router,
           w_exp_gate, w_exp_up, w_exp_down):
    b, t, d = x_prompt.shape
    db, ds, _ = x_sample.shape
    depth = w_in.shape[0]
    n_phys = cache_k.shape[1]
    n_p, n_s = b * t, db * ds
    tm_p, tm_s = _token_tile(n_p), _token_tile(n_s)
    past_len = page_table.shape[1] * PAGE_SIZE

    cos_p, sin_p = _rope_tables(np.arange(t))
    cos_s, sin_s = _rope_tables(np.broadcast_to(past_len + np.arange(ds)[None, :],
                                                (db, ds)).reshape(n_s))
    head_mean = (jnp.kron(jnp.eye(H_MOBA, dtype=F32), jnp.ones((HEAD_DIM, HEAD_DIM), F32))
                 / HEAD_DIM).astype(BF16)
    cache_kt = cache_k.transpose(0, 1, 3, 4, 2)
    cache_vt = cache_v.transpose(0, 1, 3, 4, 2)
    state_t = state_ret.transpose(0, 2, 3, 4, 1)

    xp = x_prompt.reshape(n_p, d)
    xs = x_sample.reshape(n_s, d)
    sp_l, ss_l, ks_l, vs_l = [], [], [], []
    kv_t = None
    for l in range(depth):
        w_in_b = w_in[l].astype(BF16)
        w_out_b = w_out[l].astype(BF16)
        g1 = g_mix[l].reshape(1, d)
        g2 = g_ffn[l].reshape(1, d)
        qg = jnp.tile(q_norm_g[l], H_MOBA).reshape(1, W_MOBA)
        kg = jnp.tile(k_norm_g[l], H_MOBA).reshape(1, W_MOBA)
        gn = ret_gn[l].reshape(1, W_RET)

        (qr, kr, vr, gr, qm, kmb, kmean, kt_all, vt_all) = proj_in(
            xp, g1, w_in_b, cos_p, sin_p, qg, kg, head_mean, tm=tm_p,
            prompt=(l, depth, b, kv_t))
        kv_t = (kt_all, vt_all)
        seq = lambda a: a.reshape(b, t, a.shape[-1])
        o_r, s_fin = retention_prompt(seq(qr), seq(kr), seq(vr), seq(gr), gn)
        o_m = moba_prompt(seq(qm), seq(kmb), vt_all,
                          kmean.reshape(b, t // MOBA_BLOCK, W_MOBA), l)
        x1p, h2p = out_proj(o_r.reshape(n_p, W_RET), o_m.reshape(n_p, W_MOBA), xp, w_out_b, g2,
                            tm=tm_p)
        sp_l.append(s_fin)

        (qr, kr, vr, gr, qm, km, vm) = proj_in(
            xs, g1, w_in_b, cos_s, sin_s, qg, kg, head_mean, tm=tm_s)
        f32 = lambda a: a.astype(F32)
        o_r, s_new = retention_sample(f32(qr), f32(kr), f32(vr), f32(gr), gn, state_t, l)
        heads = lambda a: a.reshape(n_s, H_MOBA, HEAD_DIM)
        o_m = moba_sample(heads(qm), heads(km), heads(vm), cache_kt, cache_vt, page_table,
                          l).reshape(n_s, W_MOBA)
        x1s, h2s = out_proj(o_r.astype(BF16), o_m.astype(BF16), xs, w_out_b, g2, tm=tm_s)
        ss_l.append(s_new)
        ks_l.append(km.reshape(db, ds, H_MOBA, HEAD_DIM))
        vs_l.append(vm.reshape(db, ds, H_MOBA, HEAD_DIM))

        i = l // 2
        if l % 2 == 0:
            wg, wu, wd = (w_ffn_gate[i].astype(BF16), w_ffn_up[i].astype(BF16),
                          w_ffn_down[i].astype(BF16))
            xp = ffn_dense(x1p, h2p, wg, wu, wd, tm=tm_p)
            xs = ffn_dense(x1s, h2s, wg, wu, wd, tm=tm_s)
        else:
            wg, wu, wd = (w_exp_gate[i].astype(BF16), w_exp_up[i].astype(BF16),
                          w_exp_down[i].astype(BF16))
            xp = moe_routed(x1p, h2p, w_router[i], wg, wu, wd, tm=tm_p)
            route_s, _ = router(h2s, w_router[i], tm=tm_s)
            xs = moe_dense(x1s, h2s, combine_matrix(route_s, wg.shape[0]), wg, wu, wd, tm=tm_s)

    token_major = lambda a: a.transpose(0, 1, 4, 2, 3)
    return (xp.reshape(b, t, d), xs.reshape(db, ds, d),
            jnp.stack(sp_l), token_major(kv_t[0]), token_major(kv_t[1]),
            jnp.stack(ss_l).transpose(0, 4, 1, 2, 3), jnp.stack(ks_l), jnp.stack(vs_l))
```

```python
import functools
import math

import jax
import jax.numpy as jnp
import numpy as np
from jax import lax
from jax.experimental import pallas as pl
from jax.experimental.pallas import tpu as pltpu

F32 = jnp.float32
BF16 = jnp.bfloat16

HEAD_DIM = 64
H_RET = 8
H_MOBA = 8
W_RET = H_RET * HEAD_DIM
W_MOBA = H_MOBA * HEAD_DIM
PAGE_SIZE = 128
ROPE_BASE = 10000.0
MOBA_BLOCK = 256
MOBA_TOPK = 3
TOP_K = 2
EPS = 1e-6
RET_CHUNK = 256
MOBA_KEY_CHUNK = 256
MOBA_UNROLL = 8
SUM_ROWS = 16
FF_TILE_MAX = 1792
MOE_TILE = 512
MOE_ROW_TILE = 512
ROW_DMA_UNROLL = 8
LANES = 128
NEG = -1e30
VMEM_LIMIT = 56 * 1024 * 1024

HIGHEST = lax.Precision.HIGHEST


def _cparams(sem):
    return pltpu.CompilerParams(dimension_semantics=sem, vmem_limit_bytes=VMEM_LIMIT)


def _silu(x):
    return x / (1.0 + jnp.exp(-x))


def _dot(a, b):
    return jnp.dot(a, b, preferred_element_type=F32)


def _dot_nt(a, b, precision=None):
    return lax.dot_general(a, b, (((1,), (1,)), ((), ())), precision=precision,
                           preferred_element_type=F32)


def _dot_tn(a, b):
    return lax.dot_general(a, b, (((0,), (0,)), ((), ())), preferred_element_type=F32)


def _proj_in_kernel(x_ref, g_ref, w_ref, cos_ref, sin_ref, qg_ref, kg_ref, hm_ref, *refs,
                    prompt, first_layer):
    if prompt and not first_layer:
        refs = refs[2:]
    qr_ref, kr_ref, vr_ref, gr_ref, qm_ref = refs[:5]
    x = x_ref[...]
    tm = x.shape[0]
    ms = jnp.mean(x * x, axis=-1, keepdims=True)
    h = (x * lax.rsqrt(ms + EPS) * g_ref[...]).astype(BF16)

    def proj(c0, width):
        return _dot(h, w_ref[:, c0:c0 + width])

    cos = cos_ref[...]
    sin = sin_ref[...]
    lane = lax.broadcasted_iota(jnp.int32, (tm, LANES), 1)
    first_half = (lane % HEAD_DIM) < (HEAD_DIM // 2)

    def rope(z):
        outs = []
        for c in range(z.shape[1] // LANES):
            zc = z[:, c * LANES:(c + 1) * LANES]
            rot = jnp.where(first_half,
                            pltpu.roll(zc, LANES - HEAD_DIM // 2, 1),
                            pltpu.roll(zc, HEAD_DIM // 2, 1))
            outs.append(zc * cos + rot * sin)
        return jnp.concatenate(outs, axis=1)

    hm = hm_ref[...]

    def head_norm(z, g):
        sq = z * z
        hi = sq.astype(BF16)
        lo = (sq - hi.astype(F32)).astype(BF16)
        msq = _dot(hi, hm) + _dot(lo, hm)
        return z * lax.rsqrt(msq + EPS) * g

    qr_ref[...] = rope(proj(0, W_RET)).astype(BF16)
    kr_ref[...] = (rope(proj(W_RET, W_RET)) * (HEAD_DIM ** -0.5)).astype(BF16)
    vr_ref[...] = proj(2 * W_RET, W_RET).astype(BF16)
    gr_ref[...] = proj(3 * W_RET, W_RET).astype(BF16)
    base = 4 * W_RET
    qm_ref[...] = head_norm(proj(base, W_MOBA), qg_ref[...])
    km = head_norm(proj(base + W_MOBA, W_MOBA), kg_ref[...])
    vm = proj(base + 2 * W_MOBA, W_MOBA)
    if not prompt:
        km_ref, vm_ref = refs[5:]
        km_ref[...] = km
        vm_ref[...] = vm
        return
    kmb_ref, kmean_ref, kt_ref, vt_ref = refs[5:]
    kmb_ref[...] = km.astype(BF16)
    for r in range(tm // MOBA_BLOCK):
        kmean_ref[0, r:r + 1, :] = jnp.mean(
            km[r * MOBA_BLOCK:(r + 1) * MOBA_BLOCK], axis=0, keepdims=True)
    for src, dst in ((km, kt_ref), (vm, vt_ref)):
        if first_layer:
            if dst.shape[0] > 1:
                dst[1:] = jnp.zeros((dst.shape[0] - 1,) + dst.shape[1:], F32)
            dst = dst.at[0]
        src_t = src.T
        for hd in range(H_MOBA):
            dst[hd] = src_t[hd * HEAD_DIM:(hd + 1) * HEAD_DIM]


def proj_in(x, g, w_bf, cos, sin, qg, kg, hm, *, tm, prompt=None):
    n, d = x.shape
    n_pos_tiles = cos.shape[0] // tm
    row = lambda i: (i, 0)
    fixed = lambda i: (0, 0)
    wide = lambda dt: jax.ShapeDtypeStruct((n, W_RET), dt)
    out_shape = [wide(BF16), wide(BF16), wide(BF16), wide(BF16), wide(F32)]
    out_specs = [pl.BlockSpec((tm, W_RET), row)] * 5
    in_specs = [
        pl.BlockSpec((tm, d), row),
        pl.BlockSpec((1, d), fixed),
        pl.BlockSpec(w_bf.shape, fixed),
        pl.BlockSpec((tm, LANES), lambda i: (i % n_pos_tiles, 0)),
        pl.BlockSpec((tm, LANES), lambda i: (i % n_pos_tiles, 0)),
        pl.BlockSpec((1, W_MOBA), fixed),
        pl.BlockSpec((1, W_MOBA), fixed),
        pl.BlockSpec((W_MOBA, W_MOBA), fixed),
    ]
    args = [x, g, w_bf, cos, sin, qg, kg, hm]
    aliases = {}
    first_layer = True
    if prompt is None:
        out_shape += [wide(F32), wide(F32)]
        out_specs += [pl.BlockSpec((tm, W_RET), row)] * 2
    else:
        layer, depth, batch, prev = prompt
        first_layer = prev is None
        tiles_per_seq = n // batch // tm
        nb = tm // MOBA_BLOCK
        kv_shape = jax.ShapeDtypeStruct((depth, batch, H_MOBA, HEAD_DIM, n // batch), F32)
        if first_layer:
            kv_spec = pl.BlockSpec((depth, None, H_MOBA, HEAD_DIM, tm),
                                   lambda i: (0, i // tiles_per_seq, 0, 0, i % tiles_per_seq))
        else:
            kv_spec = pl.BlockSpec((None, None, H_MOBA, HEAD_DIM, tm),
                                   lambda i: (layer, i // tiles_per_seq, 0, 0, i % tiles_per_seq))
            aliases = {len(args): len(out_shape) + 2, len(args) + 1: len(out_shape) + 3}
            in_specs += [pl.BlockSpec(memory_space=pl.ANY)] * 2
            args += list(prev)
        out_shape += [wide(BF16), jax.ShapeDtypeStruct((n // tm, nb, W_MOBA), F32),
                      kv_shape, kv_shape]
        out_specs += [pl.BlockSpec((tm, W_RET), row),
                      pl.BlockSpec((1, nb, W_MOBA), lambda i: (i, 0, 0)), kv_spec, kv_spec]
    return pl.pallas_call(
        functools.partial(_proj_in_kernel, prompt=prompt is not None, first_layer=first_layer),
        grid=(n // tm,),
        in_specs=in_specs,
        out_specs=out_specs,
        out_shape=out_shape,
        input_output_aliases=aliases,
        compiler_params=_cparams(("parallel",)),
        name="proj_in",
    )(*args)


def _ret_prompt_kernel(q_ref, k_ref, v_ref, g_ref, gn_ref, dm_ref, cross_ref, kdec_ref,
                       gc_ref, o_ref, sfin_ref, s_scr):
    c = pl.program_id(1)

    @pl.when(c == 0)
    def _():
        s_scr[...] = jnp.zeros_like(s_scr)

    heads = range(H_RET)
    sls = [slice(h * HEAD_DIM, (h + 1) * HEAD_DIM) for h in heads]
    qs = [q_ref[0, :, sl] for sl in sls]
    ks = [k_ref[0, :, sl] for sl in sls]
    vs = [v_ref[0, :, sl] for sl in sls]
    s0s = [s_scr[h] for h in heads]
    scores = [_dot_nt(qs[h], ks[h]) for h in heads]
    cross = [_dot(qs[h], s0s[h].astype(BF16)) for h in heads]
    kds = [(ks[h].astype(F32) * kdec_ref[h]).astype(BF16) for h in heads]
    kvs = [_dot_tn(kds[h], vs[h]) for h in heads]
    inner = [_dot((scores[h] * dm_ref[h]).astype(BF16), vs[h]) for h in heads]
    outs = []
    for h in heads:
        s_scr[h] = gc_ref[h] * s0s[h] + kvs[h]
        o = inner[h] + cross[h] * cross_ref[h]
        mu = jnp.mean(o, axis=-1, keepdims=True)
        oc = o - mu
        var = jnp.mean(oc * oc, axis=-1, keepdims=True)
        on = oc * lax.rsqrt(var + EPS) * gn_ref[:, sls[h]]
        outs.append(on * _silu(g_ref[0, :, sls[h]].astype(F32)))
    o_ref[0] = jnp.concatenate(outs, axis=1).astype(BF16)

    @pl.when(c == pl.num_programs(1) - 1)
    def _():
        sfin_ref[0] = s_scr[...]


def _ret_log_decay():
    return np.log1p(-np.exp2(-5.0 - np.arange(H_RET, dtype=np.float64)))


def retention_prompt(qr, kr, vr, gr, gn):
    b, t, w = qr.shape
    c = min(RET_CHUNK, t)
    ld = _ret_log_decay()
    i = np.arange(c, dtype=np.float64)
    diff = i[:, None] - i[None, :]
    f32c = lambda a: jnp.asarray(np.ascontiguousarray(a, dtype=np.float32))
    dmask = f32c(np.where(diff >= 0, np.exp(ld[:, None, None] * np.maximum(diff, 0.0)), 0.0))
    bc = lambda a: f32c(np.broadcast_to(a[:, :, None], (H_RET, c, HEAD_DIM)))
    cross = bc(np.exp(ld[:, None] * (i + 1.0)[None, :]))
    kdec = bc(np.exp(ld[:, None] * (c - 1.0 - i)[None, :]))
    gc = f32c(np.broadcast_to(np.exp(ld * c)[:, None, None], (H_RET, HEAD_DIM, HEAD_DIM)))
    tile = pl.BlockSpec((1, c, w), lambda bi, ci: (bi, ci, 0))
    const3 = lambda shape: pl.BlockSpec(shape, lambda bi, ci: (0, 0, 0))
    return pl.pallas_call(
        _ret_prompt_kernel,
        grid=(b, t // c),
        in_specs=[tile, tile, tile, tile,
                  pl.BlockSpec((1, w), lambda bi, ci: (0, 0)),
                  const3((H_RET, c, c)), const3((H_RET, c, HEAD_DIM)),
                  const3((H_RET, c, HEAD_DIM)), const3((H_RET, HEAD_DIM, HEAD_DIM))],
        out_specs=[tile,
                   pl.BlockSpec((1, H_RET, HEAD_DIM, HEAD_DIM), lambda bi, ci: (bi, 0, 0, 0))],
        out_shape=[jax.ShapeDtypeStruct((b, t, w), BF16),
                   jax.ShapeDtypeStruct((b, H_RET, HEAD_DIM, HEAD_DIM), F32)],
        scratch_shapes=[pltpu.VMEM((H_RET, HEAD_DIM, HEAD_DIM), F32)],
        compiler_params=_cparams(("parallel", "arbitrary")),
        name="retention_prompt",
    )(qr, kr, vr, gr, gn, dmask, cross, kdec, gc)


def _moba_prompt_kernel(q_ref, k_ref, v_ref, km_ref, o_ref, vt_scr, sel_scr, m_scr, acc_scr):
    qi = pl.program_id(2)
    nb = km_ref.shape[1]
    bs = MOBA_BLOCK
    kc = MOBA_KEY_CHUNK
    n_chunks = bs // kc
    n_heads = LANES // HEAD_DIM

    @pl.when(qi == 0)
    def _():
        for j in range(nb):
            for hh in range(n_heads):
                vt_scr[j, hh, :HEAD_DIM] = v_ref[hh, :, j * bs:(j + 1) * bs].astype(BF16)
                vt_scr[j, hh, HEAD_DIM:] = jnp.ones((SUM_ROWS, bs), BF16)

    qt = (q_ref[0] * (HEAD_DIM ** -0.5 * math.log2(math.e))).T
    drow = lax.broadcasted_iota(jnp.int32, qt.shape, 0)
    blk = lax.broadcasted_iota(jnp.int32, (nb, bs), 0)
    kmean = km_ref[0]
    qts = []
    for hh in range(n_heads):
        qh = jnp.where(drow // HEAD_DIM == hh, qt, 0.0)
        qts.append(qh.astype(BF16))
        gate = jnp.dot(kmean, qh, precision=HIGHEST, preferred_element_type=F32)
        g = jnp.where(blk < qi, gate, -jnp.inf)
        sel = jnp.zeros((nb, bs), F32)
        for _ in range(min(MOBA_TOPK, nb)):
            mx = jnp.max(g, axis=0, keepdims=True)
            idx = jnp.min(jnp.where(g == mx, blk, nb), axis=0, keepdims=True)
            pick = blk == idx
            sel = jnp.where(pick, 1.0, sel)
            g = jnp.where(pick, -jnp.inf, g)
        sel_scr[hh] = jnp.where(blk < qi, sel, 0.0)

    m_scr[...] = jnp.full(m_scr.shape, NEG, F32)
    acc_scr[...] = jnp.zeros(acc_scr.shape, F32)

    def merge(hh, mb, pv, sel_row=None):
        m = m_scr[hh]
        m_new = jnp.maximum(m, mb)
        wb = jnp.exp2(mb - m_new)
        if sel_row is not None:
            m_new = jnp.where(sel_row, m_new, m)
            wb = jnp.where(sel_row, wb, 0.0)
        wa = jnp.exp2(m - m_new)
        m_scr[hh] = m_new
        acc_scr[hh] = wa * acc_scr[hh] + wb * pv

    krow = lax.broadcasted_iota(jnp.int32, (kc, bs), 0)
    qcol = lax.broadcasted_iota(jnp.int32, (kc, bs), 1)

    def blocks(j0, n_blocks, own):
        units = [(u, c, hh) for u in range(n_blocks) for c in range(n_chunks)
                 for hh in range(n_heads)]
        k_rows = k_ref[0, pl.ds(pl.multiple_of(j0 * bs, bs), n_blocks * bs), :]
        wide = [_dot(k_rows, qts[hh]) for hh in range(n_heads)]
        parts = []
        for u, c, hh in units:
            s = wide[hh][u * bs + c * kc:u * bs + (c + 1) * kc]
            if own:
                s = jnp.where(krow + c * kc <= qcol, s, NEG)
            mb = jnp.max(s, axis=0, keepdims=True)
            parts.append((mb, jnp.exp2(s - mb).astype(BF16)))
        pvs = [_dot(vt_scr[j0 + u, hh, :, c * kc:(c + 1) * kc], parts[i][1])
               for i, (u, c, hh) in enumerate(units)]
        for i, (u, c, hh) in enumerate(units):
            sel_row = None if own else sel_scr[hh, pl.ds(j0 + u, 1), :] > 0.0
            merge(hh, parts[i][0], pvs[i], sel_row)

    def group_body(g, c_):
        blocks(g * MOBA_UNROLL, MOBA_UNROLL, False)
        return c_

    n_groups = qi // MOBA_UNROLL
    lax.fori_loop(0, n_groups, group_body, 0)
    done = n_groups * MOBA_UNROLL
    piece = MOBA_UNROLL // 2
    while piece >= 1:
        @pl.when((qi & piece) != 0)
        def _(done=done, piece=piece):
            blocks(done, piece, False)
        done = done + (qi & piece)
        piece //= 2
    blocks(qi, 1, True)
    outs = [acc_scr[hh, :HEAD_DIM] / acc_scr[hh, HEAD_DIM:HEAD_DIM + 1]
            for hh in range(n_heads)]
    o_ref[0] = jnp.concatenate(outs, axis=0).T.astype(BF16)


def moba_prompt(qm, kmb, vt_all, kmean, layer):
    b, t, w = qm.shape
    nb = t // MOBA_BLOCK
    n_heads = LANES // HEAD_DIM
    return pl.pallas_call(
        _moba_prompt_kernel,
        grid=(b, w // LANES, nb),
        in_specs=[
            pl.BlockSpec((1, MOBA_BLOCK, LANES), lambda bi, hp, qi: (bi, qi, hp)),
            pl.BlockSpec((1, t, LANES), lambda bi, hp, qi: (bi, 0, hp)),
            pl.BlockSpec((None, None, n_heads, HEAD_DIM, t),
                         lambda bi, hp, qi: (layer, bi, hp, 0, 0)),
            pl.BlockSpec((1, nb, LANES), lambda bi, hp, qi: (bi, 0, hp)),
        ],
        out_specs=pl.BlockSpec((1, MOBA_BLOCK, LANES), lambda bi, hp, qi: (bi, qi, hp)),
        out_shape=jax.ShapeDtypeStruct((b, t, w), BF16),
        scratch_shapes=[pltpu.VMEM((nb, n_heads, HEAD_DIM + SUM_ROWS, MOBA_BLOCK), BF16),
                        pltpu.VMEM((n_heads, nb, MOBA_BLOCK), F32),
                        pltpu.VMEM((n_heads, 1, MOBA_BLOCK), F32),
                        pltpu.VMEM((n_heads, HEAD_DIM + SUM_ROWS, MOBA_BLOCK), F32)],
        compiler_params=_cparams(("parallel", "parallel", "arbitrary")),
        name="moba_prompt",
    )(qm, kmb, vt_all, kmean)


def _ret_sample_kernel(q_ref, k_ref, v_ref, g_ref, gam_ref, gn_ref, s0_ref, o_ref, s_ref):
    k_t = k_ref[...]
    v_t = v_ref[...]
    q_t = q_ref[...]
    gam = gam_ref[...]
    acc = jnp.zeros(v_t.shape, F32)
    for dk in range(HEAD_DIM):
        s_new = gam * s0_ref[dk] + k_t[dk:dk + 1, :] * v_t
        s_ref[dk] = s_new
        acc = acc + q_t[dk:dk + 1, :] * s_new
    mu = jnp.mean(acc, axis=0, keepdims=True)
    oc = acc - mu
    var = jnp.mean(oc * oc, axis=0, keepdims=True)
    o_ref[...] = oc * lax.rsqrt(var + EPS) * gn_ref[...] * _silu(g_ref[...])


def retention_sample(q, k, v, gr, gn, state_t, layer):
    n = q.shape[0]
    heads_t = lambda a: a.reshape(n, H_RET, HEAD_DIM).transpose(1, 2, 0)
    gam = jnp.asarray(np.broadcast_to(np.exp(_ret_log_decay())[:, None, None],
                                      (H_RET, 1, n)).astype(np.float32))
    gn_t = jnp.broadcast_to(gn.reshape(H_RET, HEAD_DIM)[:, :, None], (H_RET, HEAD_DIM, n))
    vec = pl.BlockSpec((None, HEAD_DIM, n), lambda h: (h, 0, 0))
    o_t, s_new = pl.pallas_call(
        _ret_sample_kernel,
        grid=(H_RET,),
        in_specs=[vec, vec, vec, vec,
                  pl.BlockSpec((None, 1, n), lambda h: (h, 0, 0)), vec,
                  pl.BlockSpec((None, None, HEAD_DIM, HEAD_DIM, n),
                               lambda h: (layer, h, 0, 0, 0))],
        out_specs=[vec, pl.BlockSpec((None, HEAD_DIM, HEAD_DIM, n), lambda h: (h, 0, 0, 0))],
        out_shape=[jax.ShapeDtypeStruct((H_RET, HEAD_DIM, n), F32),
                   jax.ShapeDtypeStruct((H_RET, HEAD_DIM, HEAD_DIM, n), F32)],
        compiler_params=_cparams(("parallel",)),
        name="retention_sample",
    )(heads_t(q), heads_t(k), heads_t(v), heads_t(gr), gam, gn_t, state_t)
    return o_t.transpose(2, 0, 1).reshape(n, W_RET), s_new


def _moba_sample_kernel(pt_ref, q_ref, qt_ref, kn_ref, vn_ref, *refs, n_pages):
    del pt_ref
    k_refs = refs[:n_pages]
    v_refs = refs[n_pages:2 * n_pages]
    o_ref = refs[2 * n_pages]
    ppb = MOBA_BLOCK // PAGE_SIZE
    nbp = n_pages // ppb
    qscale = HEAD_DIM ** -0.5 * math.log2(math.e)
    qt = qt_ref[0] * qscale
    qcols = [jnp.broadcast_to(qt[:, h:h + 1], (HEAD_DIM, PAGE_SIZE)) for h in range(H_MOBA)]
    s_pages = []
    for p in range(n_pages):
        rows = [jnp.sum(k_refs[p][h] * qcols[h], axis=0, keepdims=True) for h in range(H_MOBA)]
        s_pages.append(jnp.concatenate(rows, axis=0))
    gates, maxes = [], []
    for j in range(nbp):
        tot = s_pages[j * ppb]
        top = s_pages[j * ppb]
        for p in range(1, ppb):
            tot = tot + s_pages[j * ppb + p]
            top = jnp.maximum(top, s_pages[j * ppb + p])
        gates.append(jnp.sum(tot, axis=1, keepdims=True))
        maxes.append(jnp.max(top, axis=1, keepdims=True))
    k_sel = min(MOBA_TOPK, nbp + 1)
    sels = []
    for j in range(nbp):
        rank = jnp.zeros(gates[j].shape, F32)
        for i in range(nbp):
            if i == j:
                continue
            ahead = (gates[i] > gates[j]) | ((gates[i] == gates[j]) & (i < j))
            rank = rank + jnp.where(ahead, 1.0, 0.0)
        sels.append(rank < k_sel)
    s_own = jnp.sum(q_ref[0] * kn_ref[0], axis=1, keepdims=True) * qscale
    m = s_own
    for j in range(nbp):
        m = jnp.maximum(m, jnp.where(sels[j], maxes[j], NEG))
    p_own = jnp.exp2(s_own - m)
    l = p_own
    accs = [jnp.zeros((HEAD_DIM, PAGE_SIZE), F32) for _ in range(H_MOBA)]
    for p in range(n_pages):
        shift = jnp.where(sels[p // ppb], m, -NEG)
        pp = jnp.exp2(s_pages[p] - shift)
        l = l + jnp.sum(pp, axis=1, keepdims=True)
        for h in range(H_MOBA):
            accs[h] = accs[h] + pp[h:h + 1, :] * v_refs[p][h]
    o_past = jnp.concatenate([jnp.sum(a.T, axis=0, keepdims=True) for a in accs], axis=0)
    o_ref[0] = (o_past + p_own * vn_ref[0]) / l


def moba_sample(q, k_new, v_new, cache_k, cache_v, page_table, layer):
    n = q.shape[0]
    n_pages = page_table.shape[1]
    vec = pl.BlockSpec((1, H_MOBA, HEAD_DIM), lambda s, pt: (s, 0, 0))
    vec_t = pl.BlockSpec((1, HEAD_DIM, H_MOBA), lambda s, pt: (s, 0, 0))

    def page_spec(p):
        return pl.BlockSpec((None, None, H_MOBA, HEAD_DIM, PAGE_SIZE),
                            lambda s, pt: (layer, pt[s, p], 0, 0, 0))

    pages = [page_spec(p) for p in range(n_pages)]
    return pl.pallas_call(
        functools.partial(_moba_sample_kernel, n_pages=n_pages),
        grid_spec=pltpu.PrefetchScalarGridSpec(
            num_scalar_prefetch=1,
            grid=(n,),
            in_specs=[vec, vec_t, vec, vec] + pages + pages,
            out_specs=vec,
        ),
        out_shape=jax.ShapeDtypeStruct((n, H_MOBA, HEAD_DIM), F32),
        compiler_params=_cparams(("arbitrary",)),
        name="moba_sample",
    )(page_table, q, q.transpose(0, 2, 1), k_new, v_new,
      *([cache_k] * n_pages), *([cache_v] * n_pages))


def _out_proj_kernel(or_ref, om_ref, x_ref, w_ref, g_ref, x1_ref, h2_ref):
    x1 = (x_ref[...] + _dot(or_ref[...], w_ref[:W_RET, :])
          + _dot(om_ref[...], w_ref[W_RET:, :]))
    x1_ref[...] = x1
    ms = jnp.mean(x1 * x1, axis=-1, keepdims=True)
    h2_ref[...] = x1 * lax.rsqrt(ms + EPS) * g_ref[...]


def out_proj(o_r, o_m, x, w_bf, g, *, tm):
    n, d = x.shape
    row = lambda i: (i, 0)
    fixed = lambda i: (0, 0)
    return pl.pallas_call(
        _out_proj_kernel,
        grid=(n // tm,),
        in_specs=[pl.BlockSpec((tm, W_RET), row), pl.BlockSpec((tm, W_MOBA), row),
                  pl.BlockSpec((tm, d), row), pl.BlockSpec(w_bf.shape, fixed),
                  pl.BlockSpec((1, d), fixed)],
        out_specs=[pl.BlockSpec((tm, d), row), pl.BlockSpec((tm, d), row)],
        out_shape=[jax.ShapeDtypeStruct((n, d), F32), jax.ShapeDtypeStruct((n, d), F32)],
        compiler_params=_cparams(("parallel",)),
        name="out_proj",
    )(o_r, o_m, x, w_bf, g)


def _ffn_kernel(or_ref, om_ref, x_ref, wo_ref, g_ref, wg_ref, wu_ref, wd_ref, o_ref,
                acc_ref, h_ref):
    f = pl.program_id(1)

    @pl.when(f == 0)
    def _():
        x1 = (x_ref[...] + _dot(or_ref[...], wo_ref[:W_RET, :])
              + _dot(om_ref[...], wo_ref[W_RET:, :]))
        acc_ref[...] = x1
        ms = jnp.mean(x1 * x1, axis=-1, keepdims=True)
        h_ref[...] = (x1 * lax.rsqrt(ms + EPS) * g_ref[...]).astype(BF16)

    h = h_ref[...]
    a = _silu(_dot(h, wg_ref[...])) * _dot(h, wu_ref[...])
    acc_ref[...] += _dot(a.astype(BF16), wd_ref[...])

    @pl.when(f == pl.num_programs(1) - 1)
    def _():
        o_ref[...] = acc_ref[...]


def _ff_tile(d_ff):
    for tf in range(FF_TILE_MAX, 0, -LANES):
        if d_ff % tf == 0:
            return tf
    return d_ff


def ffn_dense(o_r, o_m, x, w_out, g, wg, wu, wd, *, tm):
    n, d = x.shape
    d_ff = wg.shape[1]
    tf = _ff_tile(d_ff)
    row = lambda i, f: (i, 0)
    fixed = lambda i, f: (0, 0)
    return pl.pallas_call(
        _ffn_kernel,
        grid=(n // tm, d_ff // tf),
        in_specs=[pl.BlockSpec((tm, W_RET), row), pl.BlockSpec((tm, W_MOBA), row),
                  pl.BlockSpec((tm, d), row), pl.BlockSpec(w_out.shape, fixed),
                  pl.BlockSpec((1, d), fixed),
                  pl.BlockSpec((d, tf), lambda i, f: (0, f)),
                  pl.BlockSpec((d, tf), lambda i, f: (0, f)),
                  pl.BlockSpec((tf, d), lambda i, f: (f, 0))],
        out_specs=pl.BlockSpec((tm, d), row),
        out_shape=jax.ShapeDtypeStruct((n, d), F32),
        scratch_shapes=[pltpu.VMEM((tm, d), F32), pltpu.VMEM((tm, d), BF16)],
        compiler_params=_cparams(("parallel", "arbitrary")),
        name="ffn_dense",
    )(o_r, o_m, x, w_out, g, wg, wu, wd)


ROUTE_E, ROUTE_RANK, ROUTE_W = 0, 2, 4


def _router_kernel(h_ref, w_ref, route_ref, cnt_ref, carry_scr):
    @pl.when(pl.program_id(0) == 0)
    def _():
        carry_scr[...] = jnp.zeros_like(carry_scr)

    logits = jnp.dot(h_ref[...], w_ref[...], precision=HIGHEST, preferred_element_type=F32)
    tm, n_e = logits.shape
    eid = lax.broadcasted_iota(jnp.int32, logits.shape, 1)
    v1 = jnp.max(logits, axis=1, keepdims=True)
    i1 = jnp.min(jnp.where(logits == v1, eid, n_e), axis=1, keepdims=True)
    rest = jnp.where(eid == i1, -jnp.inf, logits)
    v2 = jnp.max(rest, axis=1, keepdims=True)
    i2 = jnp.min(jnp.where(rest == v2, eid, n_e), axis=1, keepdims=True)
    e2 = jnp.exp(v2 - v1)
    w1 = 1.0 / (1.0 + e2)
    w2 = e2 / (1.0 + e2)
    onehot = jnp.where((eid == i1) | (eid == i2), 1.0, 0.0)
    r = lax.broadcasted_iota(jnp.int32, (tm, tm), 0)
    c = lax.broadcasted_iota(jnp.int32, (tm, tm), 1)
    earlier = jnp.where(c < r, 1.0, 0.0).astype(BF16)
    ranks = _dot(earlier, onehot.astype(BF16)) + carry_scr[...]
    r1 = jnp.sum(jnp.where(eid == i1, ranks, 0.0), axis=1, keepdims=True)
    r2 = jnp.sum(jnp.where(eid == i2, ranks, 0.0), axis=1, keepdims=True)
    total = carry_scr[...] + jnp.sum(onehot, axis=0, keepdims=True)
    carry_scr[...] = total
    cnt_ref[...] = total
    route = jnp.zeros(logits.shape, F32)
    for k, col in enumerate((i1.astype(F32), i2.astype(F32), r1, r2, w1, w2)):
        route = jnp.where(eid == k, col, route)
    route_ref[...] = route


def router(h2, w_router, *, tm):
    n, d = h2.shape
    n_e = w_router.shape[1]
    return pl.pallas_call(
        _router_kernel,
        grid=(n // tm,),
        in_specs=[pl.BlockSpec((tm, d), lambda i: (i, 0)),
                  pl.BlockSpec((d, n_e), lambda i: (0, 0))],
        out_specs=[pl.BlockSpec((tm, n_e), lambda i: (i, 0)),
                   pl.BlockSpec((1, n_e), lambda i: (0, 0))],
        out_shape=[jax.ShapeDtypeStruct((n, n_e), F32), jax.ShapeDtypeStruct((1, n_e), F32)],
        scratch_shapes=[pltpu.VMEM((1, n_e), F32)],
        compiler_params=_cparams(("arbitrary",)),
        name="router",
    )(h2, w_router)


def _dispatch_kernel(slots_ref, h_ref, xs_in_ref, xs_ref, sem):
    del xs_in_ref
    tm = h_ref.shape[0]
    base = pl.program_id(0) * tm

    def start(r, c):
        for k in range(TOP_K):
            slot = slots_ref[k, base + r]
            pltpu.make_async_copy(h_ref.at[pl.ds(r, 1)], xs_ref.at[pl.ds(slot, 1)],
                                  sem).start(priority=k)
        return c

    lax.fori_loop(0, tm, start, 0, unroll=ROW_DMA_UNROLL)

    def wait(r, c):
        for k in range(TOP_K):
            pltpu.make_async_copy(h_ref.at[pl.ds(0, 1)], xs_ref.at[pl.ds(0, 1)], sem).wait()
        return c

    lax.fori_loop(0, tm, wait, 0, unroll=ROW_DMA_UNROLL)


def moe_dispatch(slots, h2, n_slots, *, tm):
    n, d = h2.shape
    xs0 = jnp.zeros((n_slots, d), h2.dtype)
    return pl.pallas_call(
        _dispatch_kernel,
        grid_spec=pltpu.PrefetchScalarGridSpec(
            num_scalar_prefetch=1,
            grid=(n // tm,),
            in_specs=[pl.BlockSpec((tm, d), lambda i, sl: (i, 0)),
                      pl.BlockSpec(memory_space=pl.ANY)],
            out_specs=pl.BlockSpec(memory_space=pl.ANY),
            scratch_shapes=[pltpu.SemaphoreType.DMA(())],
        ),
        out_shape=jax.ShapeDtypeStruct((n_slots, d), h2.dtype),
        input_output_aliases={2: 0},
        compiler_params=_cparams(("arbitrary",)),
        name="moe_dispatch",
    )(slots, h2, xs0)


def _experts_kernel(te_ref, nv_ref, xs_ref, wg_ref, wu_ref, wd_ref, ys_ref, acc_ref):
    del te_ref
    s = pl.program_id(0)
    f = pl.program_id(1)

    @pl.when(s < nv_ref[0])
    def _():
        @pl.when(f == 0)
        def _():
            acc_ref[...] = jnp.zeros_like(acc_ref)

        x = xs_ref[...].astype(BF16)
        a = _silu(_dot(x, wg_ref[0])) * _dot(x, wu_ref[0])
        acc_ref[...] += _dot(a.astype(BF16), wd_ref[0])

        @pl.when(f == pl.num_programs(1) - 1)
        def _():
            ys_ref[...] = acc_ref[...]

    @pl.when((s >= nv_ref[0]) & (f == 0))
    def _():
        ys_ref[...] = jnp.zeros_like(ys_ref)


def moe_experts(tile_expert, n_valid, xs, wg, wu, wd):
    n_slots, d = xs.shape
    d_ff = wg.shape[2]
    tf = _ff_tile(d_ff)
    nf = d_ff // tf

    def live(s, nv):
        return jnp.minimum(s, nv[0] - 1)

    def fcol(s, f, nv):
        return jnp.where(s < nv[0], f, nf - 1)

    return pl.pallas_call(
        _experts_kernel,
        grid_spec=pltpu.PrefetchScalarGridSpec(
            num_scalar_prefetch=2,
            grid=(n_slots // MOE_TILE, nf),
            in_specs=[pl.BlockSpec((MOE_TILE, d), lambda s, f, te, nv: (live(s, nv), 0)),
                      pl.BlockSpec((1, d, tf), lambda s, f, te, nv: (te[s], 0, fcol(s, f, nv))),
                      pl.BlockSpec((1, d, tf), lambda s, f, te, nv: (te[s], 0, fcol(s, f, nv))),
                      pl.BlockSpec((1, tf, d), lambda s, f, te, nv: (te[s], fcol(s, f, nv), 0))],
            out_specs=pl.BlockSpec((MOE_TILE, d), lambda s, f, te, nv: (s, 0)),
            scratch_shapes=[pltpu.VMEM((MOE_TILE, d), F32)],
        ),
        out_shape=jax.ShapeDtypeStruct((n_slots, d), F32),
        compiler_params=_cparams(("arbitrary", "arbitrary")),
        name="moe_experts",
    )(tile_expert, n_valid, xs, wg, wu, wd)


def _combine_kernel(slots_ref, x1_ref, route_ref, ys_ref, o_ref, g_scr, sem):
    tm = x1_ref.shape[0]
    base = pl.program_id(0) * tm

    def start(r, c):
        for k in range(TOP_K):
            slot = slots_ref[k, base + r]
            pltpu.make_async_copy(ys_ref.at[pl.ds(slot, 1)], g_scr.at[k, pl.ds(r, 1)],
                                  sem).start(priority=k)
        return c

    lax.fori_loop(0, tm, start, 0, unroll=ROW_DMA_UNROLL)

    def wait(r, c):
        for k in range(TOP_K):
            pltpu.make_async_copy(ys_ref.at[pl.ds(0, 1)], g_scr.at[0, pl.ds(0, 1)], sem).wait()
        return c

    lax.fori_loop(0, tm, wait, 0, unroll=ROW_DMA_UNROLL)
    route = route_ref[...]
    out = x1_ref[...]
    for k in range(TOP_K):
        out = out + route[:, ROUTE_W + k:ROUTE_W + k + 1] * g_scr[k]
    o_ref[...] = out


def moe_combine(slots, x1, route, ys, *, tm):
    n, d = x1.shape
    return pl.pallas_call(
        _combine_kernel,
        grid_spec=pltpu.PrefetchScalarGridSpec(
            num_scalar_prefetch=1,
            grid=(n // tm,),
            in_specs=[pl.BlockSpec((tm, d), lambda i, sl: (i, 0)),
                      pl.BlockSpec((tm, route.shape[1]), lambda i, sl: (i, 0)),
                      pl.BlockSpec(memory_space=pl.ANY)],
            out_specs=pl.BlockSpec((tm, d), lambda i, sl: (i, 0)),
            scratch_shapes=[pltpu.VMEM((TOP_K, tm, d), F32), pltpu.SemaphoreType.DMA(())],
        ),
        out_shape=jax.ShapeDtypeStruct((n, d), F32),
        compiler_params=_cparams(("arbitrary",)),
        name="moe_combine",
    )(slots, x1, route, ys)


def moe_routed(x1, h2, w_router, wg, wu, wd, *, tm):
    n, _ = x1.shape
    n_e = wg.shape[0]
    route, cnt = router(h2, w_router, tm=tm)
    counts = cnt[0].astype(jnp.int32)
    padded = (counts + MOE_TILE - 1) // MOE_TILE * MOE_TILE
    ends = jnp.cumsum(padded)
    starts = ends - padded
    experts = route[:, ROUTE_E:ROUTE_E + TOP_K].astype(jnp.int32)
    ranks = route[:, ROUTE_RANK:ROUTE_RANK + TOP_K].astype(jnp.int32)
    slots = (starts[experts] + ranks).T
    n_tiles = (TOP_K * n + MOE_TILE - 1) // MOE_TILE + n_e
    tile_start = jnp.arange(n_tiles, dtype=jnp.int32) * MOE_TILE
    tile_expert = jnp.minimum(jnp.sum(tile_start[:, None] >= ends[None, :], axis=1), n_e - 1)
    n_valid = (ends[-1] // MOE_TILE).reshape(1)
    rows = min(MOE_ROW_TILE, n)
    xs = moe_dispatch(slots, h2, n_tiles * MOE_TILE, tm=rows)
    ys = moe_experts(tile_expert.astype(jnp.int32), n_valid.astype(jnp.int32), xs, wg, wu, wd)
    return moe_combine(slots, x1, route, ys, tm=rows)


def combine_matrix(route, n_e):
    eid = jnp.arange(n_e, dtype=F32)[None, :]
    comb = jnp.zeros((route.shape[0], n_e), F32)
    for k in range(TOP_K):
        comb = comb + jnp.where(route[:, ROUTE_E + k:ROUTE_E + k + 1] == eid,
                                route[:, ROUTE_W + k:ROUTE_W + k + 1], 0.0)
    return comb


def _moe_kernel(x1_ref, h_ref, c_ref, wg_ref, wu_ref, wd_ref, o_ref, acc_ref):
    e = pl.program_id(1)
    f = pl.program_id(2)

    @pl.when((e == 0) & (f == 0))
    def _():
        acc_ref[...] = x1_ref[...]

    n_e = c_ref.shape[1]
    eid = lax.broadcasted_iota(jnp.int32, c_ref.shape, 1)
    ce = jnp.sum(jnp.where(eid == e, c_ref[...], 0.0), axis=1, keepdims=True)
    h = h_ref[...].astype(BF16)
    a = _silu(_dot(h, wg_ref[0])) * _dot(h, wu_ref[0])
    acc_ref[...] += ce * _dot(a.astype(BF16), wd_ref[0])

    @pl.when((e == pl.num_programs(1) - 1) & (f == pl.num_programs(2) - 1))
    def _():
        o_ref[...] = acc_ref[...]


def moe_dense(x1, h2, comb, wg, wu, wd, *, tm):
    n, d = x1.shape
    n_e, _, d_ff = wg.shape
    tf = _ff_tile(d_ff)
    row = lambda i, e, f: (i, 0)
    return pl.pallas_call(
        _moe_kernel,
        grid=(n // tm, n_e, d_ff // tf),
        in_specs=[pl.BlockSpec((tm, d), row), pl.BlockSpec((tm, d), row),
                  pl.BlockSpec((tm, n_e), row),
                  pl.BlockSpec((1, d, tf), lambda i, e, f: (e, 0, f)),
                  pl.BlockSpec((1, d, tf), lambda i, e, f: (e, 0, f)),
                  pl.BlockSpec((1, tf, d), lambda i, e, f: (e, f, 0))],
        out_specs=pl.BlockSpec((tm, d), row),
        out_shape=jax.ShapeDtypeStruct((n, d), F32),
        scratch_shapes=[pltpu.VMEM((tm, d), F32)],
        compiler_params=_cparams(("parallel", "arbitrary", "arbitrary")),
        name="moe_dense",
    )(x1, h2, comb, wg, wu, wd)


def _rope_tables(pos):
    half = HEAD_DIM // 2
    inv = ROPE_BASE ** (-np.arange(half, dtype=np.float64) / half)
    ang = pos.astype(np.float64)[:, None] * inv[None, :]
    cos, sin = np.cos(ang), np.sin(ang)
    reps = LANES // HEAD_DIM
    cos_t = np.tile(np.concatenate([cos, cos], axis=1), (1, reps))
    sin_t = np.tile(np.concatenate([-sin, sin], axis=1), (1, reps))
    return jnp.asarray(cos_t.astype(np.float32)), jnp.asarray(sin_t.astype(np.float32))


def _token_tile(n):
    for tm in (512, 256, 128, 64, 32, 16, 8):
        if n % tm == 0:
            return tm
    return n


def kernel(x_prompt, x_sample, cache_k, cache_v, state_ret, page_table, g_mix, w_in, ret_gn,
           q_norm_g, k_norm_g, w_out, g_ffn, w_ffn_gate, w_ffn_up, w_ffn_down, w_router,
           w_exp_gate, w_exp_up, w_exp_down):
    b, t, d = x_prompt.shape
    db, ds, _ = x_sample.shape
    depth = w_in.shape[0]
    n_phys = cache_k.shape[1]
    n_p, n_s = b * t, db * ds
    tm_p, tm_s = _token_tile(n_p), _token_tile(n_s)
    past_len = page_table.shape[1] * PAGE_SIZE

    cos_p, sin_p = _rope_tables(np.arange(t))
    cos_s, sin_s = _rope_tables(np.broadcast_to(past_len + np.arange(ds)[None, :],
                                                (db, ds)).reshape(n_s))
    head_mean = (jnp.kron(jnp.eye(H_MOBA, dtype=F32), jnp.ones((HEAD_DIM, HEAD_DIM), F32))
                 / HEAD_DIM).astype(BF16)
    cache_kt = cache_k.transpose(0, 1, 3, 4, 2)
    cache_vt = cache_v.transpose(0, 1, 3, 4, 2)
    state_t = state_ret.transpose(0, 2, 3, 4, 1)

    xp = x_prompt.reshape(n_p, d)
    xs = x_sample.reshape(n_s, d)
    sp_l, ss_l, ks_l, vs_l = [], [], [], []
    kv_t = None
    for l in range(depth):
        w_in_b = w_in[l].astype(BF16)
        w_out_b = w_out[l].astype(BF16)
        g1 = g_mix[l].reshape(1, d)
        g2 = g_ffn[l].reshape(1, d)
        qg = jnp.tile(q_norm_g[l], H_MOBA).reshape(1, W_MOBA)
        kg = jnp.tile(k_norm_g[l], H_MOBA).reshape(1, W_MOBA)
        gn = ret_gn[l].reshape(1, W_RET)

        (qr, kr, vr, gr, qm, kmb, kmean, kt_all, vt_all) = proj_in(
            xp, g1, w_in_b, cos_p, sin_p, qg, kg, head_mean, tm=tm_p,
            prompt=(l, depth, b, kv_t))
        kv_t = (kt_all, vt_all)
        seq = lambda a: a.reshape(b, t, a.shape[-1])
        o_r, s_fin = retention_prompt(seq(qr), seq(kr), seq(vr), seq(gr), gn)
        o_m = moba_prompt(seq(qm), seq(kmb), vt_all,
                          kmean.reshape(b, t // MOBA_BLOCK, W_MOBA), l)
        mix_p = (o_r.reshape(n_p, W_RET), o_m.reshape(n_p, W_MOBA))
        sp_l.append(s_fin)

        (qr, kr, vr, gr, qm, km, vm) = proj_in(
            xs, g1, w_in_b, cos_s, sin_s, qg, kg, head_mean, tm=tm_s)
        f32 = lambda a: a.astype(F32)
        o_r, s_new = retention_sample(f32(qr), f32(kr), f32(vr), f32(gr), gn, state_t, l)
        heads = lambda a: a.reshape(n_s, H_MOBA, HEAD_DIM)
        o_m = moba_sample(heads(qm), heads(km), heads(vm), cache_kt, cache_vt, page_table,
                          l).reshape(n_s, W_MOBA)
        mix_s = (o_r.astype(BF16), o_m.astype(BF16))
        ss_l.append(s_new)
        ks_l.append(km.reshape(db, ds, H_MOBA, HEAD_DIM))
        vs_l.append(vm.reshape(db, ds, H_MOBA, HEAD_DIM))

        i = l // 2
        if l % 2 == 0:
            wg, wu, wd = (w_ffn_gate[i].astype(BF16), w_ffn_up[i].astype(BF16),
                          w_ffn_down[i].astype(BF16))
            xp = ffn_dense(*mix_p, xp, w_out_b, g2, wg, wu, wd, tm=tm_p)
            xs = ffn_dense(*mix_s, xs, w_out_b, g2, wg, wu, wd, tm=tm_s)
        else:
            wg, wu, wd = (w_exp_gate[i].astype(BF16), w_exp_up[i].astype(BF16),
                          w_exp_down[i].astype(BF16))
            x1p, h2p = out_proj(*mix_p, xp, w_out_b, g2, tm=tm_p)
            x1s, h2s = out_proj(*mix_s, xs, w_out_b, g2, tm=tm_s)
            xp = moe_routed(x1p, h2p, w_router[i], wg, wu, wd, tm=tm_p)
            route_s, _ = router(h2s, w_router[i], tm=tm_s)
            xs = moe_dense(x1s, h2s, combine_matrix(route_s, wg.shape[0]), wg, wu, wd, tm=tm_s)

    token_major = lambda a: a.transpose(0, 1, 4, 2, 3)
    return (xp.reshape(b, t, d), xs.reshape(db, ds, d),
            jnp.stack(sp_l), token_major(kv_t[0]), token_major(kv_t[1]),
            jnp.stack(ss_l).transpose(0, 4, 1, 2, 3), jnp.stack(ks_l), jnp.stack(vs_l))
```
